```python
import jax, jax.numpy as jnp
from jax import lax
import numpy as np

D_MODEL = 2048
BATCH = 4
SEQ = 8192
DEPTH = 2
DEC_BATCH = 16
DEC_SEQ = 64
PAST_LEN = 4096

CHUNK = 64
MIX_W = 1024
N_BRANCH = 3
RWKV_HEAD = 64
RWKV_HEADS = MIX_W // RWKV_HEAD
DECAY_RANK = 64
AAA_RANK = 64
GATE_RANK = 160
RWKV_COLS = 3 * MIX_W + DECAY_RANK + AAA_RANK + GATE_RANK
RWKV_LN_EPS = 64e-5
MLSTM_HEADS = 4
MLSTM_HEAD = MIX_W // MLSTM_HEADS
MLSTM_CONV = 4
MLSTM_COLS = 4 * MIX_W + 2 * MLSTM_HEADS
RET_HEADS = 4
RET_HEAD = MIX_W // RET_HEADS
RET_COLS = 4 * MIX_W
ROPE_BASE = 10000.0
GATE_COLS = N_BRANCH * D_MODEL
IN_COLS = RWKV_COLS + MLSTM_COLS + RET_COLS + GATE_COLS
D_FF = 5632
FFN_CONV = 3
PLE_DIM = 256
NORM_EPS = 1e-6
HEAD_NORM_EPS = 1e-5

kernel_name = 'hybrid_rwkv7_mlstm_retention_streaming_step'


def rmsnorm(x, g):
    xf = x.astype(jnp.float32)
    y = xf * lax.rsqrt(jnp.mean(xf * xf, -1, keepdims=True) + NORM_EPS)
    return (y * g.astype(jnp.float32)).astype(x.dtype)


def head_norm(y, eps):
    yf = y.astype(jnp.float32)
    mu = jnp.mean(yf, -1, keepdims=True)
    var = jnp.mean(jnp.square(yf - mu), -1, keepdims=True)
    return (yf - mu) * lax.rsqrt(var + eps)


def causal_dwconv(x, buf, w):
    k = w.shape[0]
    t = x.shape[1]
    xp = jnp.concatenate([buf.astype(x.dtype), x], axis=1)
    y = xp[:, 0:t] * w[0]
    for j in range(1, k):
        y = y + xp[:, j:j + t] * w[j]
    return y, xp[:, t:]


def to_chunks(a, l):
    b, t, h = a.shape[:3]
    a = a.reshape((b, t // l, l, h) + a.shape[3:])
    return jnp.moveaxis(a, (1, 2), (0, 3))


def from_chunks(a):
    a = jnp.moveaxis(a, (0, 3), (1, 2))
    return a.reshape((a.shape[0], a.shape[1] * a.shape[2]) + a.shape[3:])


def rotary(u, pos):
    half = u.shape[-1] // 2
    freq = ROPE_BASE ** (-jnp.arange(half, dtype=jnp.float32) / half)
    ang = pos.astype(jnp.float32)[:, None] * freq
    cos = jnp.cos(ang)[:, None, :]
    sin = jnp.sin(ang)[:, None, :]
    u1, u2 = u[..., :half], u[..., half:]
    return jnp.concatenate([u1 * cos - u2 * sin, u1 * sin + u2 * cos], -1)


def rwkv7_mix(z, shift_buf, s0, mu, w0, w2, a0, a2, g2, k_k, k_a, r_k, ln_w, ln_b):
    b, t, _ = z.shape
    z_prev = jnp.concatenate([shift_buf[:, None].astype(z.dtype), z[:, :-1]], axis=1)
    zs = z + mu * (z_prev - z)
    splits = [MIX_W, 2 * MIX_W, 3 * MIX_W, 3 * MIX_W + DECAY_RANK, 3 * MIX_W + DECAY_RANK + AAA_RANK]
    r, k, v, zw, za, zg = jnp.split(zs, splits, axis=-1)
    w_log = -jax.nn.softplus(-(w0 + jnp.tanh(zw) @ w2)) - 0.5
    decay = jnp.exp(-jnp.exp(w_log.astype(jnp.float32)))
    a = jax.nn.sigmoid(a0 + za @ a2)
    g = jax.nn.sigmoid(zg) @ g2
    hs = lambda u: u.reshape(b, t, RWKV_HEADS, RWKV_HEAD)
    kk = hs(k * k_k).astype(jnp.float32)
    kk = kk / jnp.maximum(jnp.sqrt(jnp.sum(kk * kk, -1, keepdims=True)), 1e-12)
    k = k * (1 + (a - 1) * k_a)
    rh, kh, vh = hs(r), hs(k), hs(v)
    seq = lambda u: jnp.moveaxis(u.astype(jnp.float32), 1, 0)

    def step(s, inp):
        r_t, d_t, k_t, v_t, kk_t, a_t = inp
        sa = jnp.einsum('bhvk,bhk->bhv', s, -kk_t)
        s = (s * d_t[:, :, None, :] + sa[..., None] * (kk_t * a_t)[:, :, None, :]
             + v_t[..., None] * k_t[:, :, None, :])
        return s, jnp.einsum('bhvk,bhk->bhv', s, r_t)

    s_end, y = lax.scan(step, s0.astype(jnp.float32),
                        (seq(rh), seq(hs(decay)), seq(kh), seq(vh), seq(kk), seq(hs(a))))
    y = jnp.moveaxis(y, 0, 1)
    y = (head_norm(y, RWKV_LN_EPS).reshape(b, t, MIX_W) * ln_w + ln_b).astype(z.dtype)
    bonus = jnp.sum(rh * kh * r_k, -1, keepdims=True) * vh
    y = (y + bonus.reshape(b, t, MIX_W)) * g
    return y, z[:, -1], s_end.astype(s0.dtype)


def mlstm_chunked(q, k, v, ig, lf, c0, n0, m0):
    t = q.shape[1]
    l = min(CHUNK, t)
    mask = jnp.tril(jnp.ones((l, l), dtype=bool))

    def body(carry, inp):
        c, n, m = carry
        qc, kc, vc, ic, fc = inp
        bcum = jnp.cumsum(fc, -1)
        log_inter = bcum + m[..., None]
        log_intra = jnp.where(mask, bcum[..., :, None] - bcum[..., None, :] + ic[..., None, :], -jnp.inf)
        m_t = jnp.maximum(log_inter, jnp.max(log_intra, -1))
        w_inter = jnp.exp(log_inter - m_t)
        w_intra = jnp.exp(log_intra - m_t[..., None])
        s = jnp.einsum('bhtn,bhsn->bhts', qc, kc) * w_intra
        num = w_inter[..., None] * jnp.einsum('bhtn,bhnv->bhtv', qc, c) + jnp.einsum('bhts,bhsv->bhtv', s, vc)
        den = w_inter * jnp.einsum('bhtn,bhn->bht', qc, n) + jnp.sum(s, -1)
        h = num / jnp.maximum(jnp.abs(den), jnp.exp(-m_t))[..., None]
        w_end = w_intra[..., -1, :]
        c_new = w_inter[..., -1, None, None] * c + jnp.einsum('bhs,bhsn,bhsv->bhnv', w_end, kc, vc)
        n_new = w_inter[..., -1, None] * n + jnp.einsum('bhs,bhsn->bhn', w_end, kc)
        return (c_new, n_new, m_t[..., -1]), h

    carry0 = (c0.astype(jnp.float32), n0.astype(jnp.float32), m0.astype(jnp.float32))
    (c1, n1, m1), h = lax.scan(body, carry0, (to_chunks(q, l), to_chunks(k, l), to_chunks(v, l),
                                              to_chunks(ig, l), to_chunks(lf, l)))
    return from_chunks(h), c1.astype(c0.dtype), n1.astype(n0.dtype), m1.astype(m0.dtype)


def retention_chunked(q, k, v, r0):
    t = q.shape[1]
    l = min(CHUNK, t)
    log_gamma = jnp.log(1.0 - jnp.exp2(-5.0 - jnp.arange(RET_HEADS, dtype=jnp.float32)))
    idx = jnp.arange(l, dtype=jnp.float32)
    diff = idx[:, None] - idx[None, :]
    decay_in = jnp.where(diff >= 0, jnp.exp(log_gamma[:, None, None] * jnp.maximum(diff, 0.0)), 0.0)
    decay_q = jnp.exp(log_gamma[:, None] * (idx + 1.0))
    decay_k = jnp.exp(log_gamma[:, None] * (l - 1.0 - idx))
    decay_state = jnp.exp(log_gamma * l)

    def body(r, inp):
        qc, kc, vc = inp
        s = jnp.einsum('bhtn,bhsn->bhts', qc, kc) * decay_in
        o = jnp.einsum('bhts,bhsv->bhtv', s, vc) + decay_q[..., None] * jnp.einsum('bhtn,bhnv->bhtv', qc, r)
        r = decay_state[:, None, None] * r + jnp.einsum('bhsn,hs,bhsv->bhnv', kc, decay_k, vc)
        return r, o

    r1, o = lax.scan(body, r0.astype(jnp.float32), (to_chunks(q, l), to_chunks(k, l), to_chunks(v, l)))
    return from_chunks(o), r1.astype(r0.dtype)


def trunk_layer(x, p_i, st, lw, pos0):
    shift, s_rwkv, conv_m, c_m, n_m, m_m, r_ret, conv_f = st
    b, t, _ = x.shape
    f32 = jnp.float32
    h = rmsnorm(x, lw['norm_mix'])
    proj = h @ lw['w_in']
    z_r, z_m, z_t, z_g = jnp.split(proj, [RWKV_COLS, RWKV_COLS + MLSTM_COLS,
                                          RWKV_COLS + MLSTM_COLS + RET_COLS], axis=-1)
    o_r, shift_new, s_rwkv_new = rwkv7_mix(z_r, shift, s_rwkv, lw['rwkv_mu'], lw['rwkv_w0'], lw['rwkv_w2'],
                                           lw['rwkv_a0'], lw['rwkv_a2'], lw['rwkv_g2'], lw['rwkv_kk'],
                                           lw['rwkv_ka'], lw['rwkv_rk'], lw['rwkv_lnw'], lw['rwkv_lnb'])
    qk, v_m, o_pre, i_pre, f_pre = jnp.split(z_m, [2 * MIX_W, 3 * MIX_W, 4 * MIX_W, 4 * MIX_W + MLSTM_HEADS], axis=-1)
    qk, conv_m_new = causal_dwconv(qk, conv_m, lw['mlstm_conv'])
    q_m, k_m = jnp.split(jax.nn.silu(qk), 2, axis=-1)
    hm = lambda u: u.reshape(b, t, MLSTM_HEADS, MLSTM_HEAD).astype(f32)
    ig = (i_pre + lw['mlstm_bi']).astype(f32)
    lf = jax.nn.log_sigmoid((f_pre + lw['mlstm_bf']).astype(f32))
    h_m, c_new, n_new, m_new = mlstm_chunked(hm(q_m), hm(k_m) * (MLSTM_HEAD ** -0.5), hm(v_m), ig, lf, c_m, n_m, m_m)
    o_m = (head_norm(h_m, HEAD_NORM_EPS).reshape(b, t, MIX_W).astype(x.dtype) * lw['mlstm_nw']) * jax.nn.sigmoid(o_pre)
    q_t, k_t, v_t, g_t = jnp.split(z_t, 4, axis=-1)
    pos = pos0 + jnp.arange(t)
    hr = lambda u: u.reshape(b, t, RET_HEADS, RET_HEAD).astype(f32)
    ret, r_new = retention_chunked(rotary(hr(q_t), pos), rotary(hr(k_t), pos) * (RET_HEAD ** -0.5), hr(v_t), r_ret)
    o_t = head_norm(ret, HEAD_NORM_EPS).reshape(b, t, MIX_W).astype(x.dtype) * jax.nn.silu(g_t)
    merged = jax.nn.sigmoid(z_g[..., 0:D_MODEL]) * (o_r @ lw['w_branch'][0])
    merged = merged + jax.nn.sigmoid(z_g[..., D_MODEL:2 * D_MODEL]) * (o_m @ lw['w_branch'][1])
    merged = merged + jax.nn.sigmoid(z_g[..., 2 * D_MODEL:3 * D_MODEL]) * (o_t @ lw['w_branch'][2])
    x = x + merged @ lw['w_out']
    h2 = rmsnorm(x, lw['norm_ffn'])
    a_f, b_f = jnp.split(h2 @ lw['ffn_up'], 2, axis=-1)
    a_f, conv_f_new = causal_dwconv(a_f, conv_f, lw['ffn_conv'])
    x = x + (jax.nn.gelu(a_f, approximate=False) * b_f) @ lw['ffn_down']
    gate = jax.nn.sigmoid(rmsnorm(x, lw['norm_ple']) @ lw['ple_gate'])
    x = x + (p_i @ lw['ple_proj']) * gate
    return x, (shift_new, s_rwkv_new, conv_m_new, c_new, n_new, m_new, r_new, conv_f_new)


def run_trunk(x, p, states, weights, pos0):
    per_layer = []
    for i in range(DEPTH):
        lw = {name: w[i] for name, w in weights.items()}
        st = tuple(s[i] for s in states)
        x, st_new = trunk_layer(x, p[i], st, lw, pos0)
        per_layer.append(st_new)
    new_states = tuple(jnp.stack(c, 0) for c in zip(*per_layer))
    return x, new_states


def setup_inputs(seed: int = 0) -> dict:
    ks = iter(jax.random.split(jax.random.key(seed), 48))
    nrm = lambda shape, scale=1.0: jax.random.normal(next(ks), shape, jnp.float32) * scale
    uni = lambda shape, lo, hi: jax.random.uniform(next(ks), shape, jnp.float32, minval=lo, maxval=hi)
    return {
        'x_prompt': nrm((BATCH, SEQ, D_MODEL)),
        'x_sample': nrm((DEC_BATCH, DEC_SEQ, D_MODEL)),
        'state_rwkv_shift': nrm((DEPTH, DEC_BATCH, RWKV_COLS)),
        'state_rwkv_wkv': nrm((DEPTH, DEC_BATCH, RWKV_HEADS, RWKV_HEAD, RWKV_HEAD), 0.1),
        'state_mlstm_conv': nrm((DEPTH, DEC_BATCH, MLSTM_CONV - 1, 2 * MIX_W)),
        'state_mlstm_c': nrm((DEPTH, DEC_BATCH, MLSTM_HEADS, MLSTM_HEAD, MLSTM_HEAD), 0.1),
        'state_mlstm_n': nrm((DEPTH, DEC_BATCH, MLSTM_HEADS, MLSTM_HEAD), 0.1),
        'state_mlstm_m': nrm((DEPTH, DEC_BATCH, MLSTM_HEADS)),
        'state_ret': nrm((DEPTH, DEC_BATCH, RET_HEADS, RET_HEAD, RET_HEAD), 0.1),
        'state_ffn_conv': nrm((DEPTH, DEC_BATCH, FFN_CONV - 1, D_FF)),
        'p_prompt': nrm((DEPTH, BATCH, SEQ, PLE_DIM)),
        'p_sample': nrm((DEPTH, DEC_BATCH, DEC_SEQ, PLE_DIM)),
        'norm_mix': 1.0 + nrm((DEPTH, D_MODEL), 0.1),
        'w_in': nrm((DEPTH, D_MODEL, IN_COLS), D_MODEL ** -0.5),
        'rwkv_mu': uni((DEPTH, RWKV_COLS), 0.0, 1.0),
        'rwkv_w0': uni((DEPTH, MIX_W), -6.0, -1.0),
        'rwkv_w2': nrm((DEPTH, DECAY_RANK, MIX_W), DECAY_RANK ** -0.5),
        'rwkv_a0': nrm((DEPTH, MIX_W), 0.1),
        'rwkv_a2': nrm((DEPTH, AAA_RANK, MIX_W), AAA_RANK ** -0.5),
        'rwkv_g2': nrm((DEPTH, GATE_RANK, MIX_W), GATE_RANK ** -0.5),
        'rwkv_kk': 0.85 + nrm((DEPTH, MIX_W), 0.1),
        'rwkv_ka': 1.0 + nrm((DEPTH, MIX_W), 0.1),
        'rwkv_rk': nrm((DEPTH, RWKV_HEADS, RWKV_HEAD), 0.1),
        'rwkv_lnw': 1.0 + nrm((DEPTH, MIX_W), 0.1),
        'rwkv_lnb': nrm((DEPTH, MIX_W), 0.01),
        'mlstm_conv': nrm((DEPTH, MLSTM_CONV, 2 * MIX_W), MLSTM_CONV ** -0.5),
        'mlstm_bi': nrm((DEPTH, MLSTM_HEADS), 0.1),
        'mlstm_bf': uni((DEPTH, MLSTM_HEADS), 3.0, 6.0),
        'mlstm_nw': 1.0 + nrm((DEPTH, MIX_W), 0.1),
        'w_branch': nrm((DEPTH, N_BRANCH, MIX_W, D_MODEL), MIX_W ** -0.5),
        'w_out': nrm((DEPTH, D_MODEL, D_MODEL), D_MODEL ** -0.5),
        'norm_ffn': 1.0 + nrm((DEPTH, D_MODEL), 0.1),
        'ffn_up': nrm((DEPTH, D_MODEL, 2 * D_FF), D_MODEL ** -0.5),
        'ffn_conv': nrm((DEPTH, FFN_CONV, D_FF), FFN_CONV ** -0.5),
        'ffn_down': nrm((DEPTH, D_FF, D_MODEL), D_FF ** -0.5),
        'norm_ple': 1.0 + nrm((DEPTH, D_MODEL), 0.1),
        'ple_proj': nrm((DEPTH, PLE_DIM, D_MODEL), PLE_DIM ** -0.5),
        'ple_gate': nrm((DEPTH, D_MODEL, D_MODEL), D_MODEL ** -0.5),
        'norm_final': 1.0 + nrm((D_MODEL,), 0.1),
    }


def reference(x_prompt, x_sample, state_rwkv_shift, state_rwkv_wkv, state_mlstm_conv, state_mlstm_c,
              state_mlstm_n, state_mlstm_m, state_ret, state_ffn_conv, p_prompt, p_sample,
              norm_mix, w_in, rwkv_mu, rwkv_w0, rwkv_w2, rwkv_a0, rwkv_a2, rwkv_g2, rwkv_kk, rwkv_ka,
              rwkv_rk, rwkv_lnw, rwkv_lnb, mlstm_conv, mlstm_bi, mlstm_bf, mlstm_nw, w_branch, w_out,
              norm_ffn, ffn_up, ffn_conv, ffn_down, norm_ple, ple_proj, ple_gate, norm_final):
    weights = dict(norm_mix=norm_mix, w_in=w_in, rwkv_mu=rwkv_mu, rwkv_w0=rwkv_w0, rwkv_w2=rwkv_w2,
                   rwkv_a0=rwkv_a0, rwkv_a2=rwkv_a2, rwkv_g2=rwkv_g2, rwkv_kk=rwkv_kk, rwkv_ka=rwkv_ka,
                   rwkv_rk=rwkv_rk, rwkv_lnw=rwkv_lnw, rwkv_lnb=rwkv_lnb, mlstm_conv=mlstm_conv,
                   mlstm_bi=mlstm_bi, mlstm_bf=mlstm_bf, mlstm_nw=mlstm_nw, w_branch=w_branch, w_out=w_out,
                   norm_ffn=norm_ffn, ffn_up=ffn_up, ffn_conv=ffn_conv, ffn_down=ffn_down,
                   norm_ple=norm_ple, ple_proj=ple_proj, ple_gate=ple_gate)
    b0 = x_prompt.shape[0]
    dt = x_prompt.dtype
    zero_states = (
        jnp.zeros((DEPTH, b0, RWKV_COLS), dt),
        jnp.zeros((DEPTH, b0, RWKV_HEADS, RWKV_HEAD, RWKV_HEAD), dt),
        jnp.zeros((DEPTH, b0, MLSTM_CONV - 1, 2 * MIX_W), dt),
        jnp.zeros((DEPTH, b0, MLSTM_HEADS, MLSTM_HEAD, MLSTM_HEAD), dt),
        jnp.zeros((DEPTH, b0, MLSTM_HEADS, MLSTM_HEAD), dt),
        jnp.zeros((DEPTH, b0, MLSTM_HEADS), dt),
        jnp.zeros((DEPTH, b0, RET_HEADS, RET_HEAD, RET_HEAD), dt),
        jnp.zeros((DEPTH, b0, FFN_CONV - 1, D_FF), dt),
    )
    hp, new_p = run_trunk(x_prompt, p_prompt, zero_states, weights, 0)
    sample_states = (state_rwkv_shift, state_rwkv_wkv, state_mlstm_conv, state_mlstm_c,
                     state_mlstm_n, state_mlstm_m, state_ret, state_ffn_conv)
    hs, new_s = run_trunk(x_sample, p_sample, sample_states, weights, PAST_LEN)
    y_prompt = rmsnorm(hp, norm_final)
    y_sample = rmsnorm(hs, norm_final)
    p_shift, p_wkv, p_mconv, p_mc, p_mn, p_mm, p_ret, p_fconv = new_p
    s_shift, s_wkv, s_mconv, s_mc, s_mn, s_mm, s_ret, s_fconv = new_s
    return (y_prompt, y_sample, p_shift, p_wkv, p_mconv, p_mc, p_mn, p_mm, p_ret, p_fconv,
            s_shift, s_wkv, s_mconv, s_mc, s_mn, s_mm, s_ret, s_fconv)
```

```python
import functools

import numpy as np
import jax
import jax.numpy as jnp
from jax import lax
from jax.experimental import pallas as pl
from jax.experimental.pallas import tpu as pltpu

F32 = jnp.float32
BF16 = jnp.bfloat16

D_MODEL = 2048
CHUNK = 64
MIX_W = 1024
RWKV_HEAD = 64
RWKV_GROUP = 4
GROUP_W = RWKV_HEAD * RWKV_GROUP
N_GROUPS = MIX_W // GROUP_W
LORA_W = 64 + 64 + 160
RWKV_LN_EPS = 64e-5
M_HEADS = 4
M_HEAD = 256
MLSTM_CONV = 4
D_FF = 5632
FFN_CONV = 3
PLE_DIM = 256
PAST_LEN = 4096
ROPE_BASE = 10000.0
NORM_EPS = 1e-6
HEAD_NORM_EPS = 1e-5
HALO = 8
NEG_BIG = -1e30

GATE_BLK = 2048
RKV_BLK = 3072
MIX_BLK = 1024
TAIL_BLK = 512
TAIL_LORA = 384
IN_COLS_PAD = 17920
VMEM_LIMIT = 56 * 1024 * 1024


def _cparams(sem):
    return pltpu.CompilerParams(dimension_semantics=sem, vmem_limit_bytes=VMEM_LIMIT)


def _pick(n, prefs):
    for p in prefs:
        if n % p == 0:
            return p
    raise ValueError(f"no tile for {n}")


def _sigmoid(x):
    return 1.0 / (1.0 + jnp.exp(-x))


def _softplus(x):
    return jnp.maximum(x, 0.0) + jnp.log(1.0 + jnp.exp(-jnp.abs(x)))


def _rms(x, g):
    return x * lax.rsqrt(jnp.mean(x * x, axis=-1, keepdims=True) + NORM_EPS) * g


def _dot(a, b):
    return jnp.dot(a, b, preferred_element_type=F32)


def _dot_nt(a, b):
    return lax.dot_general(a, b, (((1,), (1,)), ((), ())), preferred_element_type=F32)


def _dot_tn(a, b):
    return lax.dot_general(a, b, (((0,), (0,)), ((), ())), preferred_element_type=F32)


def _dot_f32(a, b):
    return jnp.dot(a, b, preferred_element_type=F32, precision=lax.Precision.HIGHEST)


def _split_dot(x, w_bf16):
    hi = x.astype(BF16)
    lo = (x - hi.astype(F32)).astype(BF16)
    return _dot(hi, w_bf16) + _dot(lo, w_bf16)


def _norm_mm_kernel(x_ref, g_ref, w_ref, o_ref, xn_ref):
    @pl.when(pl.program_id(1) == 0)
    def _():
        xn_ref[...] = _rms(x_ref[...], g_ref[...]).astype(BF16)

    o_ref[...] = _dot(xn_ref[...], w_ref[...])


def _norm_matmul(x, g, w, name):
    m, d = x.shape
    n = w.shape[1]
    tm = _pick(m, (1024, 512, 256, 128, 64))
    tn = _pick(n, (512, 256, 128))
    return pl.pallas_call(
        _norm_mm_kernel,
        grid=(m // tm, n // tn),
        in_specs=[pl.BlockSpec((tm, d), lambda i, j: (i, 0)),
                  pl.BlockSpec((1, d), lambda i, j: (0, 0)),
                  pl.BlockSpec((d, tn), lambda i, j: (0, j))],
        out_specs=pl.BlockSpec((tm, tn), lambda i, j: (i, j)),
        out_shape=jax.ShapeDtypeStruct((m, n), F32),
        scratch_shapes=[pltpu.VMEM((tm, d), BF16)],
        compiler_params=_cparams(("parallel", "arbitrary")),
        name=name,
    )(x, g, w)


def _mm_res_kernel(a_ref, w_ref, x_ref, o_ref, acc_ref):
    k = pl.program_id(2)

    @pl.when(k == 0)
    def _():
        acc_ref[...] = jnp.zeros_like(acc_ref)

    acc_ref[...] += _dot(a_ref[...], w_ref[...])

    @pl.when(k == pl.num_programs(2) - 1)
    def _():
        o_ref[...] = x_ref[...] + acc_ref[...]


def _matmul_residual(a, w, x, name):
    m, kd = a.shape
    n = w.shape[1]
    tm = _pick(m, (1024, 512, 256, 128, 64))
    tn = _pick(n, (1024, 512, 256, 128))
    tk = _pick(kd, (512, 256, 128))
    return pl.pallas_call(
        _mm_res_kernel,
        grid=(m // tm, n // tn, kd // tk),
        in_specs=[pl.BlockSpec((tm, tk), lambda i, j, k: (i, k)),
                  pl.BlockSpec((tk, tn), lambda i, j, k: (k, j)),
                  pl.BlockSpec((tm, tn), lambda i, j, k: (i, j))],
        out_specs=pl.BlockSpec((tm, tn), lambda i, j, k: (i, j)),
        out_shape=jax.ShapeDtypeStruct((m, n), F32),
        scratch_shapes=[pltpu.VMEM((tm, tn), F32)],
        compiler_params=_cparams(("parallel", "parallel", "arbitrary")),
        name=name,
    )(a, w, x)


MERGE_NC = 512


def _merge_kernel(x_ref, g0_ref, g1_ref, g2_ref, o0_ref, o1_ref, o2_ref, wb_ref, wo_ref, out_ref, mg_ref):
    branches = ((g0_ref, o0_ref), (g1_ref, o1_ref), (g2_ref, o2_ref))
    for nc in range(D_MODEL // MERGE_NC):
        cs = slice(nc * MERGE_NC, (nc + 1) * MERGE_NC)
        acc = None
        for j, (g_ref, o_ref) in enumerate(branches):
            t = _sigmoid(g_ref[:, cs]) * _dot(o_ref[...], wb_ref[j, :, cs])
            acc = t if acc is None else acc + t
        mg_ref[:, cs] = acc.astype(BF16)
    for nc in range(D_MODEL // MERGE_NC):
        cs = slice(nc * MERGE_NC, (nc + 1) * MERGE_NC)
        out_ref[:, cs] = x_ref[:, cs] + _dot(mg_ref[...], wo_ref[:, cs])


def _merge(x, proj, o_r, o_m, o_t, wb, wo):
    m = x.shape[0]
    tm = _pick(m, (256, 128, 64))
    resident = dict(pipeline_mode=pl.Buffered(1))
    return pl.pallas_call(
        _merge_kernel,
        grid=(m // tm,),
        in_specs=[pl.BlockSpec((tm, D_MODEL), lambda i: (i, 0)),
                  pl.BlockSpec((tm, GATE_BLK), lambda i: (i, 0)),
                  pl.BlockSpec((tm, GATE_BLK), lambda i: (i, 1)),
                  pl.BlockSpec((tm, GATE_BLK), lambda i: (i, 2)),
                  pl.BlockSpec((tm, MIX_W), lambda i: (i, 0)),
                  pl.BlockSpec((tm, MIX_W), lambda i: (i, 0)),
                  pl.BlockSpec((tm, MIX_W), lambda i: (i, 0)),
                  pl.BlockSpec((3, MIX_W, D_MODEL), lambda i: (0, 0, 0), **resident),
                  pl.BlockSpec((D_MODEL, D_MODEL), lambda i: (0, 0), **resident)],
        out_specs=pl.BlockSpec((tm, D_MODEL), lambda i: (i, 0)),
        out_shape=jax.ShapeDtypeStruct((m, D_MODEL), F32),
        scratch_shapes=[pltpu.VMEM((tm, D_MODEL), BF16)],
        compiler_params=_cparams(("parallel",)),
        name="merge",
    )(x, proj, proj, proj, o_r, o_m, o_t, wb, wo)


def _ple_kernel(x_ref, g_ref, wg_ref, p_ref, wp_ref, gf_ref, o_ref, *, final):
    x = x_ref[...]
    gate = _sigmoid(_dot(_rms(x, g_ref[...]).astype(BF16), wg_ref[...]))
    y = x + _dot(p_ref[...].astype(BF16), wp_ref[...]) * gate
    if final:
        y = _rms(y, gf_ref[...])
    o_ref[...] = y


def _ple(x, g, wg, p, wp, gf, final):
    m = x.shape[0]
    tm = _pick(m, (512, 256, 128, 64))
    resident = dict(pipeline_mode=pl.Buffered(1))
    return pl.pallas_call(
        functools.partial(_ple_kernel, final=final),
        grid=(m // tm,),
        in_specs=[pl.BlockSpec((tm, D_MODEL), lambda i: (i, 0)),
                  pl.BlockSpec((1, D_MODEL), lambda i: (0, 0)),
                  pl.BlockSpec((D_MODEL, D_MODEL), lambda i: (0, 0), **resident),
                  pl.BlockSpec((tm, PLE_DIM), lambda i: (i, 0)),
                  pl.BlockSpec((PLE_DIM, D_MODEL), lambda i: (0, 0), **resident),
                  pl.BlockSpec((1, D_MODEL), lambda i: (0, 0))],
        out_specs=pl.BlockSpec((tm, D_MODEL), lambda i: (i, 0)),
        out_shape=jax.ShapeDtypeStruct((m, D_MODEL), F32),
        compiler_params=_cparams(("parallel",)),
        name="ple_final" if final else "ple",
    )(x, g, wg, p, wp, gf)


class _Seqs:
    def __init__(self, bp, tp, bs, ts):
        assert tp % CHUNK == 0 and ts % CHUNK == 0
        self.cp, self.cs = tp // CHUNK, ts // CHUNK
        self.npc = bp * self.cp
        self.n_chunks = self.npc + bs * self.cs
        self.bp = bp
        self.n_seq = bp + bs

    def _split(self, c):
        in_p = c < self.npc
        cc = c - self.npc
        seq = jnp.where(in_p, c // self.cp, self.bp + cc // self.cs)
        pos = jnp.where(in_p, c % self.cp, cc % self.cs)
        return in_p, seq, pos

    def seq(self, c):
        return self._split(c)[1]

    def first(self, c):
        return self._split(c)[2] == 0

    def last(self, c):
        in_p, _, pos = self._split(c)
        return pos == jnp.where(in_p, self.cp - 1, self.cs - 1)

    def rope_block(self, c):
        in_p, _, pos = self._split(c)
        return jnp.where(in_p, pos, self.cp + pos)


def _shifted(ext_ref, x, n_prev):
    ext_ref[HALO:HALO + CHUNK] = x
    return [ext_ref[HALO - j:HALO - j + CHUNK] for j in range(1, n_prev + 1)]


def _roll_halo(ext_ref):
    ext_ref[0:HALO] = ext_ref[CHUNK:CHUNK + HALO]


def _bd_mask():
    r = lax.broadcasted_iota(jnp.int32, (GROUP_W, GROUP_W), 0) // RWKV_HEAD
    c = lax.broadcasted_iota(jnp.int32, (GROUP_W, GROUP_W), 1) // RWKV_HEAD
    return r == c


def _rwkv_group(al, bt, r, k, v, lg, logd, s_bd, bdm, low_s, low_i, eye):
    def bd(x):
        return jnp.where(bdm, jnp.concatenate([x] * RWKV_GROUP, axis=0), 0.0).astype(BF16)

    e_in = jnp.exp(lg)
    e_inv = jnp.exp(-lg)
    at = al * jnp.exp(lg - logd)
    rt = r * e_in
    kh = k * e_inv
    bh = bt * e_inv
    e_end = e_in[CHUNK - 1:CHUNK, :]
    lhs = jnp.concatenate([at, rt], axis=0).astype(BF16)
    xb = _dot_nt(lhs, bd(bh))
    xk = _dot_nt(lhs, bd(kh))
    p = jnp.where(low_s, xb[:CHUNK], 0.0)
    rb = jnp.where(low_i, xb[CHUNK:], 0.0)
    q = jnp.where(low_s, xk[:CHUNK], 0.0)
    rk = jnp.where(low_i, xk[CHUNK:], 0.0)
    t = jnp.where(eye, 1.0, 0.0) + p
    a = _dot(p.astype(BF16), bd(p))
    for _ in range(4):
        res = _dot(jnp.concatenate([t, a], axis=0).astype(BF16), bd(a))
        t = t + res[:CHUNK]
        a = res[CHUNK:]
    t = t + _dot(t.astype(BF16), bd(a))
    s16 = s_bd.astype(BF16)
    xm = _dot_nt(lhs, s16)
    xv = _dot(jnp.concatenate([q, rk], axis=0).astype(BF16), bd(v))
    w = xm[:CHUNK] + xv[:CHUNK]
    u = _dot(t.astype(BF16), bd(w))
    y = xm[CHUNK:] + xv[CHUNK:] + _dot(rb.astype(BF16), bd(u))
    upd = _dot_tn(jnp.concatenate([u, v], axis=0).astype(BF16),
                  jnp.concatenate([bh * e_end, kh * e_end], axis=0).astype(BF16))
    s_new = s_bd * e_end + jnp.where(bdm, upd, 0.0)
    return y, s_new


def _rwkv_kernel(seqs, z_ref, tail_ref, sh0_ref, sht0_ref, s0_ref,
                 mu_ref, mut_ref, w0_ref, w2_ref, a0_ref, a2_ref, g2_ref, kk_ref, ka_ref, rk_ref,
                 lnw_ref, lnb_ref,
                 o_ref, sh_out_ref, sht_out_ref, s_out_ref,
                 ext_ref, extt_ref, sbd_ref):
    c = pl.program_id(0)

    @pl.when(seqs.first(c))
    def _():
        ext_ref[0:HALO] = sh0_ref[0]
        extt_ref[0:HALO] = sht0_ref[0]
        sbd_ref[...] = jnp.zeros_like(sbd_ref)
        for h in range(MIX_W // RWKV_HEAD):
            g, hh = divmod(h, RWKV_GROUP)
            blk = slice(hh * RWKV_HEAD, (hh + 1) * RWKV_HEAD)
            sbd_ref[g, blk, blk] = s0_ref[0, h]

    z = z_ref[...]
    (zp,) = _shifted(ext_ref, z, 1)
    zs = z + mu_ref[...] * (zp - z)
    _roll_halo(ext_ref)
    zt = tail_ref[...]
    (ztp,) = _shifted(extt_ref, zt, 1)
    lora = (zt + mut_ref[...] * (ztp - zt))[:, :TAIL_LORA]
    _roll_halo(extt_ref)

    r = zs[:, 0:MIX_W]
    k = zs[:, MIX_W:2 * MIX_W]
    v = zs[:, 2 * MIX_W:3 * MIX_W]
    w_log = -_softplus(-(w0_ref[...] + _dot(jnp.tanh(lora).astype(BF16), w2_ref[...]))) - 0.5
    logd = -jnp.exp(w_log)
    a = _sigmoid(a0_ref[...] + _dot(lora.astype(BF16), a2_ref[...]))
    g = _dot(_sigmoid(lora).astype(BF16), g2_ref[...])

    bdm = _bd_mask()
    ones_bd = jnp.where(bdm, 1.0, 0.0).astype(BF16)

    def head_sum(x):
        return jnp.concatenate(
            [_split_dot(x[:, i * GROUP_W:(i + 1) * GROUP_W], ones_bd) for i in range(N_GROUPS)], axis=1)

    kk = k * kk_ref[...]
    kk = kk / jnp.maximum(jnp.sqrt(head_sum(kk * kk)), 1e-12)
    k2 = k * (1.0 + (a - 1.0) * ka_ref[...])
    al = -kk
    bt = kk * a
    ti = lax.broadcasted_iota(jnp.int32, (CHUNK, CHUNK), 0)
    si = lax.broadcasted_iota(jnp.int32, (CHUNK, CHUNK), 1)
    lg = _dot_f32(jnp.where(ti >= si, 1.0, 0.0), logd)

    tt = lax.broadcasted_iota(jnp.int32, (CHUNK, GROUP_W), 0)
    ss = lax.broadcasted_iota(jnp.int32, (CHUNK, GROUP_W), 1) % RWKV_HEAD
    low_s, low_i, eye = tt > ss, tt >= ss, tt == ss
    ys = []
    for i in range(N_GROUPS):
        cs = slice(i * GROUP_W, (i + 1) * GROUP_W)
        y_g, s_new = _rwkv_group(al[:, cs], bt[:, cs], r[:, cs], k2[:, cs], v[:, cs], lg[:, cs], logd[:, cs],
                                 sbd_ref[i], bdm, low_s, low_i, eye)
        sbd_ref[i] = s_new
        ys.append(y_g)
    y = jnp.concatenate(ys, axis=1)

    inv_n = 1.0 / RWKV_HEAD
    d = y - head_sum(y) * inv_n
    var = head_sum(d * d) * inv_n
    yn = d * lax.rsqrt(var + RWKV_LN_EPS) * lnw_ref[...] + lnb_ref[...]
    bonus = head_sum(r * k2 * rk_ref[...]) * v
    o_ref[...] = ((yn + bonus) * g).astype(BF16)

    @pl.when(seqs.last(c))
    def _():
        sh_out_ref[0] = ext_ref[0:HALO]
        sht_out_ref[0] = extt_ref[0:HALO]
        for h in range(MIX_W // RWKV_HEAD):
            g_, hh = divmod(h, RWKV_GROUP)
            blk = slice(hh * RWKV_HEAD, (hh + 1) * RWKV_HEAD)
            s_out_ref[0, h] = sbd_ref[g_, blk, blk]


def _rwkv(seqs, proj, sh0, sht0, s0, pr):
    m = proj.shape[0]
    n_heads = MIX_W // RWKV_HEAD
    row = lambda w: pl.BlockSpec((1, w), lambda c: (0, 0))
    lora_w = pl.BlockSpec((TAIL_LORA, MIX_W), lambda c: (0, 0))
    st3 = lambda w: pl.BlockSpec((1, HALO, w), lambda c: (seqs.seq(c), 0, 0))
    st_s = pl.BlockSpec((1, n_heads, RWKV_HEAD, RWKV_HEAD), lambda c: (seqs.seq(c), 0, 0, 0))
    return pl.pallas_call(
        functools.partial(_rwkv_kernel, seqs),
        grid=(seqs.n_chunks,),
        in_specs=[pl.BlockSpec((CHUNK, RKV_BLK), lambda c: (c, 2)),
                  pl.BlockSpec((CHUNK, TAIL_BLK), lambda c: (c, IN_COLS_PAD // TAIL_BLK - 1)),
                  st3(RKV_BLK), st3(TAIL_BLK), st_s,
                  row(RKV_BLK), row(TAIL_BLK), row(MIX_W), lora_w, row(MIX_W), lora_w, lora_w,
                  row(MIX_W), row(MIX_W), row(MIX_W), row(MIX_W), row(MIX_W)],
        out_specs=[pl.BlockSpec((CHUNK, MIX_W), lambda c: (c, 0)), st3(RKV_BLK), st3(TAIL_BLK), st_s],
        out_shape=[jax.ShapeDtypeStruct((m, MIX_W), BF16),
                   jax.ShapeDtypeStruct((seqs.n_seq, HALO, RKV_BLK), F32),
                   jax.ShapeDtypeStruct((seqs.n_seq, HALO, TAIL_BLK), F32),
                   jax.ShapeDtypeStruct((seqs.n_seq, n_heads, RWKV_HEAD, RWKV_HEAD), F32)],
        scratch_shapes=[pltpu.VMEM((HALO + CHUNK, RKV_BLK), F32),
                        pltpu.VMEM((HALO + CHUNK, TAIL_BLK), F32),
                        pltpu.VMEM((N_GROUPS, GROUP_W, GROUP_W), F32)],
        compiler_params=_cparams(("arbitrary",)),
        name="rwkv",
    )(proj, proj, sh0, sht0, s0, pr["mu"], pr["mut"], pr["w0"], pr["w2"], pr["a0"], pr["a2"], pr["g2"],
      pr["kk"], pr["ka"], pr["rk"], pr["lnw"], pr["lnb"])


def _head_norm(x):
    mu = jnp.mean(x, axis=-1, keepdims=True)
    d = x - mu
    return d * lax.rsqrt(jnp.mean(d * d, axis=-1, keepdims=True) + HEAD_NORM_EPS)


def _mlstm_kernel(seqs, q_ref, k_ref, v_ref, op_ref, tail_ref, cv0_ref, c0_ref, n0_ref, m0_ref,
                  cw_ref, gb_ref, nw_ref,
                  o_ref, cv_out_ref, c_ref, n_ref, m_ref,
                  ext_ref):
    c = pl.program_id(0)

    @pl.when(seqs.first(c))
    def _():
        ext_ref[0:HALO] = cv0_ref[0]
        c_ref[...] = c0_ref[...]
        n_ref[...] = n0_ref[...]
        m_ref[...] = m0_ref[...]

    x0 = jnp.concatenate([q_ref[...], k_ref[...]], axis=1)
    x1, x2, x3 = _shifted(ext_ref, x0, MLSTM_CONV - 1)
    qk = x3 * cw_ref[0:1, :] + x2 * cw_ref[1:2, :] + x1 * cw_ref[2:3, :] + x0 * cw_ref[3:4, :]
    _roll_halo(ext_ref)
    qk = qk * _sigmoid(qk)
    q_all = qk[:, :MIX_W]
    k_all = qk[:, MIX_W:] * (M_HEAD ** -0.5)

    gates = tail_ref[:, TAIL_LORA:] + gb_ref[...]
    lsf = jnp.minimum(gates, 0.0) - jnp.log(1.0 + jnp.exp(-jnp.abs(gates)))
    ti = lax.broadcasted_iota(jnp.int32, (CHUNK, CHUNK), 0)
    si = lax.broadcasted_iota(jnp.int32, (CHUNK, CHUNK), 1)
    tril = ti >= si
    bcum_col = _dot_f32(jnp.where(tril, 1.0, 0.0), lsf)
    ig_rows = gates.T[0:HALO]
    bcum_rows = _dot_f32(lsf.T[0:HALO], jnp.where(ti <= si, 1.0, 0.0))
    lane = lax.broadcasted_iota(jnp.int32, (1, 128), 1)
    m_row = m_ref[0]
    m_new_row = m_row

    for h in range(M_HEADS):
        hs = slice(h * M_HEAD, (h + 1) * M_HEAD)
        q, k, v = q_all[:, hs], k_all[:, hs], v_ref[:, hs]
        bc = bcum_col[:, M_HEADS + h:M_HEADS + h + 1]
        ic = gates[:, h:h + 1]
        br = bcum_rows[M_HEADS + h:M_HEADS + h + 1, :]
        ir = ig_rows[h:h + 1, :]
        m_prev = m_row[:, h:h + 1]
        log_inter = bc + m_prev
        log_intra = jnp.where(tril, bc - br + ir, NEG_BIG)
        m_t = jnp.maximum(log_inter, jnp.max(log_intra, axis=-1, keepdims=True))
        w_inter = jnp.exp(log_inter - m_t)
        w_intra = jnp.exp(log_intra - m_t)
        q16, k16, v16 = q.astype(BF16), k.astype(BF16), v.astype(BF16)
        s = _dot_nt(q16, k16) * w_intra
        c_h = c_ref[0, h]
        n_h = n_ref[0, h:h + 1, :]
        num = w_inter * _dot(q16, c_h.astype(BF16)) + _dot(s.astype(BF16), v16)
        den = w_inter * jnp.sum(q * n_h, axis=-1, keepdims=True) + jnp.sum(s, axis=-1, keepdims=True)
        hh = num / jnp.maximum(jnp.abs(den), jnp.exp(-m_t))
        m_end = m_t[CHUNK - 1:CHUNK, :]
        w_end = jnp.exp(bc[CHUNK - 1:CHUNK, :] - bc + ic - m_end)
        g_end = w_inter[CHUNK - 1:CHUNK, :]
        c_ref[0, h] = g_end * c_h + _dot_tn(k16, (w_end * v).astype(BF16))
        n_ref[0, h:h + 1, :] = g_end * n_h + jnp.sum(w_end * k, axis=0, keepdims=True)
        m_new_row = jnp.where(lane == h, m_end, m_new_row)
        o_ref[:, hs] = (_head_norm(hh) * nw_ref[:, hs] * _sigmoid(op_ref[:, hs])).astype(BF16)

    m_ref[0] = m_new_row

    @pl.when(seqs.last(c))
    def _():
        cv_out_ref[0] = ext_ref[0:HALO]


def _mlstm(seqs, proj, cv0, c0, n0, m0, pr):
    m = proj.shape[0]
    blk = lambda j: pl.BlockSpec((CHUNK, MIX_BLK), lambda c: (c, j))
    sq = lambda c: seqs.seq(c)
    st_cv = pl.BlockSpec((1, HALO, 2 * MIX_W), lambda c: (sq(c), 0, 0))
    st_c = pl.BlockSpec((1, M_HEADS, M_HEAD, M_HEAD), lambda c: (sq(c), 0, 0, 0))
    st_n = pl.BlockSpec((1, M_HEADS, M_HEAD), lambda c: (sq(c), 0, 0))
    st_m = pl.BlockSpec((1, 1, 128), lambda c: (sq(c), 0, 0))
    return pl.pallas_call(
        functools.partial(_mlstm_kernel, seqs),
        grid=(seqs.n_chunks,),
        in_specs=[blk(9), blk(10), blk(11), blk(12),
                  pl.BlockSpec((CHUNK, TAIL_BLK), lambda c: (c, IN_COLS_PAD // TAIL_BLK - 1)),
                  st_cv, st_c, st_n, st_m,
                  pl.BlockSpec((MLSTM_CONV, 2 * MIX_W), lambda c: (0, 0)),
                  pl.BlockSpec((1, 128), lambda c: (0, 0)),
                  pl.BlockSpec((1, MIX_W), lambda c: (0, 0))],
        out_specs=[pl.BlockSpec((CHUNK, MIX_W), lambda c: (c, 0)), st_cv, st_c, st_n, st_m],
        out_shape=[jax.ShapeDtypeStruct((m, MIX_W), BF16),
                   jax.ShapeDtypeStruct((seqs.n_seq, HALO, 2 * MIX_W), F32),
                   jax.ShapeDtypeStruct((seqs.n_seq, M_HEADS, M_HEAD, M_HEAD), F32),
                   jax.ShapeDtypeStruct((seqs.n_seq, M_HEADS, M_HEAD), F32),
                   jax.ShapeDtypeStruct((seqs.n_seq, 1, 128), F32)],
        scratch_shapes=[pltpu.VMEM((HALO + CHUNK, 2 * MIX_W), F32)],
        compiler_params=_cparams(("arbitrary",)),
        name="mlstm",
    )(proj, proj, proj, proj, proj, cv0, c0, n0, m0, pr["cw"], pr["gb"], pr["nw"])


def _ret_kernel(seqs, q_ref, k_ref, v_ref, g_ref, cos_ref, sin_ref, r0_ref, o_ref, r_ref):
    c = pl.program_id(0)

    @pl.when(seqs.first(c))
    def _():
        r_ref[...] = r0_ref[...]

    cos, sin = cos_ref[...], sin_ref[...]
    half = M_HEAD // 2
    ti = lax.broadcasted_iota(jnp.int32, (CHUNK, CHUNK), 0)
    si = lax.broadcasted_iota(jnp.int32, (CHUNK, CHUNK), 1)
    diff = (ti - si).astype(F32)
    t_col = lax.broadcasted_iota(jnp.int32, (CHUNK, 1), 0).astype(F32)

    def rot(u):
        u1, u2 = u[:, :half], u[:, half:]
        return jnp.concatenate([u1 * cos - u2 * sin, u1 * sin + u2 * cos], axis=1)

    for h in range(M_HEADS):
        hs = slice(h * M_HEAD, (h + 1) * M_HEAD)
        log_gamma = float(np.log(1.0 - 2.0 ** (-5.0 - h)))
        decay_in = jnp.where(diff >= 0, jnp.exp(log_gamma * jnp.maximum(diff, 0.0)), 0.0)
        decay_q = jnp.exp(log_gamma * (t_col + 1.0))
        decay_k = jnp.exp(log_gamma * (CHUNK - 1.0 - t_col))
        decay_state = float(np.exp(log_gamma * CHUNK))
        q = rot(q_ref[:, hs])
        k = rot(k_ref[:, hs]) * (M_HEAD ** -0.5)
        v = v_ref[:, hs]
        q16, v16 = q.astype(BF16), v.astype(BF16)
        s = _dot_nt(q16, k.astype(BF16)) * decay_in
        r_h = r_ref[0, h]
        o = _dot(s.astype(BF16), v16) + decay_q * _dot(q16, r_h.astype(BF16))
        r_ref[0, h] = decay_state * r_h + _dot_tn((k * decay_k).astype(BF16), v16)
        gt = g_ref[:, hs]
        o_ref[:, hs] = (_head_norm(o) * (gt * _sigmoid(gt))).astype(BF16)


def _retention(seqs, proj, cos, sin, r0):
    m = proj.shape[0]
    blk = lambda j: pl.BlockSpec((CHUNK, MIX_BLK), lambda c: (c, j))
    rope = pl.BlockSpec((CHUNK, M_HEAD // 2), lambda c: (seqs.rope_block(c), 0))
    st_r = pl.BlockSpec((1, M_HEADS, M_HEAD, M_HEAD), lambda c: (seqs.seq(c), 0, 0, 0))
    return pl.pallas_call(
        functools.partial(_ret_kernel, seqs),
        grid=(seqs.n_chunks,),
        in_specs=[blk(13), blk(14), blk(15), blk(16), rope, rope, st_r],
        out_specs=[pl.BlockSpec((CHUNK, MIX_W), lambda c: (c, 0)), st_r],
        out_shape=[jax.ShapeDtypeStruct((m, MIX_W), BF16),
                   jax.ShapeDtypeStruct((seqs.n_seq, M_HEADS, M_HEAD, M_HEAD), F32)],
        compiler_params=_cparams(("arbitrary",)),
        name="retention",
    )(proj, proj, proj, proj, cos, sin, r0)


def _ffn_act_kernel(seqs, a_ref, b_ref, cv0_ref, cw_ref, o_ref, cv_out_ref, ext_ref):
    c = pl.program_id(0)

    @pl.when(seqs.first(c))
    def _():
        ext_ref[0:HALO] = cv0_ref[0]

    x0 = a_ref[...]
    x1, x2 = _shifted(ext_ref, x0, FFN_CONV - 1)
    a = x2 * cw_ref[0:1, :] + x1 * cw_ref[1:2, :] + x0 * cw_ref[2:3, :]
    _roll_halo(ext_ref)
    gelu = 0.5 * a * (1.0 + lax.erf(a * float(np.sqrt(0.5))))
    o_ref[...] = (gelu * b_ref[...]).astype(BF16)

    @pl.when(seqs.last(c))
    def _():
        cv_out_ref[0] = ext_ref[0:HALO]


def _ffn_act(seqs, up, cv0, cw):
    m = up.shape[0]
    st = pl.BlockSpec((1, HALO, D_FF), lambda c: (seqs.seq(c), 0, 0))
    return pl.pallas_call(
        functools.partial(_ffn_act_kernel, seqs),
        grid=(seqs.n_chunks,),
        in_specs=[pl.BlockSpec((CHUNK, D_FF), lambda c: (c, 0)),
                  pl.BlockSpec((CHUNK, D_FF), lambda c: (c, 1)),
                  st,
                  pl.BlockSpec((FFN_CONV, D_FF), lambda c: (0, 0))],
        out_specs=[pl.BlockSpec((CHUNK, D_FF), lambda c: (c, 0)), st],
        out_shape=[jax.ShapeDtypeStruct((m, D_FF), BF16),
                   jax.ShapeDtypeStruct((seqs.n_seq, HALO, D_FF), F32)],
        scratch_shapes=[pltpu.VMEM((HALO + CHUNK, D_FF), F32)],
        compiler_params=_cparams(("arbitrary",)),
        name="ffn_act",
    )(up, up, cv0, cw)


def _halo_rows(prev, n_prompt):
    k = prev.shape[1]
    return jnp.pad(prev, ((n_prompt, 0), (HALO - k, 0), (0, 0)))


def _with_prompt(state, n_prompt):
    return jnp.pad(state, ((n_prompt, 0),) + ((0, 0),) * (state.ndim - 1))


def _rope_tables(tp, ts):
    half = M_HEAD // 2
    freq = ROPE_BASE ** (-jnp.arange(half, dtype=F32) / half)
    pos = jnp.concatenate([jnp.arange(tp), PAST_LEN + jnp.arange(ts)]).astype(F32)
    ang = pos[:, None] * freq
    return jnp.cos(ang), jnp.sin(ang)


def _permute_in_cols(w):
    rw = 3 * MIX_W + LORA_W
    mb = rw
    tb = mb + 4 * MIX_W + 2 * M_HEADS
    gb = tb + 4 * MIX_W
    d = w.shape[0]
    z = lambda n: jnp.zeros((d, n), w.dtype)
    return jnp.concatenate([
        w[:, gb:gb + 3 * D_MODEL],
        w[:, 0:3 * MIX_W],
        w[:, mb:mb + 4 * MIX_W],
        w[:, tb:tb + 4 * MIX_W],
        w[:, 3 * MIX_W:rw], z(TAIL_LORA - LORA_W),
        w[:, mb + 4 * MIX_W:tb], z(TAIL_BLK - TAIL_LORA - 2 * M_HEADS)], axis=1)


def _lora_rows(w, start):
    return jnp.pad(w, ((start, TAIL_LORA - start - w.shape[0]), (0, 0))).astype(BF16)


def kernel(x_prompt, x_sample, state_rwkv_shift, state_rwkv_wkv, state_mlstm_conv, state_mlstm_c, state_mlstm_n, state_mlstm_m, state_ret, state_ffn_conv, p_prompt, p_sample, norm_mix, w_in, rwkv_mu, rwkv_w0, rwkv_w2, rwkv_a0, rwkv_a2, rwkv_g2, rwkv_kk, rwkv_ka, rwkv_rk, rwkv_lnw, rwkv_lnb, mlstm_conv, mlstm_bi, mlstm_bf, mlstm_nw, w_branch, w_out, norm_ffn, ffn_up, ffn_conv, ffn_down, norm_ple, ple_proj, ple_gate, norm_final):
    bp, tp, _ = x_prompt.shape
    bs, ts, _ = x_sample.shape
    mp, ms = bp * tp, bs * ts
    depth = w_in.shape[0]
    seqs = _Seqs(bp, tp, bs, ts)
    x = jnp.concatenate([x_prompt.reshape(mp, D_MODEL), x_sample.reshape(ms, D_MODEL)], axis=0)
    cos, sin = _rope_tables(tp, ts)
    row = lambda a: a.reshape(1, -1)
    new_states = []
    for i in range(depth):
        proj = _norm_matmul(x, row(norm_mix[i]), _permute_in_cols(w_in[i]).astype(BF16), "in_proj")

        shift = state_rwkv_shift[i][:, None, :]
        mu = rwkv_mu[i]
        rwkv_pr = dict(
            mu=row(mu[:3 * MIX_W]), mut=row(jnp.pad(mu[3 * MIX_W:], (0, TAIL_BLK - LORA_W))),
            w0=row(rwkv_w0[i]), w2=_lora_rows(rwkv_w2[i], 0),
            a0=row(rwkv_a0[i]), a2=_lora_rows(rwkv_a2[i], 64),
            g2=_lora_rows(rwkv_g2[i], 128),
            kk=row(rwkv_kk[i]), ka=row(rwkv_ka[i]), rk=row(rwkv_rk[i]),
            lnw=row(rwkv_lnw[i]), lnb=row(rwkv_lnb[i]))
        o_r, sh_new, sht_new, wkv_new = _rwkv(
            seqs, proj,
            _halo_rows(shift[:, :, :3 * MIX_W], bp),
            _halo_rows(jnp.pad(shift[:, :, 3 * MIX_W:], ((0, 0), (0, 0), (0, TAIL_BLK - LORA_W))), bp),
            _with_prompt(state_rwkv_wkv[i], bp), rwkv_pr)

        gate_bias = jnp.pad(jnp.concatenate([mlstm_bi[i], mlstm_bf[i]]), (0, 128 - 2 * M_HEADS))
        mlstm_pr = dict(cw=mlstm_conv[i], gb=row(gate_bias), nw=row(mlstm_nw[i]))
        m0 = jnp.pad(state_mlstm_m[i], ((0, 0), (0, 128 - M_HEADS)))[:, None, :]
        o_m, cv_new, c_new, n_new, m_new = _mlstm(
            seqs, proj, _halo_rows(state_mlstm_conv[i], bp), _with_prompt(state_mlstm_c[i], bp),
            _with_prompt(state_mlstm_n[i], bp), _with_prompt(m0, bp), mlstm_pr)

        o_t, ret_new = _retention(seqs, proj, cos, sin, _with_prompt(state_ret[i], bp))

        x = _merge(x, proj, o_r, o_m, o_t, w_branch[i].astype(BF16), w_out[i].astype(BF16))

        up = _norm_matmul(x, row(norm_ffn[i]), ffn_up[i].astype(BF16), "ffn_up")
        act, fcv_new = _ffn_act(seqs, up, _halo_rows(state_ffn_conv[i], bp), ffn_conv[i])
        x = _matmul_residual(act, ffn_down[i].astype(BF16), x, "ffn_down")

        p = jnp.concatenate([p_prompt[i].reshape(mp, PLE_DIM), p_sample[i].reshape(ms, PLE_DIM)], axis=0)
        x = _ple(x, row(norm_ple[i]), ple_gate[i].astype(BF16), p, ple_proj[i].astype(BF16),
                 row(norm_final), final=(i == depth - 1))

        shift_new = jnp.concatenate([sh_new[:, HALO - 1, :], sht_new[:, HALO - 1, :LORA_W]], axis=-1)
        new_states.append((shift_new, wkv_new, cv_new[:, HALO - MLSTM_CONV + 1:, :], c_new, n_new,
                           m_new[:, 0, :M_HEADS], ret_new, fcv_new[:, HALO - FFN_CONV + 1:, :]))

    stacked = [jnp.stack(s, axis=0) for s in zip(*new_states)]
    y_prompt = x[:mp].reshape(bp, tp, D_MODEL)
    y_sample = x[mp:].reshape(bs, ts, D_MODEL)
    return (y_prompt, y_sample) + tuple(s[:, :bp] for s in stacked) + tuple(s[:, bp:] for s in stacked)
```

```python
import functools

import numpy as np
import jax
import jax.numpy as jnp
from jax import lax
from jax.experimental import pallas as pl
from jax.experimental.pallas import tpu as pltpu

F32 = jnp.float32
BF16 = jnp.bfloat16

D_MODEL = 2048
CHUNK = 64
MIX_W = 1024
RWKV_HEAD = 64
RWKV_GROUP = 2
GROUP_W = RWKV_HEAD * RWKV_GROUP
N_GROUPS = MIX_W // GROUP_W
LORA_W = 64 + 64 + 160
RWKV_LN_EPS = 64e-5
M_HEADS = 4
M_HEAD = 256
MLSTM_CONV = 4
D_FF = 5632
FFN_CONV = 3
PLE_DIM = 256
PAST_LEN = 4096
ROPE_BASE = 10000.0
NORM_EPS = 1e-6
HEAD_NORM_EPS = 1e-5
HALO = 8
NEG_BIG = -1e30

GATE_BLK = 2048
RKV_BLK = 3072
MIX_BLK = 1024
TAIL_BLK = 512
ZW_OFF, ZW_W = 0, 64
ZA_OFF, ZA_W = 128, 64
ZG_OFF, ZG_W = 256, 160
MGATE_BLK, MGATE_LANE = 1, 64
IN_COLS_PAD = 17920
VMEM_LIMIT = 56 * 1024 * 1024


def _cparams(sem):
    return pltpu.CompilerParams(dimension_semantics=sem, vmem_limit_bytes=VMEM_LIMIT)


def _pick(n, prefs):
    for p in prefs:
        if n % p == 0:
            return p
    raise ValueError(f"no tile for {n}")


def _sigmoid(x):
    return 1.0 / (1.0 + jnp.exp(-x))


def _rms(x, g):
    return x * lax.rsqrt(jnp.mean(x * x, axis=-1, keepdims=True) + NORM_EPS) * g


def _dot(a, b):
    return jnp.dot(a, b, preferred_element_type=F32)


def _dot_nt(a, b):
    return lax.dot_general(a, b, (((1,), (1,)), ((), ())), preferred_element_type=F32)


def _dot_tn(a, b):
    return lax.dot_general(a, b, (((0,), (0,)), ((), ())), preferred_element_type=F32)


def _dot_f32(a, b):
    return jnp.dot(a, b, preferred_element_type=F32, precision=lax.Precision.HIGHEST)


def _split_dot_left(w_bf16, x):
    hi = x.astype(BF16)
    lo = (x - hi.astype(F32)).astype(BF16)
    return _dot(w_bf16, hi) + _dot(w_bf16, lo)


def _norm_mm_kernel(x_ref, g_ref, w_ref, o_ref, xn_ref):
    @pl.when(pl.program_id(1) == 0)
    def _():
        xn_ref[...] = _rms(x_ref[...], g_ref[...]).astype(BF16)

    o_ref[...] = _dot(xn_ref[...], w_ref[...])


def _norm_matmul(x, g, w, name):
    m, d = x.shape
    n = w.shape[1]
    tm = _pick(m, (1024, 512, 256, 128, 64))
    tn = _pick(n, (1280, 1024, 512, 256, 128))
    return pl.pallas_call(
        _norm_mm_kernel,
        grid=(m // tm, n // tn),
        in_specs=[pl.BlockSpec((tm, d), lambda i, j: (i, 0)),
                  pl.BlockSpec((1, d), lambda i, j: (0, 0)),
                  pl.BlockSpec((d, tn), lambda i, j: (0, j))],
        out_specs=pl.BlockSpec((tm, tn), lambda i, j: (i, j)),
        out_shape=jax.ShapeDtypeStruct((m, n), F32),
        scratch_shapes=[pltpu.VMEM((tm, d), BF16)],
        compiler_params=_cparams(("parallel", "arbitrary")),
        name=name,
    )(x, g, w)


def _mm_res_kernel(a_ref, w_ref, x_ref, o_ref, acc_ref):
    k = pl.program_id(2)

    @pl.when(k == 0)
    def _():
        acc_ref[...] = jnp.zeros_like(acc_ref)

    acc_ref[...] += _dot(a_ref[...], w_ref[...])

    @pl.when(k == pl.num_programs(2) - 1)
    def _():
        o_ref[...] = x_ref[...] + acc_ref[...]


def _matmul_residual(a, w, x, name):
    m, kd = a.shape
    n = w.shape[1]
    tm = _pick(m, (1024, 512, 256, 128, 64))
    tn = _pick(n, (1024, 512, 256, 128))
    tk = _pick(kd, (512, 256, 128))
    return pl.pallas_call(
        _mm_res_kernel,
        grid=(m // tm, n // tn, kd // tk),
        in_specs=[pl.BlockSpec((tm, tk), lambda i, j, k: (i, k)),
                  pl.BlockSpec((tk, tn), lambda i, j, k: (k, j)),
                  pl.BlockSpec((tm, tn), lambda i, j, k: (i, j))],
        out_specs=pl.BlockSpec((tm, tn), lambda i, j, k: (i, j)),
        out_shape=jax.ShapeDtypeStruct((m, n), F32),
        scratch_shapes=[pltpu.VMEM((tm, tn), F32)],
        compiler_params=_cparams(("parallel", "parallel", "arbitrary")),
        name=name,
    )(a, w, x)


MERGE_NC = 512


def _merge_kernel(x_ref, g0_ref, g1_ref, g2_ref, o0_ref, o1_ref, o2_ref, wb_ref, wo_ref, out_ref, mg_ref):
    branches = ((g0_ref, o0_ref), (g1_ref, o1_ref), (g2_ref, o2_ref))
    for nc in range(D_MODEL // MERGE_NC):
        cs = slice(nc * MERGE_NC, (nc + 1) * MERGE_NC)
        acc = None
        for j, (g_ref, o_ref) in enumerate(branches):
            t = _sigmoid(g_ref[:, cs]) * _dot(o_ref[...], wb_ref[j, :, cs])
            acc = t if acc is None else acc + t
        mg_ref[:, cs] = acc.astype(BF16)
    for nc in range(D_MODEL // MERGE_NC):
        cs = slice(nc * MERGE_NC, (nc + 1) * MERGE_NC)
        out_ref[:, cs] = x_ref[:, cs] + _dot(mg_ref[...], wo_ref[:, cs])


def _merge(x, proj, o_r, o_m, o_t, wb, wo):
    m = x.shape[0]
    tm = _pick(m, (256, 128, 64))
    resident = dict(pipeline_mode=pl.Buffered(1))
    return pl.pallas_call(
        _merge_kernel,
        grid=(m // tm,),
        in_specs=[pl.BlockSpec((tm, D_MODEL), lambda i: (i, 0)),
                  pl.BlockSpec((tm, GATE_BLK), lambda i: (i, 0)),
                  pl.BlockSpec((tm, GATE_BLK), lambda i: (i, 1)),
                  pl.BlockSpec((tm, GATE_BLK), lambda i: (i, 2)),
                  pl.BlockSpec((tm, MIX_W), lambda i: (i, 0)),
                  pl.BlockSpec((tm, MIX_W), lambda i: (i, 0)),
                  pl.BlockSpec((tm, MIX_W), lambda i: (i, 0)),
                  pl.BlockSpec((3, MIX_W, D_MODEL), lambda i: (0, 0, 0), **resident),
                  pl.BlockSpec((D_MODEL, D_MODEL), lambda i: (0, 0), **resident)],
        out_specs=pl.BlockSpec((tm, D_MODEL), lambda i: (i, 0)),
        out_shape=jax.ShapeDtypeStruct((m, D_MODEL), F32),
        scratch_shapes=[pltpu.VMEM((tm, D_MODEL), BF16)],
        compiler_params=_cparams(("parallel",)),
        name="merge",
    )(x, proj, proj, proj, o_r, o_m, o_t, wb, wo)


def _ple_kernel(x_ref, g_ref, wg_ref, p_ref, wp_ref, gf_ref, *o_refs, prompt_tiles):
    x = x_ref[...]
    gate = _sigmoid(_dot(_rms(x, g_ref[...]).astype(BF16), wg_ref[...]))
    y = x + _dot(p_ref[...].astype(BF16), wp_ref[...]) * gate
    if prompt_tiles is None:
        o_refs[0][...] = y
        return
    y = _rms(y, gf_ref[...])
    i = pl.program_id(0)

    @pl.when(i < prompt_tiles)
    def _():
        o_refs[0][...] = y

    @pl.when(i >= prompt_tiles)
    def _():
        o_refs[1][...] = y


def _ple(x, g, wg, p, wp, gf, split_rows=None):
    m = x.shape[0]
    tm = _pick(m if split_rows is None else np.gcd(*split_rows), (512, 256, 128, 64))
    resident = dict(pipeline_mode=pl.Buffered(1))
    if split_rows is None:
        prompt_tiles = None
        out_specs = pl.BlockSpec((tm, D_MODEL), lambda i: (i, 0))
        out_shape = jax.ShapeDtypeStruct((m, D_MODEL), F32)
    else:
        prompt_tiles = split_rows[0] // tm
        out_specs = [pl.BlockSpec((tm, D_MODEL), lambda i: (jnp.minimum(i, prompt_tiles - 1), 0)),
                     pl.BlockSpec((tm, D_MODEL), lambda i: (jnp.maximum(i - prompt_tiles, 0), 0))]
        out_shape = [jax.ShapeDtypeStruct((r, D_MODEL), F32) for r in split_rows]
    return pl.pallas_call(
        functools.partial(_ple_kernel, prompt_tiles=prompt_tiles),
        grid=(m // tm,),
        in_specs=[pl.BlockSpec((tm, D_MODEL), lambda i: (i, 0)),
                  pl.BlockSpec((1, D_MODEL), lambda i: (0, 0)),
                  pl.BlockSpec((D_MODEL, D_MODEL), lambda i: (0, 0), **resident),
                  pl.BlockSpec((tm, PLE_DIM), lambda i: (i, 0)),
                  pl.BlockSpec((PLE_DIM, D_MODEL), lambda i: (0, 0), **resident),
                  pl.BlockSpec((1, D_MODEL), lambda i: (0, 0))],
        out_specs=out_specs,
        out_shape=out_shape,
        compiler_params=_cparams(("arbitrary",)),
        name="ple" if split_rows is None else "ple_final",
    )(x, g, wg, p, wp, gf)


class _Seqs:
    def __init__(self, bp, tp, bs, ts):
        assert tp % CHUNK == 0 and ts % CHUNK == 0
        self.cp, self.cs = tp // CHUNK, ts // CHUNK
        self.npc = bp * self.cp
        self.n_chunks = self.npc + bs * self.cs
        self.bp = bp
        self.n_seq = bp + bs

    def _split(self, c):
        in_p = c < self.npc
        cc = c - self.npc
        seq = jnp.where(in_p, c // self.cp, self.bp + cc // self.cs)
        pos = jnp.where(in_p, c % self.cp, cc % self.cs)
        return in_p, seq, pos

    def seq(self, c):
        return self._split(c)[1]

    def first(self, c):
        return self._split(c)[2] == 0

    def last(self, c):
        in_p, _, pos = self._split(c)
        return pos == jnp.where(in_p, self.cp - 1, self.cs - 1)

    def rope_block(self, c):
        in_p, _, pos = self._split(c)
        return jnp.where(in_p, pos, self.cp + pos)


def _shifted(ext_ref, x, n_prev):
    ext_ref[HALO:HALO + CHUNK] = x
    return [ext_ref[HALO - j:HALO - j + CHUNK] for j in range(1, n_prev + 1)]


def _roll_halo(ext_ref):
    ext_ref[0:HALO] = ext_ref[CHUNK:CHUNK + HALO]


def _bd_mask():
    r = lax.broadcasted_iota(jnp.int32, (GROUP_W, GROUP_W), 0) // RWKV_HEAD
    c = lax.broadcasted_iota(jnp.int32, (GROUP_W, GROUP_W), 1) // RWKV_HEAD
    return r == c


def _rwkv_chunk(al, bt, r, k, v, lg, logd, sbd_ref):
    groups = range(N_GROUPS)
    bdm = _bd_mask()
    tt = lax.broadcasted_iota(jnp.int32, (CHUNK, GROUP_W), 0)
    ss = lax.broadcasted_iota(jnp.int32, (CHUNK, GROUP_W), 1) % RWKV_HEAD
    low_s, low_i = tt > ss, tt >= ss
    eye = jnp.where(tt == ss, 1.0, 0.0)

    def sl(x, i):
        return x[:, i * GROUP_W:(i + 1) * GROUP_W]

    def bd(x16):
        return jnp.where(bdm, jnp.concatenate([x16] * RWKV_GROUP, axis=0), jnp.zeros((), BF16))

    def stack16(a, b):
        return jnp.concatenate([a, b], axis=0).astype(BF16)

    e_in = jnp.exp(lg)
    e_inv = jnp.exp(-lg)
    at = al * jnp.exp(lg - logd)
    rt = r * e_in
    kh = k * e_inv
    bh = bt * e_inv
    e_end = e_in[CHUNK - 1:CHUNK, :]
    bh_end = (bh * e_end).astype(BF16)
    kh_end = (kh * e_end).astype(BF16)
    v16 = v.astype(BF16)

    lhs = [stack16(sl(at, i), sl(rt, i)) for i in groups]
    x = [_dot_nt(lhs[i], jnp.concatenate([bd(sl(bh, i).astype(BF16)), bd(sl(kh, i).astype(BF16))], axis=0))
         for i in groups]
    p = [jnp.where(low_s, x[i][:CHUNK, :GROUP_W], 0.0) for i in groups]
    q = [jnp.where(low_s, x[i][:CHUNK, GROUP_W:], 0.0) for i in groups]
    rb = [jnp.where(low_i, x[i][CHUNK:, :GROUP_W], 0.0) for i in groups]
    rk = [jnp.where(low_i, x[i][CHUNK:, GROUP_W:], 0.0) for i in groups]
    p16 = [p[i].astype(BF16) for i in groups]
    a = [_dot(p16[i], bd(p16[i])) for i in groups]
    t = [eye + p[i] for i in groups]
    for _ in range(4):
        res = [_dot(stack16(t[i], a[i]), bd(a[i].astype(BF16))) for i in groups]
        t = [t[i] + res[i][:CHUNK] for i in groups]
        a = [res[i][CHUNK:] for i in groups]
    t = [t[i] + _dot(t[i].astype(BF16), bd(a[i].astype(BF16))) for i in groups]
    xm = [_dot_nt(lhs[i], sbd_ref[i].astype(BF16)) for i in groups]
    xv = [_dot(stack16(q[i], rk[i]), bd(sl(v16, i))) for i in groups]
    u16 = [_dot(t[i].astype(BF16), bd((xm[i][:CHUNK] + xv[i][:CHUNK]).astype(BF16))).astype(BF16) for i in groups]
    y = [xm[i][CHUNK:] + xv[i][CHUNK:] + _dot(rb[i].astype(BF16), bd(u16[i])) for i in groups]
    for i in groups:
        upd = _dot_tn(jnp.concatenate([u16[i], sl(v16, i)], axis=0),
                      jnp.concatenate([sl(bh_end, i), sl(kh_end, i)], axis=0))
        sbd_ref[i] = sbd_ref[i] * sl(e_end, i) + jnp.where(bdm, upd, 0.0)
    return jnp.concatenate(y, axis=1)


def _rwkv_kernel(seqs, z_ref, tail_ref, sh0_ref, sht0_ref, s0_ref,
                 mu_ref, mut_ref, w0_ref, w2_ref, a0_ref, a2_ref, g2_ref, kk_ref, ka_ref, rk_ref,
                 lnw_ref, lnb_ref,
                 o_ref, sh_out_ref, sht_out_ref, s_out_ref,
                 ext_ref, extt_ref, sbd_ref):
    c = pl.program_id(0)

    @pl.when(seqs.first(c))
    def _():
        ext_ref[0:HALO] = sh0_ref[0]
        extt_ref[0:HALO] = sht0_ref[0]
        sbd_ref[...] = jnp.zeros_like(sbd_ref)
        for h in range(MIX_W // RWKV_HEAD):
            g, hh = divmod(h, RWKV_GROUP)
            blk = slice(hh * RWKV_HEAD, (hh + 1) * RWKV_HEAD)
            sbd_ref[g, blk, blk] = s0_ref[0, h]

    z = z_ref[...]
    (zp,) = _shifted(ext_ref, z, 1)
    zs = z + mu_ref[...] * (zp - z)
    _roll_halo(ext_ref)
    zt = tail_ref[...]
    (ztp,) = _shifted(extt_ref, zt, 1)
    lora = zt + mut_ref[...] * (ztp - zt)
    _roll_halo(extt_ref)

    r = zs[:, 0:MIX_W]
    k = zs[:, MIX_W:2 * MIX_W]
    v = zs[:, 2 * MIX_W:3 * MIX_W]
    zw = lora[:, ZW_OFF:ZW_OFF + 128]
    za = lora[:, ZA_OFF:ZA_OFF + 128]
    zg = lora[:, ZG_OFF:ZG_OFF + 256]
    logd = -float(np.exp(-0.5)) * _sigmoid(w0_ref[...] + _dot(jnp.tanh(zw).astype(BF16), w2_ref[...]))
    a = _sigmoid(a0_ref[...] + _dot(za.astype(BF16), a2_ref[...]))
    g = _dot(_sigmoid(zg).astype(BF16), g2_ref[...])

    ones_bd = jnp.where(_bd_mask(), 1.0, 0.0).astype(BF16)

    def head_sum(x):
        x16 = x.astype(BF16)
        return jnp.concatenate(
            [_dot(x16[:, i * GROUP_W:(i + 1) * GROUP_W], ones_bd) for i in range(N_GROUPS)], axis=1)

    kk = k * kk_ref[...]
    kk = kk * lax.rsqrt(jnp.maximum(head_sum(kk * kk), 1e-24))
    k2 = k * (1.0 + (a - 1.0) * ka_ref[...])
    al = -kk
    bt = kk * a
    ti = lax.broadcasted_iota(jnp.int32, (CHUNK, CHUNK), 0)
    si = lax.broadcasted_iota(jnp.int32, (CHUNK, CHUNK), 1)
    lg = _split_dot_left(jnp.where(ti >= si, 1.0, 0.0).astype(BF16), logd)

    y = _rwkv_chunk(al, bt, r, k2, v, lg, logd, sbd_ref)

    inv_n = 1.0 / RWKV_HEAD
    d = y - head_sum(y) * inv_n
    var = head_sum(d * d) * inv_n
    yn = d * lax.rsqrt(var + RWKV_LN_EPS) * lnw_ref[...] + lnb_ref[...]
    bonus = head_sum(r * k2 * rk_ref[...]) * v
    o_ref[...] = ((yn + bonus) * g).astype(BF16)

    @pl.when(seqs.last(c))
    def _():
        sh_out_ref[0] = ext_ref[0:HALO]
        sht_out_ref[0] = extt_ref[0:HALO]
        for h in range(MIX_W // RWKV_HEAD):
            g_, hh = divmod(h, RWKV_GROUP)
            blk = slice(hh * RWKV_HEAD, (hh + 1) * RWKV_HEAD)
            s_out_ref[0, h] = sbd_ref[g_, blk, blk]


def _rwkv(seqs, proj, sh0, sht0, s0, pr):
    m = proj.shape[0]
    n_heads = MIX_W // RWKV_HEAD
    row = lambda w: pl.BlockSpec((1, w), lambda c: (0, 0))
    lora_w = lambda k: pl.BlockSpec((k, MIX_W), lambda c: (0, 0))
    st3 =lambda w: pl.BlockSpec((1, HALO, w), lambda c: (seqs.seq(c), 0, 0))
    st_s = pl.BlockSpec((1, n_heads, RWKV_HEAD, RWKV_HEAD), lambda c: (seqs.seq(c), 0, 0, 0))
    return pl.pallas_call(
        functools.partial(_rwkv_kernel, seqs),
        grid=(seqs.n_chunks,),
        in_specs=[pl.BlockSpec((CHUNK, RKV_BLK), lambda c: (c, 2)),
                  pl.BlockSpec((CHUNK, TAIL_BLK), lambda c: (c, IN_COLS_PAD // TAIL_BLK - 1)),
                  st3(RKV_BLK), st3(TAIL_BLK), st_s,
                  row(RKV_BLK), row(TAIL_BLK), row(MIX_W), lora_w(128), row(MIX_W), lora_w(128), lora_w(256),
                  row(MIX_W), row(MIX_W), row(MIX_W), row(MIX_W), row(MIX_W)],
        out_specs=[pl.BlockSpec((CHUNK, MIX_W), lambda c: (c, 0)), st3(RKV_BLK), st3(TAIL_BLK), st_s],
        out_shape=[jax.ShapeDtypeStruct((m, MIX_W), BF16),
                   jax.ShapeDtypeStruct((seqs.n_seq, HALO, RKV_BLK), F32),
                   jax.ShapeDtypeStruct((seqs.n_seq, HALO, TAIL_BLK), F32),
                   jax.ShapeDtypeStruct((seqs.n_seq, n_heads, RWKV_HEAD, RWKV_HEAD), F32)],
        scratch_shapes=[pltpu.VMEM((HALO + CHUNK, RKV_BLK), F32),
                        pltpu.VMEM((HALO + CHUNK, TAIL_BLK), F32),
                        pltpu.VMEM((N_GROUPS, GROUP_W, GROUP_W), F32)],
        compiler_params=_cparams(("arbitrary",)),
        name="rwkv",
    )(proj, proj, sh0, sht0, s0, pr["mu"], pr["mut"], pr["w0"], pr["w2"], pr["a0"], pr["a2"], pr["g2"],
      pr["kk"], pr["ka"], pr["rk"], pr["lnw"], pr["lnb"])


def _head_norm(x):
    mu = jnp.mean(x, axis=-1, keepdims=True)
    d = x - mu
    return d * lax.rsqrt(jnp.mean(d * d, axis=-1, keepdims=True) + HEAD_NORM_EPS)


def _mlstm_kernel(seqs, q_ref, k_ref, v_ref, op_ref, tail_ref, cv0_ref, c0_ref, n0_ref, m0_ref,
                  cw_ref, gb_ref, nw_ref,
                  o_ref, cv_out_ref, c_ref, n_ref, m_ref,
                  ext_ref):
    c = pl.program_id(0)

    @pl.when(seqs.first(c))
    def _():
        ext_ref[0:HALO] = cv0_ref[0]
        c_ref[...] = c0_ref[...]
        n_ref[...] = n0_ref[...]
        m_ref[...] = m0_ref[...]

    x0 = jnp.concatenate([q_ref[...], k_ref[...]], axis=1)
    x1, x2, x3 = _shifted(ext_ref, x0, MLSTM_CONV - 1)
    qk = x3 * cw_ref[0:1, :] + x2 * cw_ref[1:2, :] + x1 * cw_ref[2:3, :] + x0 * cw_ref[3:4, :]
    _roll_halo(ext_ref)
    qk = qk * _sigmoid(qk)
    q_all = qk[:, :MIX_W]
    k_all = qk[:, MIX_W:] * (M_HEAD ** -0.5)

    gates = tail_ref[:, MGATE_BLK * 128:(MGATE_BLK + 1) * 128] + gb_ref[...]
    lsf = jnp.minimum(gates, 0.0) - jnp.log(1.0 + jnp.exp(-jnp.abs(gates)))
    ti = lax.broadcasted_iota(jnp.int32, (CHUNK, CHUNK), 0)
    si = lax.broadcasted_iota(jnp.int32, (CHUNK, CHUNK), 1)
    tril = ti >= si
    bcum_col = _dot_f32(jnp.where(tril, 1.0, 0.0), lsf)
    ig_rows = gates.T[MGATE_LANE:MGATE_LANE + HALO]
    bcum_rows = _dot_f32(lsf.T[MGATE_LANE:MGATE_LANE + HALO], jnp.where(ti <= si, 1.0, 0.0))
    lane = lax.broadcasted_iota(jnp.int32, (1, 128), 1)
    m_row = m_ref[0]
    heads = range(M_HEADS)
    hsl = [slice(h * M_HEAD, (h + 1) * M_HEAD) for h in heads]

    bc = [bcum_col[:, MGATE_LANE + M_HEADS + h:MGATE_LANE + M_HEADS + h + 1] for h in heads]
    ic = [gates[:, MGATE_LANE + h:MGATE_LANE + h + 1] for h in heads]
    log_inter = [bc[h] + m_row[:, h:h + 1] for h in heads]
    log_intra = [jnp.where(tril, bc[h] - bcum_rows[M_HEADS + h:M_HEADS + h + 1, :] + ig_rows[h:h + 1, :], NEG_BIG)
                 for h in heads]
    m_t = [jnp.maximum(log_inter[h], jnp.max(log_intra[h], axis=-1, keepdims=True)) for h in heads]
    w_inter = [jnp.exp(log_inter[h] - m_t[h]) for h in heads]
    w_intra = [jnp.exp(log_intra[h] - m_t[h]) for h in heads]
    m_end = [m_t[h][CHUNK - 1:CHUNK, :] for h in heads]
    w_end = [jnp.exp(bc[h][CHUNK - 1:CHUNK, :] - bc[h] + ic[h] - m_end[h]) for h in heads]
    g_end = [w_inter[h][CHUNK - 1:CHUNK, :] for h in heads]
    q = [q_all[:, hsl[h]] for h in heads]
    k = [k_all[:, hsl[h]] for h in heads]
    v = [v_ref[:, hsl[h]] for h in heads]
    q16 = [q[h].astype(BF16) for h in heads]
    k16 = [k[h].astype(BF16) for h in heads]
    v16 = [v[h].astype(BF16) for h in heads]
    s = [_dot_nt(q16[h], k16[h]) * w_intra[h] for h in heads]
    qc = [_dot(q16[h], c_ref[0, h].astype(BF16)) for h in heads]
    sv = [_dot(s[h].astype(BF16), v16[h]) for h in heads]
    kv = [_dot_tn(k16[h], (w_end[h] * v[h]).astype(BF16)) for h in heads]
    m_new_row = m_row
    for h in heads:
        n_h = n_ref[0, h:h + 1, :]
        num = w_inter[h] * qc[h] + sv[h]
        den = w_inter[h] * jnp.sum(q[h] * n_h, axis=-1, keepdims=True) + jnp.sum(s[h], axis=-1, keepdims=True)
        hh = num / jnp.maximum(jnp.abs(den), jnp.exp(-m_t[h]))
        c_ref[0, h] = g_end[h] * c_ref[0, h] + kv[h]
        n_ref[0, h:h + 1, :] = g_end[h] * n_h + jnp.sum(w_end[h] * k[h], axis=0, keepdims=True)
        m_new_row = jnp.where(lane == h, m_end[h], m_new_row)
        o_ref[:, hsl[h]] = (_head_norm(hh) * nw_ref[:, hsl[h]] * _sigmoid(op_ref[:, hsl[h]])).astype(BF16)

    m_ref[0] = m_new_row

    @pl.when(seqs.last(c))
    def _():
        cv_out_ref[0] = ext_ref[0:HALO]


def _mlstm(seqs, proj, cv0, c0, n0, m0, pr):
    m = proj.shape[0]
    blk = lambda j: pl.BlockSpec((CHUNK, MIX_BLK), lambda c: (c, j))
    sq = lambda c: seqs.seq(c)
    st_cv = pl.BlockSpec((1, HALO, 2 * MIX_W), lambda c: (sq(c), 0, 0))
    st_c = pl.BlockSpec((1, M_HEADS, M_HEAD, M_HEAD), lambda c: (sq(c), 0, 0, 0))
    st_n = pl.BlockSpec((1, M_HEADS, M_HEAD), lambda c: (sq(c), 0, 0))
    st_m = pl.BlockSpec((1, 1, 128), lambda c: (sq(c), 0, 0))
    return pl.pallas_call(
        functools.partial(_mlstm_kernel, seqs),
        grid=(seqs.n_chunks,),
        in_specs=[blk(9), blk(10), blk(11), blk(12),
                  pl.BlockSpec((CHUNK, TAIL_BLK), lambda c: (c, IN_COLS_PAD // TAIL_BLK - 1)),
                  st_cv, st_c, st_n, st_m,
                  pl.BlockSpec((MLSTM_CONV, 2 * MIX_W), lambda c: (0, 0)),
                  pl.BlockSpec((1, 128), lambda c: (0, 0)),
                  pl.BlockSpec((1, MIX_W), lambda c: (0, 0))],
        out_specs=[pl.BlockSpec((CHUNK, MIX_W), lambda c: (c, 0)), st_cv, st_c, st_n, st_m],
        out_shape=[jax.ShapeDtypeStruct((m, MIX_W), BF16),
                   jax.ShapeDtypeStruct((seqs.n_seq, HALO, 2 * MIX_W), F32),
                   jax.ShapeDtypeStruct((seqs.n_seq, M_HEADS, M_HEAD, M_HEAD), F32),
                   jax.ShapeDtypeStruct((seqs.n_seq, M_HEADS, M_HEAD), F32),
                   jax.ShapeDtypeStruct((seqs.n_seq, 1, 128), F32)],
        scratch_shapes=[pltpu.VMEM((HALO + CHUNK, 2 * MIX_W), F32)],
        compiler_params=_cparams(("arbitrary",)),
        name="mlstm",
    )(proj, proj, proj, proj, proj, cv0, c0, n0, m0, pr["cw"], pr["gb"], pr["nw"])


def _ret_kernel(seqs, q_ref, k_ref, v_ref, g_ref, cos_ref, sin_ref, r0_ref, o_ref, r_ref):
    c = pl.program_id(0)

    @pl.when(seqs.first(c))
    def _():
        r_ref[...] = r0_ref[...]

    cos, sin = cos_ref[...], sin_ref[...]
    half = M_HEAD // 2
    ti = lax.broadcasted_iota(jnp.int32, (CHUNK, CHUNK), 0)
    si = lax.broadcasted_iota(jnp.int32, (CHUNK, CHUNK), 1)
    diff = (ti - si).astype(F32)
    t_col = lax.broadcasted_iota(jnp.int32, (CHUNK, 1), 0).astype(F32)

    def rot(u):
        u1, u2 = u[:, :half], u[:, half:]
        return jnp.concatenate([u1 * cos - u2 * sin, u1 * sin + u2 * cos], axis=1)

    heads = range(M_HEADS)
    hsl = [slice(h * M_HEAD, (h + 1) * M_HEAD) for h in heads]
    log_gamma = [float(np.log(1.0 - 2.0 ** (-5.0 - h))) for h in heads]
    q16 = [rot(q_ref[:, hsl[h]]).astype(BF16) for h in heads]
    k = [rot(k_ref[:, hsl[h]]) * (M_HEAD ** -0.5) for h in heads]
    v16 = [v_ref[:, hsl[h]].astype(BF16) for h in heads]
    s = [_dot_nt(q16[h], k[h].astype(BF16))
         * jnp.where(diff >= 0, jnp.exp(log_gamma[h] * jnp.maximum(diff, 0.0)), 0.0) for h in heads]
    qr = [_dot(q16[h], r_ref[0, h].astype(BF16)) for h in heads]
    sv = [_dot(s[h].astype(BF16), v16[h]) for h in heads]
    kv = [_dot_tn((k[h] * jnp.exp(log_gamma[h] * (CHUNK - 1.0 - t_col))).astype(BF16), v16[h]) for h in heads]
    for h in heads:
        o = sv[h] + jnp.exp(log_gamma[h] * (t_col + 1.0)) * qr[h]
        r_ref[0, h] = float(np.exp(log_gamma[h] * CHUNK)) * r_ref[0, h] + kv[h]
        gt = g_ref[:, hsl[h]]
        o_ref[:, hsl[h]] = (_head_norm(o) * (gt * _sigmoid(gt))).astype(BF16)


def _retention(seqs, proj, cos, sin, r0):
    m = proj.shape[0]
    blk = lambda j: pl.BlockSpec((CHUNK, MIX_BLK), lambda c: (c, j))
    rope = pl.BlockSpec((CHUNK, M_HEAD // 2), lambda c: (seqs.rope_block(c), 0))
    st_r = pl.BlockSpec((1, M_HEADS, M_HEAD, M_HEAD), lambda c: (seqs.seq(c), 0, 0, 0))
    return pl.pallas_call(
        functools.partial(_ret_kernel, seqs),
        grid=(seqs.n_chunks,),
        in_specs=[blk(13), blk(14), blk(15), blk(16), rope, rope, st_r],
        out_specs=[pl.BlockSpec((CHUNK, MIX_W), lambda c: (c, 0)), st_r],
        out_shape=[jax.ShapeDtypeStruct((m, MIX_W), BF16),
                   jax.ShapeDtypeStruct((seqs.n_seq, M_HEADS, M_HEAD, M_HEAD), F32)],
        compiler_params=_cparams(("arbitrary",)),
        name="retention",
    )(proj, proj, proj, proj, cos, sin, r0)


def _ffn_act_kernel(seqs, a_ref, b_ref, cv0_ref, cw_ref, o_ref, cv_out_ref, ext_ref):
    c = pl.program_id(0)

    @pl.when(seqs.first(c))
    def _():
        ext_ref[0:HALO] = cv0_ref[0]

    x0 = a_ref[...]
    x1, x2 = _shifted(ext_ref, x0, FFN_CONV - 1)
    a = x2 * cw_ref[0:1, :] + x1 * cw_ref[1:2, :] + x0 * cw_ref[2:3, :]
    _roll_halo(ext_ref)
    gelu = 0.5 * a * (1.0 + lax.erf(a * float(np.sqrt(0.5))))
    o_ref[...] = (gelu * b_ref[...]).astype(BF16)

    @pl.when(seqs.last(c))
    def _():
        cv_out_ref[0] = ext_ref[0:HALO]


def _ffn_act(seqs, up, cv0, cw):
    m = up.shape[0]
    st = pl.BlockSpec((1, HALO, D_FF), lambda c: (seqs.seq(c), 0, 0))
    return pl.pallas_call(
        functools.partial(_ffn_act_kernel, seqs),
        grid=(seqs.n_chunks,),
        in_specs=[pl.BlockSpec((CHUNK, D_FF), lambda c: (c, 0)),
                  pl.BlockSpec((CHUNK, D_FF), lambda c: (c, 1)),
                  st,
                  pl.BlockSpec((FFN_CONV, D_FF), lambda c: (0, 0))],
        out_specs=[pl.BlockSpec((CHUNK, D_FF), lambda c: (c, 0)), st],
        out_shape=[jax.ShapeDtypeStruct((m, D_FF), BF16),
                   jax.ShapeDtypeStruct((seqs.n_seq, HALO, D_FF), F32)],
        scratch_shapes=[pltpu.VMEM((HALO + CHUNK, D_FF), F32)],
        compiler_params=_cparams(("arbitrary",)),
        name="ffn_act",
    )(up, up, cv0, cw)


def _halo_rows(prev, n_prompt):
    k = prev.shape[1]
    return jnp.pad(prev, ((n_prompt, 0), (HALO - k, 0), (0, 0)))


def _with_prompt(state, n_prompt):
    return jnp.pad(state, ((n_prompt, 0),) + ((0, 0),) * (state.ndim - 1))


def _rope_tables(tp, ts):
    half = M_HEAD // 2
    freq = ROPE_BASE ** (-jnp.arange(half, dtype=F32) / half)
    pos = jnp.concatenate([jnp.arange(tp), PAST_LEN + jnp.arange(ts)]).astype(F32)
    ang = pos[:, None] * freq
    return jnp.cos(ang), jnp.sin(ang)


def _tail_layout(lora, gates=None):
    lead = lora.shape[:-1]
    z = lambda n: jnp.zeros(lead + (n,), lora.dtype)
    mid = z(128 - ZA_W) if gates is None else jnp.concatenate(
        [z(MGATE_LANE - ZA_W), gates, z(128 - MGATE_LANE - gates.shape[-1])], axis=-1)
    return jnp.concatenate([lora[..., :ZW_W], z(128 - ZW_W), lora[..., ZW_W:ZW_W + ZA_W], mid,
                            lora[..., ZW_W + ZA_W:], z(256 - ZG_W)], axis=-1)


def _tail_lora(t):
    return jnp.concatenate([t[..., ZW_OFF:ZW_OFF + ZW_W], t[..., ZA_OFF:ZA_OFF + ZA_W], t[..., ZG_OFF:ZG_OFF + ZG_W]],
                           axis=-1)


def _permute_in_cols(w):
    rw = 3 * MIX_W + LORA_W
    mb = rw
    tb = mb + 4 * MIX_W + 2 * M_HEADS
    gb = tb + 4 * MIX_W
    return jnp.concatenate([
        w[:, gb:gb + 3 * D_MODEL],
        w[:, 0:3 * MIX_W],
        w[:, mb:mb + 4 * MIX_W],
        w[:, tb:tb + 4 * MIX_W],
        _tail_layout(w[:, 3 * MIX_W:rw], w[:, mb + 4 * MIX_W:tb])], axis=1)


def _lora_rows(w, k):
    return jnp.pad(w, ((0, k - w.shape[0]), (0, 0))).astype(BF16)


def kernel(x_prompt, x_sample, state_rwkv_shift, state_rwkv_wkv, state_mlstm_conv, state_mlstm_c, state_mlstm_n, state_mlstm_m, state_ret, state_ffn_conv, p_prompt, p_sample, norm_mix, w_in, rwkv_mu, rwkv_w0, rwkv_w2, rwkv_a0, rwkv_a2, rwkv_g2, rwkv_kk, rwkv_ka, rwkv_rk, rwkv_lnw, rwkv_lnb, mlstm_conv, mlstm_bi, mlstm_bf, mlstm_nw, w_branch, w_out, norm_ffn, ffn_up, ffn_conv, ffn_down, norm_ple, ple_proj, ple_gate, norm_final):
    bp, tp, _ = x_prompt.shape
    bs, ts, _ = x_sample.shape
    mp, ms = bp * tp, bs * ts
    depth = w_in.shape[0]
    seqs = _Seqs(bp, tp, bs, ts)
    x = jnp.concatenate([x_prompt.reshape(mp, D_MODEL), x_sample.reshape(ms, D_MODEL)], axis=0)
    cos, sin = _rope_tables(tp, ts)
    row = lambda a: a.reshape(1, -1)
    new_states = []
    for i in range(depth):
        proj = _norm_matmul(x, row(norm_mix[i]), _permute_in_cols(w_in[i].astype(BF16)), "in_proj")

        shift = state_rwkv_shift[i][:, None, :]
        mu = rwkv_mu[i]
        rwkv_pr = dict(
            mu=row(mu[:3 * MIX_W]), mut=row(_tail_layout(mu[3 * MIX_W:])),
            w0=row(rwkv_w0[i]), w2=_lora_rows(rwkv_w2[i], 128),
            a0=row(rwkv_a0[i]), a2=_lora_rows(rwkv_a2[i], 128),
            g2=_lora_rows(rwkv_g2[i], 256),
            kk=row(rwkv_kk[i]), ka=row(rwkv_ka[i]), rk=row(rwkv_rk[i]),
            lnw=row(rwkv_lnw[i]), lnb=row(rwkv_lnb[i]))
        o_r, sh_new, sht_new, wkv_new = _rwkv(
            seqs, proj,
            _halo_rows(shift[:, :, :3 * MIX_W], bp),
            _halo_rows(_tail_layout(shift[:, :, 3 * MIX_W:]), bp),
            _with_prompt(state_rwkv_wkv[i], bp), rwkv_pr)

        gate_bias = jnp.pad(jnp.concatenate([mlstm_bi[i], mlstm_bf[i]]),
                            (MGATE_LANE, 128 - MGATE_LANE - 2 * M_HEADS))
        mlstm_pr = dict(cw=mlstm_conv[i], gb=row(gate_bias), nw=row(mlstm_nw[i]))
        m0 = jnp.pad(state_mlstm_m[i], ((0, 0), (0, 128 - M_HEADS)))[:, None, :]
        o_m, cv_new, c_new, n_new, m_new = _mlstm(
            seqs, proj, _halo_rows(state_mlstm_conv[i], bp), _with_prompt(state_mlstm_c[i], bp),
            _with_prompt(state_mlstm_n[i], bp), _with_prompt(m0, bp), mlstm_pr)

        o_t, ret_new = _retention(seqs, proj, cos, sin, _with_prompt(state_ret[i], bp))

        x = _merge(x, proj, o_r, o_m, o_t, w_branch[i].astype(BF16), w_out[i].astype(BF16))

        up = _norm_matmul(x, row(norm_ffn[i]), ffn_up[i].astype(BF16), "ffn_up")
        act, fcv_new = _ffn_act(seqs, up, _halo_rows(state_ffn_conv[i], bp), ffn_conv[i])
        x = _matmul_residual(act, ffn_down[i].astype(BF16), x, "ffn_down")

        p = jnp.concatenate([p_prompt[i].reshape(mp, PLE_DIM), p_sample[i].reshape(ms, PLE_DIM)], axis=0)
        x = _ple(x, row(norm_ple[i]), ple_gate[i].astype(BF16), p, ple_proj[i].astype(BF16),
                 row(norm_final), split_rows=(mp, ms) if i == depth - 1 else None)

        shift_new = jnp.concatenate([sh_new[:, HALO - 1, :], _tail_lora(sht_new[:, HALO - 1, :])], axis=-1)
        new_states.append((shift_new, wkv_new, cv_new[:, HALO - MLSTM_CONV + 1:, :], c_new, n_new,
                           m_new[:, 0, :M_HEADS], ret_new, fcv_new[:, HALO - FFN_CONV + 1:, :]))

    stacked = [jnp.stack(s, axis=0) for s in zip(*new_states)]
    y_prompt = x[0].reshape(bp, tp, D_MODEL)
    y_sample = x[1].reshape(bs, ts, D_MODEL)
    return (y_prompt, y_sample) + tuple(s[:, :bp] for s in stacked) + tuple(s[:, bp:] for s in stacked)
```

```python
import functools

import numpy as np
import jax
import jax.numpy as jnp
from jax import lax
from jax.experimental import pallas as pl
from jax.experimental.pallas import tpu as pltpu

F32 = jnp.float32
BF16 = jnp.bfloat16

D_MODEL = 2048
CHUNK = 64
MIX_W = 1024
RWKV_HEAD = 64
RWKV_GROUP = 2
GROUP_W = RWKV_HEAD * RWKV_GROUP
N_GROUPS = MIX_W // GROUP_W
LORA_W = 64 + 64 + 160
RWKV_LN_EPS = 64e-5
M_HEADS = 4
M_HEAD = 256
MLSTM_CONV = 4
D_FF = 5632
FFN_CONV = 3
PLE_DIM = 256
PAST_LEN = 4096
ROPE_BASE = 10000.0
NORM_EPS = 1e-6
HEAD_NORM_EPS = 1e-5
HALO = 8
NEG_BIG = -1e30

GATE_BLK = 2048
RKV_BLK = 3072
MIX_BLK = 1024
TAIL_BLK = 512
ZW_OFF, ZW_W = 0, 64
ZA_OFF, ZA_W = 128, 64
ZG_OFF, ZG_W = 256, 160
MGATE_BLK, MGATE_LANE = 1, 64
IN_COLS_PAD = 17920
VMEM_LIMIT = 56 * 1024 * 1024


def _cparams(sem):
    return pltpu.CompilerParams(dimension_semantics=sem, vmem_limit_bytes=VMEM_LIMIT)


def _pick(n, prefs):
    for p in prefs:
        if n % p == 0:
            return p
    raise ValueError(f"no tile for {n}")


def _sigmoid(x):
    return jax.nn.sigmoid(x)


def _rms(x, g):
    return x * lax.rsqrt(jnp.mean(x * x, axis=-1, keepdims=True) + NORM_EPS) * g


def _dot(a, b):
    return jnp.dot(a, b, preferred_element_type=F32)


def _dot_nt(a, b):
    return lax.dot_general(a, b, (((1,), (1,)), ((), ())), preferred_element_type=F32)


def _dot_tn(a, b):
    return lax.dot_general(a, b, (((0,), (0,)), ((), ())), preferred_element_type=F32)


def _dot_f32(a, b):
    return jnp.dot(a, b, preferred_element_type=F32, precision=lax.Precision.HIGHEST)


def _split_dot_left(w_bf16, x):
    hi = x.astype(BF16)
    lo = (x - hi.astype(F32)).astype(BF16)
    return _dot(w_bf16, hi) + _dot(w_bf16, lo)


def _norm_mm_kernel(x_ref, g_ref, w_ref, o_ref, xn_ref):
    @pl.when(pl.program_id(1) == 0)
    def _():
        xn_ref[...] = _rms(x_ref[...], g_ref[...]).astype(BF16)

    o_ref[...] = _dot(xn_ref[...], w_ref[...])


def _norm_matmul(x, g, w, layer, name):
    m, d = x.shape
    n = w.shape[2]
    tm = _pick(m, (1024, 512, 256, 128, 64))
    tn = _pick(n, (1280, 1024, 512, 256, 128))
    return pl.pallas_call(
        _norm_mm_kernel,
        grid=(m // tm, n // tn),
        in_specs=[pl.BlockSpec((tm, d), lambda i, j: (i, 0)),
                  pl.BlockSpec((1, d), lambda i, j: (0, 0)),
                  pl.BlockSpec((None, d, tn), lambda i, j: (layer, 0, j))],
        out_specs=pl.BlockSpec((tm, tn), lambda i, j: (i, j)),
        out_shape=jax.ShapeDtypeStruct((m, n), F32),
        scratch_shapes=[pltpu.VMEM((tm, d), BF16)],
        compiler_params=_cparams(("parallel", "arbitrary")),
        name=name,
    )(x, g, w)


FFN_TN = 512
FFN_RB = 256


def _ffn_kernel(x_ref, g_ref, wa_ref, wb_ref, wd_ref, cw_ref, *rest, tm, tiles_per_seq, seq_rows):
    if tiles_per_seq:
        o_ref, fcv_ref, xn_ref, ext_ref, carry_ref = rest
    else:
        st_ref, o_ref, fcv_ref, xn_ref, ext_ref = rest
    i, j = pl.program_id(0), pl.program_id(1)
    rb_rows = min(FFN_RB, tm)

    @pl.when(j == 0)
    def _():
        x = x_ref[...]
        xn_ref[...] = _rms(x, g_ref[...]).astype(BF16)
        o_ref[...] = x

    if tiles_per_seq:
        @pl.when(i % tiles_per_seq == 0)
        def _():
            ext_ref[0:HALO] = jnp.zeros((HALO, FFN_TN), F32)

        @pl.when(i % tiles_per_seq != 0)
        def _():
            ext_ref[0:HALO] = carry_ref[j]

    def up(rb):
        r0 = rb * rb_rows
        xn = xn_ref[r0:r0 + rb_rows]
        x0 = _dot(xn, wa_ref[...])
        ext_ref[HALO + r0:HALO + r0 + rb_rows] = x0
        return x0, _dot(xn, wb_ref[...])

    def gated(rb, x0, b):
        r0 = rb * rb_rows
        x1 = ext_ref[HALO - 1 + r0:HALO - 1 + r0 + rb_rows]
        x2 = ext_ref[HALO - 2 + r0:HALO - 2 + r0 + rb_rows]
        if not tiles_per_seq:
            loc = lax.broadcasted_iota(jnp.int32, (rb_rows, 1), 0) % seq_rows
            seq0 = r0 // seq_rows
            prev = lambda row: jnp.concatenate(
                [jnp.broadcast_to(st_ref[seq0 + q, row:row + 1, :], (seq_rows, FFN_TN))
                 for q in range(rb_rows // seq_rows)], axis=0)
            s1, s2 = prev(HALO - 1), prev(HALO - 2)
            x1 = jnp.where(loc == 0, s1, x1)
            x2 = jnp.where(loc == 0, s2, jnp.where(loc == 1, s1, x2))
        a = x2 * cw_ref[0:1, :] + x1 * cw_ref[1:2, :] + x0 * cw_ref[2:3, :]
        return (0.5 * a * (1.0 + lax.erf(a * float(np.sqrt(0.5)))) * b).astype(BF16)

    n_rb = tm // rb_rows
    pending = up(0)
    for rb in range(n_rb):
        following = up(rb + 1) if rb + 1 < n_rb else None
        act = gated(rb, *pending)
        o_ref[rb * rb_rows:(rb + 1) * rb_rows] += _dot(act, wd_ref[...])
        pending = following

    if tiles_per_seq:
        carry_ref[j] = ext_ref[tm:tm + HALO]
        fcv_ref[0] = ext_ref[tm:tm + HALO]
    else:
        for q in range(tm // seq_rows):
            fcv_ref[q] = ext_ref[(q + 1) * seq_rows:(q + 1) * seq_rows + HALO]


def _ffn(x, g, wa_wb, wd, cw, n_seq, seq_rows, row0, st=None):
    m = x.shape[0]
    nj = D_FF // FFN_TN
    if st is None:
        tm = _pick(seq_rows, (1024, 512, 256, 128, 64))
        tiles_per_seq = seq_rows // tm
        seqs_per_tile = 1
    else:
        tm = _pick(n_seq * seq_rows, (1024, 512, 256, 128, 64))
        assert tm % seq_rows == 0 and min(FFN_RB, tm) % seq_rows == 0
        tiles_per_seq = 0
        seqs_per_tile = tm // seq_rows
    assert row0 % tm == 0
    i0 = row0 // tm
    n_tiles = n_seq * seq_rows // tm
    in_specs = [pl.BlockSpec((tm, D_MODEL), lambda i, j: (i0 + i, 0)),
                pl.BlockSpec((1, D_MODEL), lambda i, j: (0, 0)),
                pl.BlockSpec((D_MODEL, FFN_TN), lambda i, j: (0, j)),
                pl.BlockSpec((D_MODEL, FFN_TN), lambda i, j: (0, nj + j)),
                pl.BlockSpec((FFN_TN, D_MODEL), lambda i, j: (j, 0)),
                pl.BlockSpec((FFN_CONV, FFN_TN), lambda i, j: (0, j))]
    args = [x, g, wa_wb, wa_wb, wd, cw]
    scratch = [pltpu.VMEM((tm, D_MODEL), BF16), pltpu.VMEM((HALO + tm, FFN_TN), F32)]
    if st is None:
        scratch.append(pltpu.VMEM((nj, HALO, FFN_TN), F32))
        fcv_rows = n_tiles
    else:
        in_specs.append(pl.BlockSpec((seqs_per_tile, HALO, FFN_TN), lambda i, j: (i, 0, j)))
        args.append(st)
        fcv_rows = n_seq
    return pl.pallas_call(
        functools.partial(_ffn_kernel, tm=tm, tiles_per_seq=tiles_per_seq, seq_rows=seq_rows),
        grid=(n_tiles, nj),
        in_specs=in_specs,
        out_specs=[pl.BlockSpec((tm, D_MODEL), lambda i, j: (i0 + i, 0)),
                   pl.BlockSpec((seqs_per_tile, HALO, FFN_TN), lambda i, j: (i, 0, j))],
        out_shape=[jax.ShapeDtypeStruct((m, D_MODEL), F32),
                   jax.ShapeDtypeStruct((fcv_rows, HALO, D_FF), F32)],
        scratch_shapes=scratch,
        input_output_aliases={0: 0},
        compiler_params=_cparams(("arbitrary", "arbitrary")),
        name="ffn_prompt" if st is None else "ffn_sample",
    )(*args)


MERGE_NC = 512


def _merge_kernel(x_ref, g0_ref, g1_ref, g2_ref, o0_ref, o1_ref, o2_ref, wb_ref, wo_ref, out_ref, mg_ref):
    branches = ((g0_ref, o0_ref), (g1_ref, o1_ref), (g2_ref, o2_ref))
    for nc in range(D_MODEL // MERGE_NC):
        cs = slice(nc * MERGE_NC, (nc + 1) * MERGE_NC)
        acc = None
        for j, (g_ref, o_ref) in enumerate(branches):
            t = _sigmoid(g_ref[:, cs]) * _dot(o_ref[...], wb_ref[j, :, cs])
            acc = t if acc is None else acc + t
        mg_ref[:, cs] = acc.astype(BF16)
    for nc in range(D_MODEL // MERGE_NC):
        cs = slice(nc * MERGE_NC, (nc + 1) * MERGE_NC)
        out_ref[:, cs] = x_ref[:, cs] + _dot(mg_ref[...], wo_ref[:, cs])


def _merge(x, proj, o_r, o_m, o_t, wb, wo):
    m = x.shape[0]
    tm = _pick(m, (256, 128, 64))
    resident = dict(pipeline_mode=pl.Buffered(1))
    return pl.pallas_call(
        _merge_kernel,
        grid=(m // tm,),
        in_specs=[pl.BlockSpec((tm, D_MODEL), lambda i: (i, 0)),
                  pl.BlockSpec((tm, GATE_BLK), lambda i: (i, 0)),
                  pl.BlockSpec((tm, GATE_BLK), lambda i: (i, 1)),
                  pl.BlockSpec((tm, GATE_BLK), lambda i: (i, 2)),
                  pl.BlockSpec((tm, MIX_W), lambda i: (i, 0)),
                  pl.BlockSpec((tm, MIX_W), lambda i: (i, 0)),
                  pl.BlockSpec((tm, MIX_W), lambda i: (i, 0)),
                  pl.BlockSpec((3, MIX_W, D_MODEL), lambda i: (0, 0, 0), **resident),
                  pl.BlockSpec((D_MODEL, D_MODEL), lambda i: (0, 0), **resident)],
        out_specs=pl.BlockSpec((tm, D_MODEL), lambda i: (i, 0)),
        out_shape=jax.ShapeDtypeStruct((m, D_MODEL), F32),
        scratch_shapes=[pltpu.VMEM((tm, D_MODEL), BF16)],
        compiler_params=_cparams(("parallel",)),
        name="merge",
    )(x, proj, proj, proj, o_r, o_m, o_t, wb, wo)


def _ple_kernel(x_ref, g_ref, wg_ref, p_ref, wp_ref, gf_ref, *o_refs, prompt_tiles):
    x = x_ref[...]
    gate = _sigmoid(_dot(_rms(x, g_ref[...]).astype(BF16), wg_ref[...]))
    y = x + _dot(p_ref[...].astype(BF16), wp_ref[...]) * gate
    if prompt_tiles is None:
        o_refs[0][...] = y
        return
    y = _rms(y, gf_ref[...])
    i = pl.program_id(0)

    @pl.when(i < prompt_tiles)
    def _():
        o_refs[0][...] = y

    @pl.when(i >= prompt_tiles)
    def _():
        o_refs[1][...] = y


def _ple(x, g, wg, p, wp, gf, split_rows=None):
    m = x.shape[0]
    tm = _pick(m if split_rows is None else np.gcd(*split_rows), (512, 256, 128, 64))
    resident = dict(pipeline_mode=pl.Buffered(1))
    if split_rows is None:
        prompt_tiles = None
        out_specs = pl.BlockSpec((tm, D_MODEL), lambda i: (i, 0))
        out_shape = jax.ShapeDtypeStruct((m, D_MODEL), F32)
    else:
        prompt_tiles = split_rows[0] // tm
        out_specs = [pl.BlockSpec((tm, D_MODEL), lambda i: (jnp.minimum(i, prompt_tiles - 1), 0)),
                     pl.BlockSpec((tm, D_MODEL), lambda i: (jnp.maximum(i - prompt_tiles, 0), 0))]
        out_shape = [jax.ShapeDtypeStruct((r, D_MODEL), F32) for r in split_rows]
    return pl.pallas_call(
        functools.partial(_ple_kernel, prompt_tiles=prompt_tiles),
        grid=(m // tm,),
        in_specs=[pl.BlockSpec((tm, D_MODEL), lambda i: (i, 0)),
                  pl.BlockSpec((1, D_MODEL), lambda i: (0, 0)),
                  pl.BlockSpec((D_MODEL, D_MODEL), lambda i: (0, 0), **resident),
                  pl.BlockSpec((tm, PLE_DIM), lambda i: (i, 0)),
                  pl.BlockSpec((PLE_DIM, D_MODEL), lambda i: (0, 0), **resident),
                  pl.BlockSpec((1, D_MODEL), lambda i: (0, 0))],
        out_specs=out_specs,
        out_shape=out_shape,
        compiler_params=_cparams(("arbitrary",)),
        name="ple" if split_rows is None else "ple_final",
    )(x, g, wg, p, wp, gf)


class _Seqs:
    def __init__(self, bp, tp, bs, ts):
        assert tp % CHUNK == 0 and ts % CHUNK == 0
        self.cp, self.cs = tp // CHUNK, ts // CHUNK
        self.npc = bp * self.cp
        self.n_chunks = self.npc + bs * self.cs
        self.bp = bp
        self.n_seq = bp + bs

    def _split(self, c):
        in_p = c < self.npc
        cc = c - self.npc
        seq = jnp.where(in_p, c // self.cp, self.bp + cc // self.cs)
        pos = jnp.where(in_p, c % self.cp, cc % self.cs)
        return in_p, seq, pos

    def seq(self, c):
        return self._split(c)[1]

    def first(self, c):
        return self._split(c)[2] == 0

    def last(self, c):
        in_p, _, pos = self._split(c)
        return pos == jnp.where(in_p, self.cp - 1, self.cs - 1)

    def rope_block(self, c):
        in_p, _, pos = self._split(c)
        return jnp.where(in_p, pos, self.cp + pos)


def _shifted(ext_ref, x, n_prev):
    ext_ref[HALO:HALO + CHUNK] = x
    return [ext_ref[HALO - j:HALO - j + CHUNK] for j in range(1, n_prev + 1)]


def _roll_halo(ext_ref):
    ext_ref[0:HALO] = ext_ref[CHUNK:CHUNK + HALO]


def _bd_mask():
    r = lax.broadcasted_iota(jnp.int32, (GROUP_W, GROUP_W), 0) // RWKV_HEAD
    c = lax.broadcasted_iota(jnp.int32, (GROUP_W, GROUP_W), 1) // RWKV_HEAD
    return r == c


def _rwkv_chunk(al, bt, r, k, v, lg, logd, sbd_ref):
    groups = range(N_GROUPS)
    bdm = _bd_mask()
    tt = lax.broadcasted_iota(jnp.int32, (CHUNK, GROUP_W), 0)
    ss = lax.broadcasted_iota(jnp.int32, (CHUNK, GROUP_W), 1) % RWKV_HEAD
    low_s, low_i = tt > ss, tt >= ss
    eye = jnp.where(tt == ss, 1.0, 0.0)

    def sl(x, i):
        return x[:, i * GROUP_W:(i + 1) * GROUP_W]

    def bd(x16):
        return jnp.where(bdm, jnp.concatenate([x16] * RWKV_GROUP, axis=0), jnp.zeros((), BF16))

    def stack16(a, b):
        return jnp.concatenate([a, b], axis=0).astype(BF16)

    e_in = jnp.exp(lg)
    e_inv = jnp.exp(-lg)
    at = al * jnp.exp(lg - logd)
    rt = r * e_in
    kh = k * e_inv
    bh = bt * e_inv
    e_end = e_in[CHUNK - 1:CHUNK, :]
    bh_end = (bh * e_end).astype(BF16)
    kh_end = (kh * e_end).astype(BF16)
    v16 = v.astype(BF16)

    lhs = [stack16(sl(at, i), sl(rt, i)) for i in groups]
    x = [_dot_nt(lhs[i], jnp.concatenate([bd(sl(bh, i).astype(BF16)), bd(sl(kh, i).astype(BF16))], axis=0))
         for i in groups]
    p = [jnp.where(low_s, x[i][:CHUNK, :GROUP_W], 0.0) for i in groups]
    q = [jnp.where(low_s, x[i][:CHUNK, GROUP_W:], 0.0) for i in groups]
    rb = [jnp.where(low_i, x[i][CHUNK:, :GROUP_W], 0.0) for i in groups]
    rk = [jnp.where(low_i, x[i][CHUNK:, GROUP_W:], 0.0) for i in groups]
    p16 = [p[i].astype(BF16) for i in groups]
    a = [_dot(p16[i], bd(p16[i])) for i in groups]
    t = [eye + p[i] for i in groups]
    for _ in range(4):
        res = [_dot(stack16(t[i], a[i]), bd(a[i].astype(BF16))) for i in groups]
        t = [t[i] + res[i][:CHUNK] for i in groups]
        a = [res[i][CHUNK:] for i in groups]
    t = [t[i] + _dot(t[i].astype(BF16), bd(a[i].astype(BF16))) for i in groups]
    xm = [_dot_nt(lhs[i], sbd_ref[i].astype(BF16)) for i in groups]
    xv = [_dot(stack16(q[i], rk[i]), bd(sl(v16, i))) for i in groups]
    u16 = [_dot(t[i].astype(BF16), bd((xm[i][:CHUNK] + xv[i][:CHUNK]).astype(BF16))).astype(BF16) for i in groups]
    y = [xm[i][CHUNK:] + xv[i][CHUNK:] + _dot(rb[i].astype(BF16), bd(u16[i])) for i in groups]
    for i in groups:
        upd = _dot_tn(jnp.concatenate([u16[i], sl(v16, i)], axis=0),
                      jnp.concatenate([sl(bh_end, i), sl(kh_end, i)], axis=0))
        sbd_ref[i] = sbd_ref[i] * sl(e_end, i) + jnp.where(bdm, upd, 0.0)
    return jnp.concatenate(y, axis=1)


def _rwkv_kernel(seqs, z_ref, tail_ref, sh0_ref, sht0_ref, s0_ref,
                 mu_ref, mut_ref, w0_ref, w2_ref, a0_ref, a2_ref, g2_ref, kk_ref, ka_ref, rk_ref,
                 lnw_ref, lnb_ref,
                 o_ref, sh_out_ref, sht_out_ref, s_out_ref,
                 ext_ref, extt_ref, sbd_ref):
    c = pl.program_id(0)

    @pl.when(seqs.first(c))
    def _():
        ext_ref[0:HALO] = sh0_ref[0]
        extt_ref[0:HALO] = sht0_ref[0]
        sbd_ref[...] = jnp.zeros_like(sbd_ref)
        for h in range(MIX_W // RWKV_HEAD):
            g, hh = divmod(h, RWKV_GROUP)
            blk = slice(hh * RWKV_HEAD, (hh + 1) * RWKV_HEAD)
            sbd_ref[g, blk, blk] = s0_ref[0, h]

    z = z_ref[...]
    (zp,) = _shifted(ext_ref, z, 1)
    zs = z + mu_ref[...] * (zp - z)
    _roll_halo(ext_ref)
    zt = tail_ref[...]
    (ztp,) = _shifted(extt_ref, zt, 1)
    lora = zt + mut_ref[...] * (ztp - zt)
    _roll_halo(extt_ref)

    r = zs[:, 0:MIX_W]
    k = zs[:, MIX_W:2 * MIX_W]
    v = zs[:, 2 * MIX_W:3 * MIX_W]
    zw = lora[:, ZW_OFF:ZW_OFF + 128]
    za = lora[:, ZA_OFF:ZA_OFF + 128]
    zg = lora[:, ZG_OFF:ZG_OFF + 256]
    logd = -float(np.exp(-0.5)) * _sigmoid(w0_ref[...] + _dot(jnp.tanh(zw).astype(BF16), w2_ref[...]))
    a = _sigmoid(a0_ref[...] + _dot(za.astype(BF16), a2_ref[...]))
    g = _dot(_sigmoid(zg).astype(BF16), g2_ref[...])

    ones_bd = jnp.where(_bd_mask(), 1.0, 0.0).astype(BF16)

    def head_sum(x):
        x16 = x.astype(BF16)
        return jnp.concatenate(
            [_dot(x16[:, i * GROUP_W:(i + 1) * GROUP_W], ones_bd) for i in range(N_GROUPS)], axis=1)

    kk = k * kk_ref[...]
    kk = kk * lax.rsqrt(jnp.maximum(head_sum(kk * kk), 1e-24))
    k2 = k * (1.0 + (a - 1.0) * ka_ref[...])
    al = -kk
    bt = kk * a
    ti = lax.broadcasted_iota(jnp.int32, (CHUNK, CHUNK), 0)
    si = lax.broadcasted_iota(jnp.int32, (CHUNK, CHUNK), 1)
    lg = _split_dot_left(jnp.where(ti >= si, 1.0, 0.0).astype(BF16), logd)

    y = _rwkv_chunk(al, bt, r, k2, v, lg, logd, sbd_ref)

    inv_n = 1.0 / RWKV_HEAD
    d = y - head_sum(y) * inv_n
    var = head_sum(d * d) * inv_n
    yn = d * lax.rsqrt(var + RWKV_LN_EPS) * lnw_ref[...] + lnb_ref[...]
    bonus = head_sum(r * k2 * rk_ref[...]) * v
    o_ref[...] = ((yn + bonus) * g).astype(BF16)

    @pl.when(seqs.last(c))
    def _():
        sh_out_ref[0] = ext_ref[0:HALO]
        sht_out_ref[0] = extt_ref[0:HALO]
        for h in range(MIX_W // RWKV_HEAD):
            g_, hh = divmod(h, RWKV_GROUP)
            blk = slice(hh * RWKV_HEAD, (hh + 1) * RWKV_HEAD)
            s_out_ref[0, h] = sbd_ref[g_, blk, blk]


def _rwkv(seqs, proj, sh0, sht0, s0, pr):
    m = proj.shape[0]
    n_heads = MIX_W // RWKV_HEAD
    row = lambda w: pl.BlockSpec((1, w), lambda c: (0, 0))
    lora_w = lambda k: pl.BlockSpec((k, MIX_W), lambda c: (0, 0))
    st3 =lambda w: pl.BlockSpec((1, HALO, w), lambda c: (seqs.seq(c), 0, 0))
    st_s = pl.BlockSpec((1, n_heads, RWKV_HEAD, RWKV_HEAD), lambda c: (seqs.seq(c), 0, 0, 0))
    return pl.pallas_call(
        functools.partial(_rwkv_kernel, seqs),
        grid=(seqs.n_chunks,),
        in_specs=[pl.BlockSpec((CHUNK, RKV_BLK), lambda c: (c, 2)),
                  pl.BlockSpec((CHUNK, TAIL_BLK), lambda c: (c, IN_COLS_PAD // TAIL_BLK - 1)),
                  st3(RKV_BLK), st3(TAIL_BLK), st_s,
                  row(RKV_BLK), row(TAIL_BLK), row(MIX_W), lora_w(128), row(MIX_W), lora_w(128), lora_w(256),
                  row(MIX_W), row(MIX_W), row(MIX_W), row(MIX_W), row(MIX_W)],
        out_specs=[pl.BlockSpec((CHUNK, MIX_W), lambda c: (c, 0)), st3(RKV_BLK), st3(TAIL_BLK), st_s],
        out_shape=[jax.ShapeDtypeStruct((m, MIX_W), BF16),
                   jax.ShapeDtypeStruct((seqs.n_seq, HALO, RKV_BLK), F32),
                   jax.ShapeDtypeStruct((seqs.n_seq, HALO, TAIL_BLK), F32),
                   jax.ShapeDtypeStruct((seqs.n_seq, n_heads, RWKV_HEAD, RWKV_HEAD), F32)],
        scratch_shapes=[pltpu.VMEM((HALO + CHUNK, RKV_BLK), F32),
                        pltpu.VMEM((HALO + CHUNK, TAIL_BLK), F32),
                        pltpu.VMEM((N_GROUPS, GROUP_W, GROUP_W), F32)],
        compiler_params=_cparams(("arbitrary",)),
        name="rwkv",
    )(proj, proj, sh0, sht0, s0, pr["mu"], pr["mut"], pr["w0"], pr["w2"], pr["a0"], pr["a2"], pr["g2"],
      pr["kk"], pr["ka"], pr["rk"], pr["lnw"], pr["lnb"])


def _head_norm(x):
    mu = jnp.mean(x, axis=-1, keepdims=True)
    d = x - mu
    return d * lax.rsqrt(jnp.mean(d * d, axis=-1, keepdims=True) + HEAD_NORM_EPS)


def _mlstm_kernel(seqs, q_ref, k_ref, v_ref, op_ref, tail_ref, cv0_ref, c0_ref, n0_ref, m0_ref,
                  cw_ref, gb_ref, nw_ref,
                  o_ref, cv_out_ref, c_ref, n_ref, m_ref,
                  ext_ref):
    c = pl.program_id(0)

    @pl.when(seqs.first(c))
    def _():
        ext_ref[0:HALO] = cv0_ref[0]
        c_ref[...] = c0_ref[...]
        n_ref[...] = n0_ref[...]
        m_ref[...] = m0_ref[...]

    x0 = jnp.concatenate([q_ref[...], k_ref[...]], axis=1)
    x1, x2, x3 = _shifted(ext_ref, x0, MLSTM_CONV - 1)
    qk = x3 * cw_ref[0:1, :] + x2 * cw_ref[1:2, :] + x1 * cw_ref[2:3, :] + x0 * cw_ref[3:4, :]
    _roll_halo(ext_ref)
    qk = qk * _sigmoid(qk)
    q_all = qk[:, :MIX_W]
    k_all = qk[:, MIX_W:] * (M_HEAD ** -0.5)

    gates = tail_ref[:, MGATE_BLK * 128:(MGATE_BLK + 1) * 128] + gb_ref[...]
    lsf = jnp.minimum(gates, 0.0) - jnp.log(1.0 + jnp.exp(-jnp.abs(gates)))
    ti = lax.broadcasted_iota(jnp.int32, (CHUNK, CHUNK), 0)
    si = lax.broadcasted_iota(jnp.int32, (CHUNK, CHUNK), 1)
    tril = ti >= si
    bcum_col = _dot_f32(jnp.where(tril, 1.0, 0.0), lsf)
    ig_rows = gates.T[MGATE_LANE:MGATE_LANE + HALO]
    bcum_rows = _dot_f32(lsf.T[MGATE_LANE:MGATE_LANE + HALO], jnp.where(ti <= si, 1.0, 0.0))
    lane = lax.broadcasted_iota(jnp.int32, (1, 128), 1)
    m_row = m_ref[0]
    heads = range(M_HEADS)
    hsl = [slice(h * M_HEAD, (h + 1) * M_HEAD) for h in heads]

    bc = [bcum_col[:, MGATE_LANE + M_HEADS + h:MGATE_LANE + M_HEADS + h + 1] for h in heads]
    ic = [gates[:, MGATE_LANE + h:MGATE_LANE + h + 1] for h in heads]
    log_inter = [bc[h] + m_row[:, h:h + 1] for h in heads]
    log_intra = [jnp.where(tril, bc[h] - bcum_rows[M_HEADS + h:M_HEADS + h + 1, :] + ig_rows[h:h + 1, :], NEG_BIG)
                 for h in heads]
    m_t = [jnp.maximum(log_inter[h], jnp.max(log_intra[h], axis=-1, keepdims=True)) for h in heads]
    w_inter = [jnp.exp(log_inter[h] - m_t[h]) for h in heads]
    w_intra = [jnp.exp(log_intra[h] - m_t[h]) for h in heads]
    m_end = [m_t[h][CHUNK - 1:CHUNK, :] for h in heads]
    w_end = [jnp.exp(bc[h][CHUNK - 1:CHUNK, :] - bc[h] + ic[h] - m_end[h]) for h in heads]
    g_end = [w_inter[h][CHUNK - 1:CHUNK, :] for h in heads]
    q = [q_all[:, hsl[h]] for h in heads]
    k = [k_all[:, hsl[h]] for h in heads]
    v = [v_ref[:, hsl[h]] for h in heads]
    q16 = [q[h].astype(BF16) for h in heads]
    k16 = [k[h].astype(BF16) for h in heads]
    v16 = [v[h].astype(BF16) for h in heads]
    s = [_dot_nt(q16[h], k16[h]) * w_intra[h] for h in heads]
    qc = [_dot(q16[h], c_ref[0, h].astype(BF16)) for h in heads]
    sv = [_dot(s[h].astype(BF16), v16[h]) for h in heads]
    kv = [_dot_tn(k16[h], (w_end[h] * v[h]).astype(BF16)) for h in heads]
    m_new_row = m_row
    for h in heads:
        n_h = n_ref[0, h:h + 1, :]
        num = w_inter[h] * qc[h] + sv[h]
        den = w_inter[h] * jnp.sum(q[h] * n_h, axis=-1, keepdims=True) + jnp.sum(s[h], axis=-1, keepdims=True)
        hh = num / jnp.maximum(jnp.abs(den), jnp.exp(-m_t[h]))
        c_ref[0, h] = g_end[h] * c_ref[0, h] + kv[h]
        n_ref[0, h:h + 1, :] = g_end[h] * n_h + jnp.sum(w_end[h] * k[h], axis=0, keepdims=True)
        m_new_row = jnp.where(lane == h, m_end[h], m_new_row)
        o_ref[:, hsl[h]] = (_head_norm(hh) * nw_ref[:, hsl[h]] * _sigmoid(op_ref[:, hsl[h]])).astype(BF16)

    m_ref[0] = m_new_row

    @pl.when(seqs.last(c))
    def _():
        cv_out_ref[0] = ext_ref[0:HALO]


def _mlstm(seqs, proj, cv0, c0, n0, m0, pr):
    m = proj.shape[0]
    blk = lambda j: pl.BlockSpec((CHUNK, MIX_BLK), lambda c: (c, j))
    sq = lambda c: seqs.seq(c)
    st_cv = pl.BlockSpec((1, HALO, 2 * MIX_W), lambda c: (sq(c), 0, 0))
    st_c = pl.BlockSpec((1, M_HEADS, M_HEAD, M_HEAD), lambda c: (sq(c), 0, 0, 0))
    st_n = pl.BlockSpec((1, M_HEADS, M_HEAD), lambda c: (sq(c), 0, 0))
    st_m = pl.BlockSpec((1, 1, 128), lambda c: (sq(c), 0, 0))
    return pl.pallas_call(
        functools.partial(_mlstm_kernel, seqs),
        grid=(seqs.n_chunks,),
        in_specs=[blk(9), blk(10), blk(11), blk(12),
                  pl.BlockSpec((CHUNK, TAIL_BLK), lambda c: (c, IN_COLS_PAD // TAIL_BLK - 1)),
                  st_cv, st_c, st_n, st_m,
                  pl.BlockSpec((MLSTM_CONV, 2 * MIX_W), lambda c: (0, 0)),
                  pl.BlockSpec((1, 128), lambda c: (0, 0)),
                  pl.BlockSpec((1, MIX_W), lambda c: (0, 0))],
        out_specs=[pl.BlockSpec((CHUNK, MIX_W), lambda c: (c, 0)), st_cv, st_c, st_n, st_m],
        out_shape=[jax.ShapeDtypeStruct((m, MIX_W), BF16),
                   jax.ShapeDtypeStruct((seqs.n_seq, HALO, 2 * MIX_W), F32),
                   jax.ShapeDtypeStruct((seqs.n_seq, M_HEADS, M_HEAD, M_HEAD), F32),
                   jax.ShapeDtypeStruct((seqs.n_seq, M_HEADS, M_HEAD), F32),
                   jax.ShapeDtypeStruct((seqs.n_seq, 1, 128), F32)],
        scratch_shapes=[pltpu.VMEM((HALO + CHUNK, 2 * MIX_W), F32)],
        compiler_params=_cparams(("arbitrary",)),
        name="mlstm",
    )(proj, proj, proj, proj, proj, cv0, c0, n0, m0, pr["cw"], pr["gb"], pr["nw"])


def _ret_kernel(seqs, q_ref, k_ref, v_ref, g_ref, cos_ref, sin_ref, r0_ref, o_ref, r_ref):
    c = pl.program_id(0)

    @pl.when(seqs.first(c))
    def _():
        r_ref[...] = r0_ref[...]

    cos, sin = cos_ref[...], sin_ref[...]
    half = M_HEAD // 2
    ti = lax.broadcasted_iota(jnp.int32, (CHUNK, CHUNK), 0)
    si = lax.broadcasted_iota(jnp.int32, (CHUNK, CHUNK), 1)
    diff = (ti - si).astype(F32)
    t_col = lax.broadcasted_iota(jnp.int32, (CHUNK, 1), 0).astype(F32)

    def rot(u):
        u1, u2 = u[:, :half], u[:, half:]
        return jnp.concatenate([u1 * cos - u2 * sin, u1 * sin + u2 * cos], axis=1)

    heads = range(M_HEADS)
    hsl = [slice(h * M_HEAD, (h + 1) * M_HEAD) for h in heads]
    log_gamma = [float(np.log(1.0 - 2.0 ** (-5.0 - h))) for h in heads]
    q16 = [rot(q_ref[:, hsl[h]]).astype(BF16) for h in heads]
    k = [rot(k_ref[:, hsl[h]]) * (M_HEAD ** -0.5) for h in heads]
    v16 = [v_ref[:, hsl[h]].astype(BF16) for h in heads]
    s = [_dot_nt(q16[h], k[h].astype(BF16))
         * jnp.where(diff >= 0, jnp.exp(log_gamma[h] * jnp.maximum(diff, 0.0)), 0.0) for h in heads]
    qr = [_dot(q16[h], r_ref[0, h].astype(BF16)) for h in heads]
    sv = [_dot(s[h].astype(BF16), v16[h]) for h in heads]
    kv = [_dot_tn((k[h] * jnp.exp(log_gamma[h] * (CHUNK - 1.0 - t_col))).astype(BF16), v16[h]) for h in heads]
    for h in heads:
        o = sv[h] + jnp.exp(log_gamma[h] * (t_col + 1.0)) * qr[h]
        r_ref[0, h] = float(np.exp(log_gamma[h] * CHUNK)) * r_ref[0, h] + kv[h]
        gt = g_ref[:, hsl[h]]
        o_ref[:, hsl[h]] = (_head_norm(o) * (gt * _sigmoid(gt))).astype(BF16)


def _retention(seqs, proj, cos, sin, r0):
    m = proj.shape[0]
    blk = lambda j: pl.BlockSpec((CHUNK, MIX_BLK), lambda c: (c, j))
    rope = pl.BlockSpec((CHUNK, M_HEAD // 2), lambda c: (seqs.rope_block(c), 0))
    st_r = pl.BlockSpec((1, M_HEADS, M_HEAD, M_HEAD), lambda c: (seqs.seq(c), 0, 0, 0))
    return pl.pallas_call(
        functools.partial(_ret_kernel, seqs),
        grid=(seqs.n_chunks,),
        in_specs=[blk(13), blk(14), blk(15), blk(16), rope, rope, st_r],
        out_specs=[pl.BlockSpec((CHUNK, MIX_W), lambda c: (c, 0)), st_r],
        out_shape=[jax.ShapeDtypeStruct((m, MIX_W), BF16),
                   jax.ShapeDtypeStruct((seqs.n_seq, M_HEADS, M_HEAD, M_HEAD), F32)],
        compiler_params=_cparams(("arbitrary",)),
        name="retention",
    )(proj, proj, proj, proj, cos, sin, r0)


def _halo_rows(prev, n_prompt):
    k = prev.shape[1]
    return jnp.pad(prev, ((n_prompt, 0), (HALO - k, 0), (0, 0)))


def _with_prompt(state, n_prompt):
    return jnp.pad(state, ((n_prompt, 0),) + ((0, 0),) * (state.ndim - 1))


def _rope_tables(tp, ts):
    half = M_HEAD // 2
    freq = ROPE_BASE ** (-jnp.arange(half, dtype=F32) / half)
    pos = jnp.concatenate([jnp.arange(tp), PAST_LEN + jnp.arange(ts)]).astype(F32)
    ang = pos[:, None] * freq
    return jnp.cos(ang), jnp.sin(ang)


def _tail_layout(lora, gates=None):
    lead = lora.shape[:-1]
    z = lambda n: jnp.zeros(lead + (n,), lora.dtype)
    mid = z(128 - ZA_W) if gates is None else jnp.concatenate(
        [z(MGATE_LANE - ZA_W), gates, z(128 - MGATE_LANE - gates.shape[-1])], axis=-1)
    return jnp.concatenate([lora[..., :ZW_W], z(128 - ZW_W), lora[..., ZW_W:ZW_W + ZA_W], mid,
                            lora[..., ZW_W + ZA_W:], z(256 - ZG_W)], axis=-1)


def _tail_lora(t):
    return jnp.concatenate([t[..., ZW_OFF:ZW_OFF + ZW_W], t[..., ZA_OFF:ZA_OFF + ZA_W], t[..., ZG_OFF:ZG_OFF + ZG_W]],
                           axis=-1)


def _permute_in_cols(w):
    rw = 3 * MIX_W + LORA_W
    mb = rw
    tb = mb + 4 * MIX_W + 2 * M_HEADS
    gb = tb + 4 * MIX_W
    return jnp.concatenate([
        w[..., gb:gb + 3 * D_MODEL],
        w[..., 0:3 * MIX_W],
        w[..., mb:mb + 4 * MIX_W],
        w[..., tb:tb + 4 * MIX_W],
        _tail_layout(w[..., 3 * MIX_W:rw], w[..., mb + 4 * MIX_W:tb])], axis=-1)


def _lora_rows(w, k):
    return jnp.pad(w, ((0, k - w.shape[0]), (0, 0))).astype(BF16)


def kernel(x_prompt, x_sample, state_rwkv_shift, state_rwkv_wkv, state_mlstm_conv, state_mlstm_c, state_mlstm_n, state_mlstm_m, state_ret, state_ffn_conv, p_prompt, p_sample, norm_mix, w_in, rwkv_mu, rwkv_w0, rwkv_w2, rwkv_a0, rwkv_a2, rwkv_g2, rwkv_kk, rwkv_ka, rwkv_rk, rwkv_lnw, rwkv_lnb, mlstm_conv, mlstm_bi, mlstm_bf, mlstm_nw, w_branch, w_out, norm_ffn, ffn_up, ffn_conv, ffn_down, norm_ple, ple_proj, ple_gate, norm_final):
    bp, tp, _ = x_prompt.shape
    bs, ts, _ = x_sample.shape
    mp, ms = bp * tp, bs * ts
    depth = w_in.shape[0]
    seqs = _Seqs(bp, tp, bs, ts)
    x = jnp.concatenate([x_prompt.reshape(mp, D_MODEL), x_sample.reshape(ms, D_MODEL)], axis=0)
    cos, sin = _rope_tables(tp, ts)
    row = lambda a: a.reshape(1, -1)
    new_states = []
    w_in_p = _permute_in_cols(w_in).astype(BF16)
    for i in range(depth):
        proj = _norm_matmul(x, row(norm_mix[i]), w_in_p, i, "in_proj")

        shift = state_rwkv_shift[i][:, None, :]
        mu = rwkv_mu[i]
        rwkv_pr = dict(
            mu=row(mu[:3 * MIX_W]), mut=row(_tail_layout(mu[3 * MIX_W:])),
            w0=row(rwkv_w0[i]), w2=_lora_rows(rwkv_w2[i], 128),
            a0=row(rwkv_a0[i]), a2=_lora_rows(rwkv_a2[i], 128),
            g2=_lora_rows(rwkv_g2[i], 256),
            kk=row(rwkv_kk[i]), ka=row(rwkv_ka[i]), rk=row(rwkv_rk[i]),
            lnw=row(rwkv_lnw[i]), lnb=row(rwkv_lnb[i]))
        o_r, sh_new, sht_new, wkv_new = _rwkv(
            seqs, proj,
            _halo_rows(shift[:, :, :3 * MIX_W], bp),
            _halo_rows(_tail_layout(shift[:, :, 3 * MIX_W:]), bp),
            _with_prompt(state_rwkv_wkv[i], bp), rwkv_pr)

        gate_bias = jnp.pad(jnp.concatenate([mlstm_bi[i], mlstm_bf[i]]),
                            (MGATE_LANE, 128 - MGATE_LANE - 2 * M_HEADS))
        mlstm_pr = dict(cw=mlstm_conv[i], gb=row(gate_bias), nw=row(mlstm_nw[i]))
        m0 = jnp.pad(state_mlstm_m[i], ((0, 0), (0, 128 - M_HEADS)))[:, None, :]
        o_m, cv_new, c_new, n_new, m_new = _mlstm(
            seqs, proj, _halo_rows(state_mlstm_conv[i], bp), _with_prompt(state_mlstm_c[i], bp),
            _with_prompt(state_mlstm_n[i], bp), _with_prompt(m0, bp), mlstm_pr)

        o_t, ret_new = _retention(seqs, proj, cos, sin, _with_prompt(state_ret[i], bp))

        x = _merge(x, proj, o_r, o_m, o_t, w_branch[i].astype(BF16), w_out[i].astype(BF16))

        wa_wb, wd = ffn_up[i].astype(BF16), ffn_down[i].astype(BF16)
        x, fcv_p = _ffn(x, row(norm_ffn[i]), wa_wb, wd, ffn_conv[i], bp, tp, 0)
        x, fcv_s = _ffn(x, row(norm_ffn[i]), wa_wb, wd, ffn_conv[i], bs, ts, mp,
                        st=jnp.pad(state_ffn_conv[i], ((0, 0), (HALO - FFN_CONV + 1, 0), (0, 0))))
        fcv_new = jnp.concatenate([fcv_p.reshape(bp, -1, HALO, D_FF)[:, -1], fcv_s], axis=0)

        p = jnp.concatenate([p_prompt[i].reshape(mp, PLE_DIM), p_sample[i].reshape(ms, PLE_DIM)], axis=0)
        x = _ple(x, row(norm_ple[i]), ple_gate[i].astype(BF16), p, ple_proj[i].astype(BF16),
                 row(norm_final), split_rows=(mp, ms) if i == depth - 1 else None)

        shift_new = jnp.concatenate([sh_new[:, HALO - 1, :], _tail_lora(sht_new[:, HALO - 1, :])], axis=-1)
        new_states.append((shift_new, wkv_new, cv_new[:, HALO - MLSTM_CONV + 1:, :], c_new, n_new,
                           m_new[:, 0, :M_HEADS], ret_new, fcv_new[:, HALO - FFN_CONV + 1:, :]))

    stacked = [jnp.stack(s, axis=0) for s in zip(*new_states)]
    y_prompt = x[0].reshape(bp, tp, D_MODEL)
    y_sample = x[1].reshape(bs, ts, D_MODEL)
    return (y_prompt, y_sample) + tuple(s[:, :bp] for s in stacked) + tuple(s[:, bp:] for s in stacked)
```

```python
import functools

import numpy as np
import jax
import jax.numpy as jnp
from jax import lax
from jax.experimental import pallas as pl
from jax.experimental.pallas import tpu as pltpu

F32 = jnp.float32
BF16 = jnp.bfloat16

D_MODEL = 2048
CHUNK = 64
MIX_W = 1024
RWKV_HEAD = 64
RWKV_GROUP = 2
GROUP_W = RWKV_HEAD * RWKV_GROUP
N_GROUPS = MIX_W // GROUP_W
LORA_W = 64 + 64 + 160
RWKV_LN_EPS = 64e-5
M_HEADS = 4
M_HEAD = 256
MLSTM_CONV = 4
D_FF = 5632
FFN_CONV = 3
PLE_DIM = 256
PAST_LEN = 4096
ROPE_BASE = 10000.0
NORM_EPS = 1e-6
HEAD_NORM_EPS = 1e-5
HALO = 8
NEG_BIG = -1e30

GATE_BLK = 2048
RKV_BLK = 3072
MIX_BLK = 1024
TAIL_BLK = 512
ZW_OFF, ZW_W = 0, 64
ZA_OFF, ZA_W = 128, 64
ZG_OFF, ZG_W = 256, 160
MGATE_BLK, MGATE_LANE = 1, 64
IN_COLS_PAD = 17920
VMEM_LIMIT = 56 * 1024 * 1024


def _cparams(sem):
    return pltpu.CompilerParams(dimension_semantics=sem, vmem_limit_bytes=VMEM_LIMIT)


def _pick(n, prefs):
    for p in prefs:
        if n % p == 0:
            return p
    raise ValueError(f"no tile for {n}")


def _sigmoid(x):
    return jax.nn.sigmoid(x)


def _rms(x, g):
    return x * lax.rsqrt(jnp.mean(x * x, axis=-1, keepdims=True) + NORM_EPS) * g


def _dot(a, b):
    return jnp.dot(a, b, preferred_element_type=F32)


def _dot_nt(a, b):
    return lax.dot_general(a, b, (((1,), (1,)), ((), ())), preferred_element_type=F32)


def _dot_tn(a, b):
    return lax.dot_general(a, b, (((0,), (0,)), ((), ())), preferred_element_type=F32)


def _dot_f32(a, b):
    return jnp.dot(a, b, preferred_element_type=F32, precision=lax.Precision.HIGHEST)


def _split_dot_left(w_bf16, x):
    hi = x.astype(BF16)
    lo = (x - hi.astype(F32)).astype(BF16)
    return _dot(w_bf16, hi) + _dot(w_bf16, lo)


def _norm_mm_kernel(x_ref, g_ref, w_ref, o_ref, xn_ref):
    @pl.when(pl.program_id(1) == 0)
    def _():
        xn_ref[...] = _rms(x_ref[...], g_ref[...]).astype(BF16)

    o_ref[...] = _dot(xn_ref[...], w_ref[...])


def _norm_matmul(x, g, w, layer, name):
    m, d = x.shape
    n = w.shape[2]
    tm = _pick(m, (1024, 512, 256, 128, 64))
    tn = _pick(n, (1280, 1024, 512, 256, 128))
    return pl.pallas_call(
        _norm_mm_kernel,
        grid=(m // tm, n // tn),
        in_specs=[pl.BlockSpec((tm, d), lambda i, j: (i, 0)),
                  pl.BlockSpec((1, d), lambda i, j: (0, 0)),
                  pl.BlockSpec((None, d, tn), lambda i, j: (layer, 0, j))],
        out_specs=pl.BlockSpec((tm, tn), lambda i, j: (i, j)),
        out_shape=jax.ShapeDtypeStruct((m, n), F32),
        scratch_shapes=[pltpu.VMEM((tm, d), BF16)],
        compiler_params=_cparams(("parallel", "arbitrary")),
        name=name,
    )(x, g, w)


FFN_TN = 512
FFN_RB = 256


def _ffn_kernel(x_ref, g_ref, wa_ref, wb_ref, wd_ref, cw_ref, *rest, tm, tiles_per_seq, seq_rows):
    if tiles_per_seq:
        o_ref, fcv_ref, xn_ref, ext_ref, carry_ref = rest
    else:
        st_ref, o_ref, fcv_ref, xn_ref, ext_ref = rest
    i, j = pl.program_id(0), pl.program_id(1)
    rb_rows = min(FFN_RB, tm)

    @pl.when(j == 0)
    def _():
        x = x_ref[...]
        xn_ref[...] = _rms(x, g_ref[...]).astype(BF16)
        o_ref[...] = x

    if tiles_per_seq:
        @pl.when(i % tiles_per_seq == 0)
        def _():
            ext_ref[0:HALO] = jnp.zeros((HALO, FFN_TN), F32)

        @pl.when(i % tiles_per_seq != 0)
        def _():
            ext_ref[0:HALO] = carry_ref[j]

    def up(rb):
        r0 = rb * rb_rows
        xn = xn_ref[r0:r0 + rb_rows]
        x0 = _dot(xn, wa_ref[...])
        ext_ref[HALO + r0:HALO + r0 + rb_rows] = x0
        return x0, _dot(xn, wb_ref[...])

    def gated(rb, x0, b):
        r0 = rb * rb_rows
        x1 = ext_ref[HALO - 1 + r0:HALO - 1 + r0 + rb_rows]
        x2 = ext_ref[HALO - 2 + r0:HALO - 2 + r0 + rb_rows]
        if not tiles_per_seq:
            loc = lax.broadcasted_iota(jnp.int32, (rb_rows, 1), 0) % seq_rows
            seq0 = r0 // seq_rows
            prev = lambda row: jnp.concatenate(
                [jnp.broadcast_to(st_ref[seq0 + q, row:row + 1, :], (seq_rows, FFN_TN))
                 for q in range(rb_rows // seq_rows)], axis=0)
            s1, s2 = prev(HALO - 1), prev(HALO - 2)
            x1 = jnp.where(loc == 0, s1, x1)
            x2 = jnp.where(loc == 0, s2, jnp.where(loc == 1, s1, x2))
        a = x2 * cw_ref[0:1, :] + x1 * cw_ref[1:2, :] + x0 * cw_ref[2:3, :]
        return (0.5 * a * (1.0 + lax.erf(a * float(np.sqrt(0.5)))) * b).astype(BF16)

    n_rb = tm // rb_rows
    pending = up(0)
    for rb in range(n_rb):
        following = up(rb + 1) if rb + 1 < n_rb else None
        act = gated(rb, *pending)
        o_ref[rb * rb_rows:(rb + 1) * rb_rows] += _dot(act, wd_ref[...])
        pending = following

    if tiles_per_seq:
        carry_ref[j] = ext_ref[tm:tm + HALO]
        fcv_ref[0] = ext_ref[tm:tm + HALO]
    else:
        for q in range(tm // seq_rows):
            fcv_ref[q] = ext_ref[(q + 1) * seq_rows:(q + 1) * seq_rows + HALO]


def _ffn(x, g, wa_wb, wd, cw, n_seq, seq_rows, row0, st=None):
    m = x.shape[0]
    nj = D_FF // FFN_TN
    if st is None:
        tm = _pick(seq_rows, (1024, 512, 256, 128, 64))
        tiles_per_seq = seq_rows // tm
        seqs_per_tile = 1
    else:
        tm = _pick(n_seq * seq_rows, (1024, 512, 256, 128, 64))
        assert tm % seq_rows == 0 and min(FFN_RB, tm) % seq_rows == 0
        tiles_per_seq = 0
        seqs_per_tile = tm // seq_rows
    assert row0 % tm == 0
    i0 = row0 // tm
    n_tiles = n_seq * seq_rows // tm
    in_specs = [pl.BlockSpec((tm, D_MODEL), lambda i, j: (i0 + i, 0)),
                pl.BlockSpec((1, D_MODEL), lambda i, j: (0, 0)),
                pl.BlockSpec((D_MODEL, FFN_TN), lambda i, j: (0, j)),
                pl.BlockSpec((D_MODEL, FFN_TN), lambda i, j: (0, nj + j)),
                pl.BlockSpec((FFN_TN, D_MODEL), lambda i, j: (j, 0)),
                pl.BlockSpec((FFN_CONV, FFN_TN), lambda i, j: (0, j))]
    args = [x, g, wa_wb, wa_wb, wd, cw]
    scratch = [pltpu.VMEM((tm, D_MODEL), BF16), pltpu.VMEM((HALO + tm, FFN_TN), F32)]
    if st is None:
        scratch.append(pltpu.VMEM((nj, HALO, FFN_TN), F32))
        fcv_rows = n_tiles
    else:
        in_specs.append(pl.BlockSpec((seqs_per_tile, HALO, FFN_TN), lambda i, j: (i, 0, j)))
        args.append(st)
        fcv_rows = n_seq
    return pl.pallas_call(
        functools.partial(_ffn_kernel, tm=tm, tiles_per_seq=tiles_per_seq, seq_rows=seq_rows),
        grid=(n_tiles, nj),
        in_specs=in_specs,
        out_specs=[pl.BlockSpec((tm, D_MODEL), lambda i, j: (i0 + i, 0)),
                   pl.BlockSpec((seqs_per_tile, HALO, FFN_TN), lambda i, j: (i, 0, j))],
        out_shape=[jax.ShapeDtypeStruct((m, D_MODEL), F32),
                   jax.ShapeDtypeStruct((fcv_rows, HALO, D_FF), F32)],
        scratch_shapes=scratch,
        input_output_aliases={0: 0},
        compiler_params=_cparams(("arbitrary", "arbitrary")),
        name="ffn_prompt" if st is None else "ffn_sample",
    )(*args)


MERGE_NC = 512


def _merge_kernel(x_ref, g0_ref, g1_ref, g2_ref, o0_ref, o1_ref, o2_ref, wb_ref, wo_ref, out_ref, mg_ref):
    branches = ((g0_ref, o0_ref), (g1_ref, o1_ref), (g2_ref, o2_ref))
    for nc in range(D_MODEL // MERGE_NC):
        cs = slice(nc * MERGE_NC, (nc + 1) * MERGE_NC)
        acc = None
        for j, (g_ref, o_ref) in enumerate(branches):
            t = _sigmoid(g_ref[:, cs]) * _dot(o_ref[...].reshape(-1, MIX_W), wb_ref[j, :, cs])
            acc = t if acc is None else acc + t
        mg_ref[:, cs] = acc.astype(BF16)
    for nc in range(D_MODEL // MERGE_NC):
        cs = slice(nc * MERGE_NC, (nc + 1) * MERGE_NC)
        out_ref[:, cs] = x_ref[:, cs] + _dot(mg_ref[...], wo_ref[:, cs])


def _merge(seqs, mp, ms, x, proj, o_r, o_m, o_t, wb, wo):
    m = x.shape[0]
    tm = _pick(np.gcd(mp // LANES, ms // LANES), (256, 128, 64))
    resident = dict(pipeline_mode=pl.Buffered(1))

    def mixer_block(i):
        lane, blk = seqs.lane_row_block(i, tm, mp, ms)
        return blk, lane, 0, 0

    mixer_spec = pl.BlockSpec((tm // CHUNK, None, CHUNK, MIX_W), mixer_block)
    return pl.pallas_call(
        _merge_kernel,
        grid=(m // tm,),
        in_specs=[pl.BlockSpec((tm, D_MODEL), lambda i: (i, 0)),
                  pl.BlockSpec((tm, GATE_BLK), lambda i: (i, 0)),
                  pl.BlockSpec((tm, GATE_BLK), lambda i: (i, 1)),
                  pl.BlockSpec((tm, GATE_BLK), lambda i: (i, 2)),
                  mixer_spec, mixer_spec, mixer_spec,
                  pl.BlockSpec((3, MIX_W, D_MODEL), lambda i: (0, 0, 0), **resident),
                  pl.BlockSpec((D_MODEL, D_MODEL), lambda i: (0, 0), **resident)],
        out_specs=pl.BlockSpec((tm, D_MODEL), lambda i: (i, 0)),
        out_shape=jax.ShapeDtypeStruct((m, D_MODEL), F32),
        scratch_shapes=[pltpu.VMEM((tm, D_MODEL), BF16)],
        compiler_params=_cparams(("parallel",)),
        name="merge",
    )(x, proj, proj, proj, o_r, o_m, o_t, wb, wo)


def _ple_kernel(x_ref, g_ref, wg_ref, p_ref, wp_ref, gf_ref, *o_refs, prompt_tiles):
    x = x_ref[...]
    gate = _sigmoid(_dot(_rms(x, g_ref[...]).astype(BF16), wg_ref[...]))
    y = x + _dot(p_ref[...].astype(BF16), wp_ref[...]) * gate
    if prompt_tiles is None:
        o_refs[0][...] = y
        return
    y = _rms(y, gf_ref[...])
    i = pl.program_id(0)

    @pl.when(i < prompt_tiles)
    def _():
        o_refs[0][...] = y

    @pl.when(i >= prompt_tiles)
    def _():
        o_refs[1][...] = y


def _ple(x, g, wg, p, wp, gf, split_rows=None):
    m = x.shape[0]
    tm = _pick(m if split_rows is None else np.gcd(*split_rows), (512, 256, 128, 64))
    resident = dict(pipeline_mode=pl.Buffered(1))
    if split_rows is None:
        prompt_tiles = None
        out_specs = pl.BlockSpec((tm, D_MODEL), lambda i: (i, 0))
        out_shape = jax.ShapeDtypeStruct((m, D_MODEL), F32)
    else:
        prompt_tiles = split_rows[0] // tm
        out_specs = [pl.BlockSpec((tm, D_MODEL), lambda i: (jnp.minimum(i, prompt_tiles - 1), 0)),
                     pl.BlockSpec((tm, D_MODEL), lambda i: (jnp.maximum(i - prompt_tiles, 0), 0))]
        out_shape = [jax.ShapeDtypeStruct((r, D_MODEL), F32) for r in split_rows]
    return pl.pallas_call(
        functools.partial(_ple_kernel, prompt_tiles=prompt_tiles),
        grid=(m // tm,),
        in_specs=[pl.BlockSpec((tm, D_MODEL), lambda i: (i, 0)),
                  pl.BlockSpec((1, D_MODEL), lambda i: (0, 0)),
                  pl.BlockSpec((D_MODEL, D_MODEL), lambda i: (0, 0), **resident),
                  pl.BlockSpec((tm, PLE_DIM), lambda i: (i, 0)),
                  pl.BlockSpec((PLE_DIM, D_MODEL), lambda i: (0, 0), **resident),
                  pl.BlockSpec((1, D_MODEL), lambda i: (0, 0))],
        out_specs=out_specs,
        out_shape=out_shape,
        compiler_params=_cparams(("arbitrary",)),
        name="ple" if split_rows is None else "ple_final",
    )(x, g, wg, p, wp, gf)


LANES = 2


class _Seqs:
    def __init__(self, bp, tp, bs, ts):
        assert tp % CHUNK == 0 and ts % CHUNK == 0 and bp % LANES == 0 and bs % LANES == 0
        self.cp, self.cs = tp // CHUNK, ts // CHUNK
        self.bp, self.bs = bp, bs
        self.npc = bp * self.cp
        self.lane_p = self.npc // LANES
        self.lane_s = bs * self.cs // LANES
        self.n_steps = self.lane_p + self.lane_s
        self.lane_seqs = (bp + bs) // LANES

    def _split(self, s):
        in_p = s < self.lane_p
        ss = s - self.lane_p
        pos = jnp.where(in_p, s % self.cp, ss % self.cs)
        return in_p, ss, pos

    def chunk(self, lane, s):
        in_p, ss, _ = self._split(s)
        return jnp.where(in_p, lane * self.lane_p + s, self.npc + lane * self.lane_s + ss)

    def seq(self, lane, s):
        in_p, ss, _ = self._split(s)
        return jnp.where(in_p, lane * (self.bp // LANES) + s // self.cp,
                         self.bp + lane * (self.bs // LANES) + ss // self.cs)

    def lane_seq(self, s):
        in_p, ss, _ = self._split(s)
        return jnp.where(in_p, s // self.cp, self.bp // LANES + ss // self.cs)

    def first(self, s):
        return self._split(s)[2] == 0

    def last(self, s):
        in_p, _, pos = self._split(s)
        return pos == jnp.where(in_p, self.cp - 1, self.cs - 1)

    def rope_block(self, s):
        in_p, _, pos = self._split(s)
        return jnp.where(in_p, pos, self.cp + pos)

    def of_chunk(self, c):
        in_p = c < self.npc
        cc = c - self.npc
        return (jnp.where(in_p, c // self.lane_p, cc // self.lane_s),
                jnp.where(in_p, c % self.lane_p, self.lane_p + cc % self.lane_s))

    def unlane(self, per_lane):
        hp = self.bp // LANES
        return jnp.concatenate([a[:hp] for a in per_lane] + [a[hp:] for a in per_lane], axis=0)

    def lane_row_block(self, i, tm, mp, ms):
        r0 = i * tm
        in_p = r0 < mp
        rs = r0 - mp
        lane = jnp.where(in_p, r0 // (mp // LANES), rs // (ms // LANES))
        local = jnp.where(in_p, r0 % (mp // LANES), mp // LANES + rs % (ms // LANES))
        return lane, local // tm


def _shifted(ext_ref, x, n_prev):
    ext_ref[HALO:HALO + CHUNK] = x
    return [ext_ref[HALO - j:HALO - j + CHUNK] for j in range(1, n_prev + 1)]


def _roll_halo(ext_ref):
    ext_ref[0:HALO] = ext_ref[CHUNK:CHUNK + HALO]


def _bd_mask():
    r = lax.broadcasted_iota(jnp.int32, (GROUP_W, GROUP_W), 0) // RWKV_HEAD
    c = lax.broadcasted_iota(jnp.int32, (GROUP_W, GROUP_W), 1) // RWKV_HEAD
    return r == c


def _rwkv_chunk(al, bt, r, k, v, lg, logd, sbd_ref):
    groups = range(al.shape[1] // GROUP_W)
    bdm = _bd_mask()
    tt = lax.broadcasted_iota(jnp.int32, (CHUNK, GROUP_W), 0)
    ss = lax.broadcasted_iota(jnp.int32, (CHUNK, GROUP_W), 1) % RWKV_HEAD
    low_s, low_i = tt > ss, tt >= ss
    eye = jnp.where(tt == ss, 1.0, 0.0)

    def sl(x, i):
        return x[:, i * GROUP_W:(i + 1) * GROUP_W]

    def bd(x16):
        return jnp.where(bdm, jnp.concatenate([x16] * RWKV_GROUP, axis=0), jnp.zeros((), BF16))

    def stack16(a, b):
        return jnp.concatenate([a, b], axis=0).astype(BF16)

    e_in = jnp.exp(lg)
    e_inv = jnp.exp(-lg)
    at = al * jnp.exp(lg - logd)
    rt = r * e_in
    kh = k * e_inv
    bh = bt * e_inv
    e_end = e_in[CHUNK - 1:CHUNK, :]
    bh_end = (bh * e_end).astype(BF16)
    kh_end = (kh * e_end).astype(BF16)
    v16 = v.astype(BF16)

    lhs = [stack16(sl(at, i), sl(rt, i)) for i in groups]
    x = [_dot_nt(lhs[i], jnp.concatenate([bd(sl(bh, i).astype(BF16)), bd(sl(kh, i).astype(BF16))], axis=0))
         for i in groups]
    p = [jnp.where(low_s, x[i][:CHUNK, :GROUP_W], 0.0) for i in groups]
    q = [jnp.where(low_s, x[i][:CHUNK, GROUP_W:], 0.0) for i in groups]
    rb = [jnp.where(low_i, x[i][CHUNK:, :GROUP_W], 0.0) for i in groups]
    rk = [jnp.where(low_i, x[i][CHUNK:, GROUP_W:], 0.0) for i in groups]
    p16 = [p[i].astype(BF16) for i in groups]
    a = [_dot(p16[i], bd(p16[i])) for i in groups]
    t = [eye + p[i] for i in groups]
    for _ in range(4):
        res = [_dot(stack16(t[i], a[i]), bd(a[i].astype(BF16))) for i in groups]
        t = [t[i] + res[i][:CHUNK] for i in groups]
        a = [res[i][CHUNK:] for i in groups]
    t = [t[i] + _dot(t[i].astype(BF16), bd(a[i].astype(BF16))) for i in groups]
    xm = [_dot_nt(lhs[i], sbd_ref[i].astype(BF16)) for i in groups]
    xv = [_dot(stack16(q[i], rk[i]), bd(sl(v16, i))) for i in groups]
    u16 = [_dot(t[i].astype(BF16), bd((xm[i][:CHUNK] + xv[i][:CHUNK]).astype(BF16))).astype(BF16) for i in groups]
    y = [xm[i][CHUNK:] + xv[i][CHUNK:] + _dot(rb[i].astype(BF16), bd(u16[i])) for i in groups]
    for i in groups:
        upd = _dot_tn(jnp.concatenate([u16[i], sl(v16, i)], axis=0),
                      jnp.concatenate([sl(bh_end, i), sl(kh_end, i)], axis=0))
        sbd_ref[i] = sbd_ref[i] * sl(e_end, i) + jnp.where(bdm, upd, 0.0)
    return jnp.concatenate(y, axis=1)


RWKV_LANE_IN, RWKV_SHARED, RWKV_LANE_OUT, RWKV_LANE_SCRATCH = 5, 12, 3, 2


def _rwkv_kernel(seqs, *refs):
    lane_in = [refs[l * RWKV_LANE_IN:(l + 1) * RWKV_LANE_IN] for l in range(LANES)]
    refs = refs[LANES * RWKV_LANE_IN:]
    (mu_ref, mut_ref, w0_ref, w2_ref, a0_ref, a2_ref, g2_ref, kk_ref, ka_ref, rk_ref, lnw_ref, lnb_ref) = \
        refs[:RWKV_SHARED]
    o_ref = refs[RWKV_SHARED]
    refs = refs[RWKV_SHARED + 1:]
    lane_out = [refs[l * RWKV_LANE_OUT:(l + 1) * RWKV_LANE_OUT] for l in range(LANES)]
    refs = refs[LANES * RWKV_LANE_OUT:]
    lane_scr = [refs[l * RWKV_LANE_SCRATCH:(l + 1) * RWKV_LANE_SCRATCH] for l in range(LANES)]
    sbd_ref = refs[LANES * RWKV_LANE_SCRATCH]
    s = pl.program_id(0)
    n_heads = MIX_W // RWKV_HEAD

    def state_blocks(l):
        for h in range(n_heads):
            g, hh = divmod(h, RWKV_GROUP)
            yield h, l * N_GROUPS + g, slice(hh * RWKV_HEAD, (hh + 1) * RWKV_HEAD)

    @pl.when(seqs.first(s))
    def _():
        sbd_ref[...] = jnp.zeros_like(sbd_ref)
        for l in range(LANES):
            _, _, sh0_ref, sht0_ref, s0_ref = lane_in[l]
            ext_ref, extt_ref = lane_scr[l]
            ext_ref[0:HALO] = sh0_ref[0]
            extt_ref[0:HALO] = sht0_ref[0]
            for h, g, blk in state_blocks(l):
                sbd_ref[g, blk, blk] = s0_ref[0, h]

    ones_bd = jnp.where(_bd_mask(), 1.0, 0.0).astype(BF16)
    ti = lax.broadcasted_iota(jnp.int32, (CHUNK, CHUNK), 0)
    si = lax.broadcasted_iota(jnp.int32, (CHUNK, CHUNK), 1)
    tril16 = jnp.where(ti >= si, 1.0, 0.0).astype(BF16)

    def head_sum(x):
        x16 = x.astype(BF16)
        return jnp.concatenate(
            [_dot(x16[:, i * GROUP_W:(i + 1) * GROUP_W], ones_bd) for i in range(N_GROUPS)], axis=1)

    def prepare(l):
        z_ref, tail_ref = lane_in[l][:2]
        ext_ref, extt_ref = lane_scr[l]
        z = z_ref[...]
        (zp,) = _shifted(ext_ref, z, 1)
        zs = z + mu_ref[...] * (zp - z)
        _roll_halo(ext_ref)
        zt = tail_ref[...]
        (ztp,) = _shifted(extt_ref, zt, 1)
        lora = zt + mut_ref[...] * (ztp - zt)
        _roll_halo(extt_ref)
        r = zs[:, 0:MIX_W]
        k = zs[:, MIX_W:2 * MIX_W]
        v = zs[:, 2 * MIX_W:3 * MIX_W]
        zw = lora[:, ZW_OFF:ZW_OFF + 128]
        za = lora[:, ZA_OFF:ZA_OFF + 128]
        zg = lora[:, ZG_OFF:ZG_OFF + 256]
        logd = -float(np.exp(-0.5)) * _sigmoid(w0_ref[...] + _dot(jnp.tanh(zw).astype(BF16), w2_ref[...]))
        a = _sigmoid(a0_ref[...] + _dot(za.astype(BF16), a2_ref[...]))
        g = _dot(_sigmoid(zg).astype(BF16), g2_ref[...])
        kk = k * kk_ref[...]
        kk = kk * lax.rsqrt(jnp.maximum(head_sum(kk * kk), 1e-24))
        k2 = k * (1.0 + (a - 1.0) * ka_ref[...])
        lg = _split_dot_left(tril16, logd)
        return dict(al=-kk, bt=kk * a, r=r, k=k2, v=v, lg=lg, logd=logd, g=g)

    pre = [prepare(l) for l in range(LANES)]
    both = lambda name: jnp.concatenate([p[name] for p in pre], axis=1)
    y_all = _rwkv_chunk(both("al"), both("bt"), both("r"), both("k"), both("v"), both("lg"), both("logd"), sbd_ref)

    inv_n = 1.0 / RWKV_HEAD
    for l in range(LANES):
        p = pre[l]
        y = y_all[:, l * MIX_W:(l + 1) * MIX_W]
        d = y - head_sum(y) * inv_n
        var = head_sum(d * d) * inv_n
        yn = d * lax.rsqrt(var + RWKV_LN_EPS) * lnw_ref[...] + lnb_ref[...]
        bonus = head_sum(p["r"] * p["k"] * rk_ref[...]) * p["v"]
        o_ref[0, l] = ((yn + bonus) * p["g"]).astype(BF16)

    @pl.when(seqs.last(s))
    def _():
        for l in range(LANES):
            sh_out_ref, sht_out_ref, s_out_ref = lane_out[l]
            ext_ref, extt_ref = lane_scr[l]
            sh_out_ref[0] = ext_ref[0:HALO]
            sht_out_ref[0] = extt_ref[0:HALO]
            for h, g, blk in state_blocks(l):
                s_out_ref[0, h] = sbd_ref[g, blk, blk]


def _mixer_out_spec():
    return pl.BlockSpec((1, LANES, CHUNK, MIX_W), lambda s: (s, 0, 0, 0))


def _mixer_out_shape(seqs):
    return jax.ShapeDtypeStruct((seqs.n_steps, LANES, CHUNK, MIX_W), BF16)


def _lane_specs(make):
    return [spec for lane in range(LANES) for spec in make(lane)]


def _rwkv(seqs, proj, sh0, sht0, s0, pr):
    n_heads = MIX_W // RWKV_HEAD
    row = lambda w: pl.BlockSpec((1, w), lambda s: (0, 0))
    lora_w = lambda k: pl.BlockSpec((k, MIX_W), lambda s: (0, 0))
    tail_blk = IN_COLS_PAD // TAIL_BLK - 1

    def lane_in(l):
        st3 = lambda w: pl.BlockSpec((1, HALO, w), lambda s: (seqs.seq(l, s), 0, 0))
        return [pl.BlockSpec((CHUNK, RKV_BLK), lambda s: (seqs.chunk(l, s), 2)),
                pl.BlockSpec((CHUNK, TAIL_BLK), lambda s: (seqs.chunk(l, s), tail_blk)),
                st3(RKV_BLK), st3(TAIL_BLK),
                pl.BlockSpec((1, n_heads, RWKV_HEAD, RWKV_HEAD), lambda s: (seqs.seq(l, s), 0, 0, 0))]

    def lane_out(l):
        st3 = lambda w: pl.BlockSpec((1, HALO, w), lambda s: (seqs.lane_seq(s), 0, 0))
        return [st3(RKV_BLK), st3(TAIL_BLK),
                pl.BlockSpec((1, n_heads, RWKV_HEAD, RWKV_HEAD), lambda s: (seqs.lane_seq(s), 0, 0, 0))]

    lane_shapes = [jax.ShapeDtypeStruct((seqs.lane_seqs, HALO, RKV_BLK), F32),
                   jax.ShapeDtypeStruct((seqs.lane_seqs, HALO, TAIL_BLK), F32),
                   jax.ShapeDtypeStruct((seqs.lane_seqs, n_heads, RWKV_HEAD, RWKV_HEAD), F32)]
    return pl.pallas_call(
        functools.partial(_rwkv_kernel, seqs),
        grid=(seqs.n_steps,),
        in_specs=_lane_specs(lane_in) + [
            row(RKV_BLK), row(TAIL_BLK), row(MIX_W), lora_w(128), row(MIX_W), lora_w(128), lora_w(256),
            row(MIX_W), row(MIX_W), row(MIX_W), row(MIX_W), row(MIX_W)],
        out_specs=[_mixer_out_spec()] + _lane_specs(lane_out),
        out_shape=[_mixer_out_shape(seqs)] + lane_shapes * LANES,
        scratch_shapes=[sh for _ in range(LANES) for sh in (pltpu.VMEM((HALO + CHUNK, RKV_BLK), F32),
                                                             pltpu.VMEM((HALO + CHUNK, TAIL_BLK), F32))]
        + [pltpu.VMEM((LANES * N_GROUPS, GROUP_W, GROUP_W), F32)],
        compiler_params=_cparams(("arbitrary",)),
        name="rwkv",
    )(*([proj, proj, sh0, sht0, s0] * LANES), pr["mu"], pr["mut"], pr["w0"], pr["w2"], pr["a0"], pr["a2"], pr["g2"],
      pr["kk"], pr["ka"], pr["rk"], pr["lnw"], pr["lnb"])


def _head_norm(x):
    mu = jnp.mean(x, axis=-1, keepdims=True)
    d = x - mu
    return d * lax.rsqrt(jnp.mean(d * d, axis=-1, keepdims=True) + HEAD_NORM_EPS)


def _mlstm_kernel(seqs, q_ref, k_ref, v_ref, op_ref, tail_ref, cv0_ref, c0_ref, n0_ref, m0_ref,
                  cw_ref, gb_ref, nw_ref,
                  o_ref, cv_out_ref, c_ref, n_ref, m_ref,
                  ext_ref):
    _, step = seqs.of_chunk(pl.program_id(0))

    @pl.when(seqs.first(step))
    def _():
        ext_ref[0:HALO] = cv0_ref[0]
        c_ref[...] = c0_ref[...]
        n_ref[...] = n0_ref[...]
        m_ref[...] = m0_ref[...]

    x0 = jnp.concatenate([q_ref[...], k_ref[...]], axis=1)
    x1, x2, x3 = _shifted(ext_ref, x0, MLSTM_CONV - 1)
    qk = x3 * cw_ref[0:1, :] + x2 * cw_ref[1:2, :] + x1 * cw_ref[2:3, :] + x0 * cw_ref[3:4, :]
    _roll_halo(ext_ref)
    qk = qk * _sigmoid(qk)
    q_all = qk[:, :MIX_W]
    k_all = qk[:, MIX_W:] * (M_HEAD ** -0.5)

    gates = tail_ref[:, MGATE_BLK * 128:(MGATE_BLK + 1) * 128] + gb_ref[...]
    lsf = jnp.minimum(gates, 0.0) - jnp.log(1.0 + jnp.exp(-jnp.abs(gates)))
    ti = lax.broadcasted_iota(jnp.int32, (CHUNK, CHUNK), 0)
    si = lax.broadcasted_iota(jnp.int32, (CHUNK, CHUNK), 1)
    tril = ti >= si
    bcum_col = _dot_f32(jnp.where(tril, 1.0, 0.0), lsf)
    ig_rows = gates.T[MGATE_LANE:MGATE_LANE + HALO]
    bcum_rows = _dot_f32(lsf.T[MGATE_LANE:MGATE_LANE + HALO], jnp.where(ti <= si, 1.0, 0.0))
    lane = lax.broadcasted_iota(jnp.int32, (1, 128), 1)
    m_row = m_ref[0]
    heads = range(M_HEADS)
    hsl = [slice(h * M_HEAD, (h + 1) * M_HEAD) for h in heads]

    bc = [bcum_col[:, MGATE_LANE + M_HEADS + h:MGATE_LANE + M_HEADS + h + 1] for h in heads]
    ic = [gates[:, MGATE_LANE + h:MGATE_LANE + h + 1] for h in heads]
    log_inter = [bc[h] + m_row[:, h:h + 1] for h in heads]
    log_intra = [jnp.where(tril, bc[h] - bcum_rows[M_HEADS + h:M_HEADS + h + 1, :] + ig_rows[h:h + 1, :], NEG_BIG)
                 for h in heads]
    m_t = [jnp.maximum(log_inter[h], jnp.max(log_intra[h], axis=-1, keepdims=True)) for h in heads]
    w_inter = [jnp.exp(log_inter[h] - m_t[h]) for h in heads]
    w_intra = [jnp.exp(log_intra[h] - m_t[h]) for h in heads]
    m_end = [m_t[h][CHUNK - 1:CHUNK, :] for h in heads]
    w_end = [jnp.exp(bc[h][CHUNK - 1:CHUNK, :] - bc[h] + ic[h] - m_end[h]) for h in heads]
    g_end = [w_inter[h][CHUNK - 1:CHUNK, :] for h in heads]
    q = [q_all[:, hsl[h]] for h in heads]
    k = [k_all[:, hsl[h]] for h in heads]
    v = [v_ref[:, hsl[h]] for h in heads]
    q16 = [q[h].astype(BF16) for h in heads]
    k16 = [k[h].astype(BF16) for h in heads]
    v16 = [v[h].astype(BF16) for h in heads]
    s = [_dot_nt(q16[h], k16[h]) * w_intra[h] for h in heads]
    qc = [_dot(q16[h], c_ref[0, h].astype(BF16)) for h in heads]
    sv = [_dot(s[h].astype(BF16), v16[h]) for h in heads]
    kv = [_dot_tn(k16[h], (w_end[h] * v[h]).astype(BF16)) for h in heads]
    m_new_row = m_row
    for h in heads:
        n_h = n_ref[0, h:h + 1, :]
        num = w_inter[h] * qc[h] + sv[h]
        den = w_inter[h] * jnp.sum(q[h] * n_h, axis=-1, keepdims=True) + jnp.sum(s[h], axis=-1, keepdims=True)
        hh = num / jnp.maximum(jnp.abs(den), jnp.exp(-m_t[h]))
        c_ref[0, h] = g_end[h] * c_ref[0, h] + kv[h]
        n_ref[0, h:h + 1, :] = g_end[h] * n_h + jnp.sum(w_end[h] * k[h], axis=0, keepdims=True)
        m_new_row = jnp.where(lane == h, m_end[h], m_new_row)
        o_ref[0, 0, :, hsl[h]] = (_head_norm(hh) * nw_ref[:, hsl[h]] * _sigmoid(op_ref[:, hsl[h]])).astype(BF16)

    m_ref[0] = m_new_row

    @pl.when(seqs.last(step))
    def _():
        cv_out_ref[0] = ext_ref[0:HALO]


def _mlstm(seqs, proj, cv0, c0, n0, m0, pr):
    n_seq = seqs.bp + seqs.bs
    blk = lambda j: pl.BlockSpec((CHUNK, MIX_BLK), lambda c: (c, j))
    sq = lambda c: seqs.seq(*seqs.of_chunk(c))
    st_cv = pl.BlockSpec((1, HALO, 2 * MIX_W), lambda c: (sq(c), 0, 0))
    st_c = pl.BlockSpec((1, M_HEADS, M_HEAD, M_HEAD), lambda c: (sq(c), 0, 0, 0))
    st_n = pl.BlockSpec((1, M_HEADS, M_HEAD), lambda c: (sq(c), 0, 0))
    st_m = pl.BlockSpec((1, 1, 128), lambda c: (sq(c), 0, 0))
    return pl.pallas_call(
        functools.partial(_mlstm_kernel, seqs),
        grid=(seqs.npc + seqs.bs * seqs.cs,),
        in_specs=[blk(9), blk(10), blk(11), blk(12),
                  pl.BlockSpec((CHUNK, TAIL_BLK), lambda c: (c, IN_COLS_PAD // TAIL_BLK - 1)),
                  st_cv, st_c, st_n, st_m,
                  pl.BlockSpec((MLSTM_CONV, 2 * MIX_W), lambda c: (0, 0)),
                  pl.BlockSpec((1, 128), lambda c: (0, 0)),
                  pl.BlockSpec((1, MIX_W), lambda c: (0, 0))],
        out_specs=[pl.BlockSpec((1, 1, CHUNK, MIX_W), lambda c: seqs.of_chunk(c)[::-1] + (0, 0)),
                   st_cv, st_c, st_n, st_m],
        out_shape=[_mixer_out_shape(seqs),
                   jax.ShapeDtypeStruct((n_seq, HALO, 2 * MIX_W), F32),
                   jax.ShapeDtypeStruct((n_seq, M_HEADS, M_HEAD, M_HEAD), F32),
                   jax.ShapeDtypeStruct((n_seq, M_HEADS, M_HEAD), F32),
                   jax.ShapeDtypeStruct((n_seq, 1, 128), F32)],
        scratch_shapes=[pltpu.VMEM((HALO + CHUNK, 2 * MIX_W), F32)],
        compiler_params=_cparams(("arbitrary",)),
        name="mlstm",
    )(proj, proj, proj, proj, proj, cv0, c0, n0, m0, pr["cw"], pr["gb"], pr["nw"])


RET_LANE_IN = 5


def _ret_kernel(seqs, *refs):
    lane_in = [refs[l * RET_LANE_IN:(l + 1) * RET_LANE_IN] for l in range(LANES)]
    refs = refs[LANES * RET_LANE_IN:]
    cos_ref, sin_ref, o_ref = refs[:3]
    r_refs = refs[3:]
    s = pl.program_id(0)

    @pl.when(seqs.first(s))
    def _():
        for l in range(LANES):
            r_refs[l][...] = lane_in[l][4][...]

    cos, sin = cos_ref[...], sin_ref[...]
    half = M_HEAD // 2
    ti = lax.broadcasted_iota(jnp.int32, (CHUNK, CHUNK), 0)
    si = lax.broadcasted_iota(jnp.int32, (CHUNK, CHUNK), 1)
    diff = (ti - si).astype(F32)
    t_col = lax.broadcasted_iota(jnp.int32, (CHUNK, 1), 0).astype(F32)

    def rot(u):
        u1, u2 = u[:, :half], u[:, half:]
        return jnp.concatenate([u1 * cos - u2 * sin, u1 * sin + u2 * cos], axis=1)

    items = [(l, h) for l in range(LANES) for h in range(M_HEADS)]
    idx = range(len(items))
    hsl = [slice(h * M_HEAD, (h + 1) * M_HEAD) for _, h in items]
    log_gamma = [float(np.log(1.0 - 2.0 ** (-5.0 - h))) for _, h in items]
    decay_in = {h: jnp.where(diff >= 0, jnp.exp(float(np.log(1.0 - 2.0 ** (-5.0 - h))) * jnp.maximum(diff, 0.0)), 0.0)
                for h in range(M_HEADS)}
    q16 = [rot(lane_in[l][0][:, hsl[i]]).astype(BF16) for i, (l, _) in enumerate(items)]
    k = [rot(lane_in[l][1][:, hsl[i]]) * (M_HEAD ** -0.5) for i, (l, _) in enumerate(items)]
    v16 = [lane_in[l][2][:, hsl[i]].astype(BF16) for i, (l, _) in enumerate(items)]
    sm = [_dot_nt(q16[i], k[i].astype(BF16)) * decay_in[h] for i, (_, h) in enumerate(items)]
    qr = [_dot(q16[i], r_refs[l][0, h].astype(BF16)) for i, (l, h) in enumerate(items)]
    sv = [_dot(sm[i].astype(BF16), v16[i]) for i in idx]
    kv = [_dot_tn((k[i] * jnp.exp(log_gamma[i] * (CHUNK - 1.0 - t_col))).astype(BF16), v16[i]) for i in idx]
    for i, (l, h) in enumerate(items):
        o = sv[i] + jnp.exp(log_gamma[i] * (t_col + 1.0)) * qr[i]
        r_refs[l][0, h] = float(np.exp(log_gamma[i] * CHUNK)) * r_refs[l][0, h] + kv[i]
        gt = lane_in[l][3][:, hsl[i]]
        o_ref[0, l, :, hsl[i]] = (_head_norm(o) * (gt * _sigmoid(gt))).astype(BF16)


def _retention(seqs, proj, cos, sin, r0):
    rope = pl.BlockSpec((CHUNK, M_HEAD // 2), lambda s: (seqs.rope_block(s), 0))
    st_r = lambda index: pl.BlockSpec((1, M_HEADS, M_HEAD, M_HEAD), lambda s: (index(s), 0, 0, 0))

    def lane_in(l):
        blk = lambda j: pl.BlockSpec((CHUNK, MIX_BLK), lambda s: (seqs.chunk(l, s), j))
        return [blk(13), blk(14), blk(15), blk(16), st_r(lambda s: seqs.seq(l, s))]

    return pl.pallas_call(
        functools.partial(_ret_kernel, seqs),
        grid=(seqs.n_steps,),
        in_specs=_lane_specs(lane_in) + [rope, rope],
        out_specs=[_mixer_out_spec()] + [st_r(seqs.lane_seq)] * LANES,
        out_shape=[_mixer_out_shape(seqs)]
        + [jax.ShapeDtypeStruct((seqs.lane_seqs, M_HEADS, M_HEAD, M_HEAD), F32)] * LANES,
        compiler_params=_cparams(("arbitrary",)),
        name="retention",
    )(*([proj] * 4 + [r0]) * LANES, cos, sin)


def _halo_rows(prev, n_prompt):
    k = prev.shape[1]
    return jnp.pad(prev, ((n_prompt, 0), (HALO - k, 0), (0, 0)))


def _with_prompt(state, n_prompt):
    return jnp.pad(state, ((n_prompt, 0),) + ((0, 0),) * (state.ndim - 1))


def _rope_tables(tp, ts):
    half = M_HEAD // 2
    freq = ROPE_BASE ** (-jnp.arange(half, dtype=F32) / half)
    pos = jnp.concatenate([jnp.arange(tp), PAST_LEN + jnp.arange(ts)]).astype(F32)
    ang = pos[:, None] * freq
    return jnp.cos(ang), jnp.sin(ang)


def _tail_layout(lora, gates=None):
    lead = lora.shape[:-1]
    z = lambda n: jnp.zeros(lead + (n,), lora.dtype)
    mid = z(128 - ZA_W) if gates is None else jnp.concatenate(
        [z(MGATE_LANE - ZA_W), gates, z(128 - MGATE_LANE - gates.shape[-1])], axis=-1)
    return jnp.concatenate([lora[..., :ZW_W], z(128 - ZW_W), lora[..., ZW_W:ZW_W + ZA_W], mid,
                            lora[..., ZW_W + ZA_W:], z(256 - ZG_W)], axis=-1)


def _tail_lora(t):
    return jnp.concatenate([t[..., ZW_OFF:ZW_OFF + ZW_W], t[..., ZA_OFF:ZA_OFF + ZA_W], t[..., ZG_OFF:ZG_OFF + ZG_W]],
                           axis=-1)


def _permute_in_cols(w):
    rw = 3 * MIX_W + LORA_W
    mb = rw
    tb = mb + 4 * MIX_W + 2 * M_HEADS
    gb = tb + 4 * MIX_W
    return jnp.concatenate([
        w[..., gb:gb + 3 * D_MODEL],
        w[..., 0:3 * MIX_W],
        w[..., mb:mb + 4 * MIX_W],
        w[..., tb:tb + 4 * MIX_W],
        _tail_layout(w[..., 3 * MIX_W:rw], w[..., mb + 4 * MIX_W:tb])], axis=-1)


def _lora_rows(w, k):
    return jnp.pad(w, ((0, k - w.shape[0]), (0, 0))).astype(BF16)


def kernel(x_prompt, x_sample, state_rwkv_shift, state_rwkv_wkv, state_mlstm_conv, state_mlstm_c, state_mlstm_n, state_mlstm_m, state_ret, state_ffn_conv, p_prompt, p_sample, norm_mix, w_in, rwkv_mu, rwkv_w0, rwkv_w2, rwkv_a0, rwkv_a2, rwkv_g2, rwkv_kk, rwkv_ka, rwkv_rk, rwkv_lnw, rwkv_lnb, mlstm_conv, mlstm_bi, mlstm_bf, mlstm_nw, w_branch, w_out, norm_ffn, ffn_up, ffn_conv, ffn_down, norm_ple, ple_proj, ple_gate, norm_final):
    bp, tp, _ = x_prompt.shape
    bs, ts, _ = x_sample.shape
    mp, ms = bp * tp, bs * ts
    depth = w_in.shape[0]
    seqs = _Seqs(bp, tp, bs, ts)
    x = jnp.concatenate([x_prompt.reshape(mp, D_MODEL), x_sample.reshape(ms, D_MODEL)], axis=0)
    cos, sin = _rope_tables(tp, ts)
    row = lambda a: a.reshape(1, -1)
    new_states = []
    w_in_p = _permute_in_cols(w_in).astype(BF16)
    for i in range(depth):
        proj = _norm_matmul(x, row(norm_mix[i]), w_in_p, i, "in_proj")

        shift = state_rwkv_shift[i][:, None, :]
        mu = rwkv_mu[i]
        rwkv_pr = dict(
            mu=row(mu[:3 * MIX_W]), mut=row(_tail_layout(mu[3 * MIX_W:])),
            w0=row(rwkv_w0[i]), w2=_lora_rows(rwkv_w2[i], 128),
            a0=row(rwkv_a0[i]), a2=_lora_rows(rwkv_a2[i], 128),
            g2=_lora_rows(rwkv_g2[i], 256),
            kk=row(rwkv_kk[i]), ka=row(rwkv_ka[i]), rk=row(rwkv_rk[i]),
            lnw=row(rwkv_lnw[i]), lnb=row(rwkv_lnb[i]))
        o_r, *rwkv_st = _rwkv(
            seqs, proj,
            _halo_rows(shift[:, :, :3 * MIX_W], bp),
            _halo_rows(_tail_layout(shift[:, :, 3 * MIX_W:]), bp),
            _with_prompt(state_rwkv_wkv[i], bp), rwkv_pr)

        gate_bias = jnp.pad(jnp.concatenate([mlstm_bi[i], mlstm_bf[i]]),
                            (MGATE_LANE, 128 - MGATE_LANE - 2 * M_HEADS))
        mlstm_pr = dict(cw=mlstm_conv[i], gb=row(gate_bias), nw=row(mlstm_nw[i]))
        m0 = jnp.pad(state_mlstm_m[i], ((0, 0), (0, 128 - M_HEADS)))[:, None, :]
        o_m, cv_new, c_new, n_new, m_new = _mlstm(
            seqs, proj, _halo_rows(state_mlstm_conv[i], bp), _with_prompt(state_mlstm_c[i], bp),
            _with_prompt(state_mlstm_n[i], bp), _with_prompt(m0, bp), mlstm_pr)

        o_t, *ret_st = _retention(seqs, proj, cos, sin, _with_prompt(state_ret[i], bp))
        sh_new, sht_new, wkv_new = (seqs.unlane(rwkv_st[j::3]) for j in range(3))
        ret_new = seqs.unlane(ret_st)

        x = _merge(seqs, mp, ms, x, proj, o_r, o_m, o_t, w_branch[i].astype(BF16), w_out[i].astype(BF16))

        wa_wb, wd = ffn_up[i].astype(BF16), ffn_down[i].astype(BF16)
        x, fcv_p = _ffn(x, row(norm_ffn[i]), wa_wb, wd, ffn_conv[i], bp, tp, 0)
        x, fcv_s = _ffn(x, row(norm_ffn[i]), wa_wb, wd, ffn_conv[i], bs, ts, mp,
                        st=jnp.pad(state_ffn_conv[i], ((0, 0), (HALO - FFN_CONV + 1, 0), (0, 0))))
        fcv_new = jnp.concatenate([fcv_p.reshape(bp, -1, HALO, D_FF)[:, -1], fcv_s], axis=0)

        p = jnp.concatenate([p_prompt[i].reshape(mp, PLE_DIM), p_sample[i].reshape(ms, PLE_DIM)], axis=0)
        x = _ple(x, row(norm_ple[i]), ple_gate[i].astype(BF16), p, ple_proj[i].astype(BF16),
                 row(norm_final), split_rows=(mp, ms) if i == depth - 1 else None)

        shift_new = jnp.concatenate([sh_new[:, HALO - 1, :], _tail_lora(sht_new[:, HALO - 1, :])], axis=-1)
        new_states.append((shift_new, wkv_new, cv_new[:, HALO - MLSTM_CONV + 1:, :], c_new, n_new,
                           m_new[:, 0, :M_HEADS], ret_new, fcv_new[:, HALO - FFN_CONV + 1:, :]))

    stacked = [jnp.stack(s, axis=0) for s in zip(*new_states)]
    y_prompt = x[0].reshape(bp, tp, D_MODEL)
    y_sample = x[1].reshape(bs, ts, D_MODEL)
    return (y_prompt, y_sample) + tuple(s[:, :bp] for s in stacked) + tuple(s[:, bp:] for s in stacked)
```

```python
import functools

import numpy as np
import jax
import jax.numpy as jnp
from jax import lax
from jax.experimental import pallas as pl
from jax.experimental.pallas import tpu as pltpu

F32 = jnp.float32
BF16 = jnp.bfloat16

D_MODEL = 2048
CHUNK = 64
MIX_W = 1024
RWKV_HEAD = 64
RWKV_GROUP = 2
GROUP_W = RWKV_HEAD * RWKV_GROUP
N_GROUPS = MIX_W // GROUP_W
LORA_W = 64 + 64 + 160
RWKV_LN_EPS = 64e-5
M_HEADS = 4
M_HEAD = 256
MLSTM_CONV = 4
D_FF = 5632
FFN_CONV = 3
PLE_DIM = 256
PAST_LEN = 4096
ROPE_BASE = 10000.0
NORM_EPS = 1e-6
HEAD_NORM_EPS = 1e-5
HALO = 8
NEG_BIG = -1e30

GATE_BLK = 2048
RKV_BLK = 3072
MIX_BLK = 1024
TAIL_BLK = 512
ZW_OFF, ZW_W = 0, 64
ZA_OFF, ZA_W = 128, 64
ZG_OFF, ZG_W = 256, 160
MGATE_BLK, MGATE_LANE = 1, 64
IN_COLS_PAD = 17920
VMEM_LIMIT = 56 * 1024 * 1024


def _cparams(sem):
    return pltpu.CompilerParams(dimension_semantics=sem, vmem_limit_bytes=VMEM_LIMIT)


def _pick(n, prefs):
    for p in prefs:
        if n % p == 0:
            return p
    raise ValueError(f"no tile for {n}")


def _sigmoid(x):
    return jax.nn.sigmoid(x)


def _rms(x, g):
    return x * lax.rsqrt(jnp.mean(x * x, axis=-1, keepdims=True) + NORM_EPS) * g


def _dot(a, b):
    return jnp.dot(a, b, preferred_element_type=F32)


def _dot_nt(a, b):
    return lax.dot_general(a, b, (((1,), (1,)), ((), ())), preferred_element_type=F32)


def _dot_tn(a, b):
    return lax.dot_general(a, b, (((0,), (0,)), ((), ())), preferred_element_type=F32)


def _dot_f32(a, b):
    return jnp.dot(a, b, preferred_element_type=F32, precision=lax.Precision.HIGHEST)


def _split_dot_left(w_bf16, x):
    hi = x.astype(BF16)
    lo = (x - hi.astype(F32)).astype(BF16)
    return _dot(w_bf16, hi) + _dot(w_bf16, lo)


def _norm_mm_kernel(x_ref, g_ref, w_ref, o_ref, xn_ref):
    @pl.when(pl.program_id(1) == 0)
    def _():
        xn_ref[...] = _rms(x_ref[...], g_ref[...]).astype(BF16)

    o_ref[...] = _dot(xn_ref[...], w_ref[...])


def _norm_matmul(x, g, w, layer, name):
    m, d = x.shape
    n = w.shape[2]
    tm = _pick(m, (1024, 512, 256, 128, 64))
    tn = _pick(n, (1280, 1024, 512, 256, 128))
    return pl.pallas_call(
        _norm_mm_kernel,
        grid=(m // tm, n // tn),
        in_specs=[pl.BlockSpec((tm, d), lambda i, j: (i, 0)),
                  pl.BlockSpec((1, d), lambda i, j: (0, 0)),
                  pl.BlockSpec((None, d, tn), lambda i, j: (layer, 0, j))],
        out_specs=pl.BlockSpec((tm, tn), lambda i, j: (i, j)),
        out_shape=jax.ShapeDtypeStruct((m, n), F32),
        scratch_shapes=[pltpu.VMEM((tm, d), BF16)],
        compiler_params=_cparams(("parallel", "arbitrary")),
        name=name,
    )(x, g, w)


FFN_TN = 512
FFN_RB = 256


def _ffn_kernel(x_ref, g_ref, wa_ref, wb_ref, wd_ref, cw_ref, *rest, tm, tiles_per_seq, seq_rows):
    if tiles_per_seq:
        o_ref, fcv_ref, xn_ref, ext_ref, carry_ref = rest
    else:
        st_ref, o_ref, fcv_ref, xn_ref, ext_ref = rest
    i, j = pl.program_id(0), pl.program_id(1)
    rb_rows = min(FFN_RB, tm)

    @pl.when(j == 0)
    def _():
        x = x_ref[...]
        xn_ref[...] = _rms(x, g_ref[...]).astype(BF16)
        o_ref[...] = x

    if tiles_per_seq:
        @pl.when(i % tiles_per_seq == 0)
        def _():
            ext_ref[0:HALO] = jnp.zeros((HALO, FFN_TN), F32)

        @pl.when(i % tiles_per_seq != 0)
        def _():
            ext_ref[0:HALO] = carry_ref[j]

    def up(rb):
        r0 = rb * rb_rows
        xn = xn_ref[r0:r0 + rb_rows]
        x0 = _dot(xn, wa_ref[...])
        ext_ref[HALO + r0:HALO + r0 + rb_rows] = x0
        return x0, _dot(xn, wb_ref[...])

    def gated(rb, x0, b):
        r0 = rb * rb_rows
        x1 = ext_ref[HALO - 1 + r0:HALO - 1 + r0 + rb_rows]
        x2 = ext_ref[HALO - 2 + r0:HALO - 2 + r0 + rb_rows]
        if not tiles_per_seq:
            loc = lax.broadcasted_iota(jnp.int32, (rb_rows, 1), 0) % seq_rows
            seq0 = r0 // seq_rows
            prev = lambda row: jnp.concatenate(
                [jnp.broadcast_to(st_ref[seq0 + q, row:row + 1, :], (seq_rows, FFN_TN))
                 for q in range(rb_rows // seq_rows)], axis=0)
            s1, s2 = prev(HALO - 1), prev(HALO - 2)
            x1 = jnp.where(loc == 0, s1, x1)
            x2 = jnp.where(loc == 0, s2, jnp.where(loc == 1, s1, x2))
        a = x2 * cw_ref[0:1, :] + x1 * cw_ref[1:2, :] + x0 * cw_ref[2:3, :]
        return (0.5 * a * (1.0 + lax.erf(a * float(np.sqrt(0.5)))) * b).astype(BF16)

    n_rb = tm // rb_rows
    pending = up(0)
    for rb in range(n_rb):
        following = up(rb + 1) if rb + 1 < n_rb else None
        act = gated(rb, *pending)
        o_ref[rb * rb_rows:(rb + 1) * rb_rows] += _dot(act, wd_ref[...])
        pending = following

    if tiles_per_seq:
        carry_ref[j] = ext_ref[tm:tm + HALO]
        fcv_ref[0] = ext_ref[tm:tm + HALO]
    else:
        for q in range(tm // seq_rows):
            fcv_ref[q] = ext_ref[(q + 1) * seq_rows:(q + 1) * seq_rows + HALO]


def _ffn(x, g, wa_wb, wd, cw, n_seq, seq_rows, row0, st=None):
    m = x.shape[0]
    nj = D_FF // FFN_TN
    if st is None:
        tm = _pick(seq_rows, (1024, 512, 256, 128, 64))
        tiles_per_seq = seq_rows // tm
        seqs_per_tile = 1
    else:
        tm = _pick(n_seq * seq_rows, (1024, 512, 256, 128, 64))
        assert tm % seq_rows == 0 and min(FFN_RB, tm) % seq_rows == 0
        tiles_per_seq = 0
        seqs_per_tile = tm // seq_rows
    assert row0 % tm == 0
    i0 = row0 // tm
    n_tiles = n_seq * seq_rows // tm
    in_specs = [pl.BlockSpec((tm, D_MODEL), lambda i, j: (i0 + i, 0)),
                pl.BlockSpec((1, D_MODEL), lambda i, j: (0, 0)),
                pl.BlockSpec((D_MODEL, FFN_TN), lambda i, j: (0, j)),
                pl.BlockSpec((D_MODEL, FFN_TN), lambda i, j: (0, nj + j)),
                pl.BlockSpec((FFN_TN, D_MODEL), lambda i, j: (j, 0)),
                pl.BlockSpec((FFN_CONV, FFN_TN), lambda i, j: (0, j))]
    args = [x, g, wa_wb, wa_wb, wd, cw]
    scratch = [pltpu.VMEM((tm, D_MODEL), BF16), pltpu.VMEM((HALO + tm, FFN_TN), F32)]
    if st is None:
        scratch.append(pltpu.VMEM((nj, HALO, FFN_TN), F32))
        fcv_rows = n_tiles
    else:
        in_specs.append(pl.BlockSpec((seqs_per_tile, HALO, FFN_TN), lambda i, j: (i, 0, j)))
        args.append(st)
        fcv_rows = n_seq
    return pl.pallas_call(
        functools.partial(_ffn_kernel, tm=tm, tiles_per_seq=tiles_per_seq, seq_rows=seq_rows),
        grid=(n_tiles, nj),
        in_specs=in_specs,
        out_specs=[pl.BlockSpec((tm, D_MODEL), lambda i, j: (i0 + i, 0)),
                   pl.BlockSpec((seqs_per_tile, HALO, FFN_TN), lambda i, j: (i, 0, j))],
        out_shape=[jax.ShapeDtypeStruct((m, D_MODEL), F32),
                   jax.ShapeDtypeStruct((fcv_rows, HALO, D_FF), F32)],
        scratch_shapes=scratch,
        input_output_aliases={0: 0},
        compiler_params=_cparams(("arbitrary", "arbitrary")),
        name="ffn_prompt" if st is None else "ffn_sample",
    )(*args)


MERGE_NC = 512


def _merge_kernel(x_ref, g0_ref, g1_ref, g2_ref, o0_ref, o1_ref, o2_ref, wb_ref, wo_ref, out_ref, mg_ref):
    branches = ((g0_ref, o0_ref), (g1_ref, o1_ref), (g2_ref, o2_ref))
    for nc in range(D_MODEL // MERGE_NC):
        cs = slice(nc * MERGE_NC, (nc + 1) * MERGE_NC)
        acc = None
        for j, (g_ref, o_ref) in enumerate(branches):
            t = _sigmoid(g_ref[:, cs]) * _dot(o_ref[...].reshape(-1, MIX_W), wb_ref[j, :, cs])
            acc = t if acc is None else acc + t
        mg_ref[:, cs] = acc.astype(BF16)
    for nc in range(D_MODEL // MERGE_NC):
        cs = slice(nc * MERGE_NC, (nc + 1) * MERGE_NC)
        out_ref[:, cs] = x_ref[:, cs] + _dot(mg_ref[...], wo_ref[:, cs])


def _merge(seqs, mp, ms, x, proj, o_r, o_m, o_t, wb, wo):
    m = x.shape[0]
    tm = _pick(np.gcd(mp // LANES, ms // LANES), (256, 128, 64))
    resident = dict(pipeline_mode=pl.Buffered(1))

    def mixer_block(i):
        lane, blk = seqs.lane_row_block(i, tm, mp, ms)
        return blk, lane, 0, 0

    mixer_spec = pl.BlockSpec((tm // CHUNK, None, CHUNK, MIX_W), mixer_block)
    return pl.pallas_call(
        _merge_kernel,
        grid=(m // tm,),
        in_specs=[pl.BlockSpec((tm, D_MODEL), lambda i: (i, 0)),
                  pl.BlockSpec((tm, GATE_BLK), lambda i: (i, 0)),
                  pl.BlockSpec((tm, GATE_BLK), lambda i: (i, 1)),
                  pl.BlockSpec((tm, GATE_BLK), lambda i: (i, 2)),
                  mixer_spec, mixer_spec, mixer_spec,
                  pl.BlockSpec((3, MIX_W, D_MODEL), lambda i: (0, 0, 0), **resident),
                  pl.BlockSpec((D_MODEL, D_MODEL), lambda i: (0, 0), **resident)],
        out_specs=pl.BlockSpec((tm, D_MODEL), lambda i: (i, 0)),
        out_shape=jax.ShapeDtypeStruct((m, D_MODEL), F32),
        scratch_shapes=[pltpu.VMEM((tm, D_MODEL), BF16)],
        compiler_params=_cparams(("parallel",)),
        name="merge",
    )(x, proj, proj, proj, o_r, o_m, o_t, wb, wo)


def _ple_kernel(x_ref, g_ref, wg_ref, p_ref, wp_ref, gf_ref, *o_refs, prompt_tiles):
    x = x_ref[...]
    gate = _sigmoid(_dot(_rms(x, g_ref[...]).astype(BF16), wg_ref[...]))
    y = x + _dot(p_ref[...].astype(BF16), wp_ref[...]) * gate
    if prompt_tiles is None:
        o_refs[0][...] = y
        return
    y = _rms(y, gf_ref[...])
    i = pl.program_id(0)

    @pl.when(i < prompt_tiles)
    def _():
        o_refs[0][...] = y

    @pl.when(i >= prompt_tiles)
    def _():
        o_refs[1][...] = y


def _ple(x, g, wg, p, wp, gf, split_rows=None):
    m = x.shape[0]
    tm = _pick(m if split_rows is None else np.gcd(*split_rows), (512, 256, 128, 64))
    resident = dict(pipeline_mode=pl.Buffered(1))
    if split_rows is None:
        prompt_tiles = None
        out_specs = pl.BlockSpec((tm, D_MODEL), lambda i: (i, 0))
        out_shape = jax.ShapeDtypeStruct((m, D_MODEL), F32)
    else:
        prompt_tiles = split_rows[0] // tm
        out_specs = [pl.BlockSpec((tm, D_MODEL), lambda i: (jnp.minimum(i, prompt_tiles - 1), 0)),
                     pl.BlockSpec((tm, D_MODEL), lambda i: (jnp.maximum(i - prompt_tiles, 0), 0))]
        out_shape = [jax.ShapeDtypeStruct((r, D_MODEL), F32) for r in split_rows]
    return pl.pallas_call(
        functools.partial(_ple_kernel, prompt_tiles=prompt_tiles),
        grid=(m // tm,),
        in_specs=[pl.BlockSpec((tm, D_MODEL), lambda i: (i, 0)),
                  pl.BlockSpec((1, D_MODEL), lambda i: (0, 0)),
                  pl.BlockSpec((D_MODEL, D_MODEL), lambda i: (0, 0), **resident),
                  pl.BlockSpec((tm, PLE_DIM), lambda i: (i, 0)),
                  pl.BlockSpec((PLE_DIM, D_MODEL), lambda i: (0, 0), **resident),
                  pl.BlockSpec((1, D_MODEL), lambda i: (0, 0))],
        out_specs=out_specs,
        out_shape=out_shape,
        compiler_params=_cparams(("arbitrary",)),
        name="ple" if split_rows is None else "ple_final",
    )(x, g, wg, p, wp, gf)


LANES = 2


class _Seqs:
    def __init__(self, bp, tp, bs, ts):
        assert tp % CHUNK == 0 and ts % CHUNK == 0 and bp % LANES == 0 and bs % LANES == 0
        self.cp, self.cs = tp // CHUNK, ts // CHUNK
        self.bp, self.bs = bp, bs
        self.npc = bp * self.cp
        self.lane_p = self.npc // LANES
        self.lane_s = bs * self.cs // LANES
        self.n_steps = self.lane_p + self.lane_s
        self.lane_seqs = (bp + bs) // LANES

    def _split(self, s):
        in_p = s < self.lane_p
        ss = s - self.lane_p
        pos = jnp.where(in_p, s % self.cp, ss % self.cs)
        return in_p, ss, pos

    def chunk(self, lane, s):
        in_p, ss, _ = self._split(s)
        return jnp.where(in_p, lane * self.lane_p + s, self.npc + lane * self.lane_s + ss)

    def seq(self, lane, s):
        in_p, ss, _ = self._split(s)
        return jnp.where(in_p, lane * (self.bp // LANES) + s // self.cp,
                         self.bp + lane * (self.bs // LANES) + ss // self.cs)

    def lane_seq(self, s):
        in_p, ss, _ = self._split(s)
        return jnp.where(in_p, s // self.cp, self.bp // LANES + ss // self.cs)

    def first(self, s):
        return self._split(s)[2] == 0

    def last(self, s):
        in_p, _, pos = self._split(s)
        return pos == jnp.where(in_p, self.cp - 1, self.cs - 1)

    def rope_block(self, s):
        in_p, _, pos = self._split(s)
        return jnp.where(in_p, pos, self.cp + pos)

    def of_chunk(self, c):
        in_p = c < self.npc
        cc = c - self.npc
        return (jnp.where(in_p, c // self.lane_p, cc // self.lane_s),
                jnp.where(in_p, c % self.lane_p, self.lane_p + cc % self.lane_s))

    def unlane(self, per_lane):
        hp = self.bp // LANES
        return jnp.concatenate([a[:hp] for a in per_lane] + [a[hp:] for a in per_lane], axis=0)

    def lane_row_block(self, i, tm, mp, ms):
        r0 = i * tm
        in_p = r0 < mp
        rs = r0 - mp
        lane = jnp.where(in_p, r0 // (mp // LANES), rs // (ms // LANES))
        local = jnp.where(in_p, r0 % (mp // LANES), mp // LANES + rs % (ms // LANES))
        return lane, local // tm


def _shifted(ext_ref, x, n_prev):
    ext_ref[HALO:HALO + CHUNK] = x
    return [ext_ref[HALO - j:HALO - j + CHUNK] for j in range(1, n_prev + 1)]


def _roll_halo(ext_ref):
    ext_ref[0:HALO] = ext_ref[CHUNK:CHUNK + HALO]


def _bd_mask():
    r = lax.broadcasted_iota(jnp.int32, (GROUP_W, GROUP_W), 0) // RWKV_HEAD
    c = lax.broadcasted_iota(jnp.int32, (GROUP_W, GROUP_W), 1) // RWKV_HEAD
    return r == c


def _rwkv_chunk(al, bt, r, k, v, lg, logd, sbd_ref):
    groups = range(al.shape[1] // GROUP_W)
    bdm = _bd_mask()
    tt = lax.broadcasted_iota(jnp.int32, (CHUNK, GROUP_W), 0)
    ss = lax.broadcasted_iota(jnp.int32, (CHUNK, GROUP_W), 1) % RWKV_HEAD
    low_s, low_i = tt > ss, tt >= ss
    eye = jnp.where(tt == ss, 1.0, 0.0)

    def sl(x, i):
        return x[:, i * GROUP_W:(i + 1) * GROUP_W]

    def bd(x16):
        return jnp.where(bdm, jnp.concatenate([x16] * RWKV_GROUP, axis=0), jnp.zeros((), BF16))

    def stack16(a, b):
        return jnp.concatenate([a, b], axis=0).astype(BF16)

    e_in = jnp.exp(lg)
    e_inv = jnp.exp(-lg)
    at = al * jnp.exp(lg - logd)
    rt = r * e_in
    kh = k * e_inv
    bh = bt * e_inv
    e_end = e_in[CHUNK - 1:CHUNK, :]
    bh_end = (bh * e_end).astype(BF16)
    kh_end = (kh * e_end).astype(BF16)
    v16 = v.astype(BF16)

    lhs = [stack16(sl(at, i), sl(rt, i)) for i in groups]
    x = [_dot_nt(lhs[i], jnp.concatenate([bd(sl(bh, i).astype(BF16)), bd(sl(kh, i).astype(BF16))], axis=0))
         for i in groups]
    p = [jnp.where(low_s, x[i][:CHUNK, :GROUP_W], 0.0) for i in groups]
    q = [jnp.where(low_s, x[i][:CHUNK, GROUP_W:], 0.0) for i in groups]
    rb = [jnp.where(low_i, x[i][CHUNK:, :GROUP_W], 0.0) for i in groups]
    rk = [jnp.where(low_i, x[i][CHUNK:, GROUP_W:], 0.0) for i in groups]
    p16 = [p[i].astype(BF16) for i in groups]
    a = [_dot(p16[i], bd(p16[i])) for i in groups]
    t = [eye + p[i] for i in groups]
    for _ in range(4):
        res = [_dot(stack16(t[i], a[i]), bd(a[i].astype(BF16))) for i in groups]
        t = [t[i] + res[i][:CHUNK] for i in groups]
        a = [res[i][CHUNK:] for i in groups]
    t = [t[i] + _dot(t[i].astype(BF16), bd(a[i].astype(BF16))) for i in groups]
    xm = [_dot_nt(lhs[i], sbd_ref[i].astype(BF16)) for i in groups]
    xv = [_dot(stack16(q[i], rk[i]), bd(sl(v16, i))) for i in groups]
    u16 = [_dot(t[i].astype(BF16), bd((xm[i][:CHUNK] + xv[i][:CHUNK]).astype(BF16))).astype(BF16) for i in groups]
    y = [xm[i][CHUNK:] + xv[i][CHUNK:] + _dot(rb[i].astype(BF16), bd(u16[i])) for i in groups]
    for i in groups:
        upd = _dot_tn(jnp.concatenate([u16[i], sl(v16, i)], axis=0),
                      jnp.concatenate([sl(bh_end, i), sl(kh_end, i)], axis=0))
        sbd_ref[i] = sbd_ref[i] * sl(e_end, i) + jnp.where(bdm, upd, 0.0)
    return jnp.concatenate(y, axis=1)


RWKV_LANE_IN, RWKV_SHARED, RWKV_LANE_OUT, RWKV_LANE_SCRATCH = 5, 12, 3, 2


def _rwkv_kernel(seqs, *refs):
    lane_in = [refs[l * RWKV_LANE_IN:(l + 1) * RWKV_LANE_IN] for l in range(LANES)]
    refs = refs[LANES * RWKV_LANE_IN:]
    (mu_ref, mut_ref, w0_ref, w2_ref, a0_ref, a2_ref, g2_ref, kk_ref, ka_ref, rk_ref, lnw_ref, lnb_ref) = \
        refs[:RWKV_SHARED]
    o_ref = refs[RWKV_SHARED]
    refs = refs[RWKV_SHARED + 1:]
    lane_out = [refs[l * RWKV_LANE_OUT:(l + 1) * RWKV_LANE_OUT] for l in range(LANES)]
    refs = refs[LANES * RWKV_LANE_OUT:]
    lane_scr = [refs[l * RWKV_LANE_SCRATCH:(l + 1) * RWKV_LANE_SCRATCH] for l in range(LANES)]
    sbd_ref = refs[LANES * RWKV_LANE_SCRATCH]
    s = pl.program_id(0)
    n_heads = MIX_W // RWKV_HEAD

    def state_blocks(l):
        for h in range(n_heads):
            g, hh = divmod(h, RWKV_GROUP)
            yield h, l * N_GROUPS + g, slice(hh * RWKV_HEAD, (hh + 1) * RWKV_HEAD)

    @pl.when(seqs.first(s))
    def _():
        sbd_ref[...] = jnp.zeros_like(sbd_ref)
        for l in range(LANES):
            _, _, sh0_ref, sht0_ref, s0_ref = lane_in[l]
            ext_ref, extt_ref = lane_scr[l]
            ext_ref[0:HALO] = sh0_ref[0]
            extt_ref[0:HALO] = sht0_ref[0]
            for h, g, blk in state_blocks(l):
                sbd_ref[g, blk, blk] = s0_ref[0, h]

    ones_bd = jnp.where(_bd_mask(), 1.0, 0.0).astype(BF16)
    ti = lax.broadcasted_iota(jnp.int32, (CHUNK, CHUNK), 0)
    si = lax.broadcasted_iota(jnp.int32, (CHUNK, CHUNK), 1)
    tril16 = jnp.where(ti >= si, 1.0, 0.0).astype(BF16)

    def head_sum(x):
        x16 = x.astype(BF16)
        return jnp.concatenate(
            [_dot(x16[:, i * GROUP_W:(i + 1) * GROUP_W], ones_bd) for i in range(N_GROUPS)], axis=1)

    def prepare(l):
        z_ref, tail_ref = lane_in[l][:2]
        ext_ref, extt_ref = lane_scr[l]
        z = z_ref[...]
        (zp,) = _shifted(ext_ref, z, 1)
        zs = z + mu_ref[...] * (zp - z)
        _roll_halo(ext_ref)
        zt = tail_ref[...]
        (ztp,) = _shifted(extt_ref, zt, 1)
        lora = zt + mut_ref[...] * (ztp - zt)
        _roll_halo(extt_ref)
        r = zs[:, 0:MIX_W]
        k = zs[:, MIX_W:2 * MIX_W]
        v = zs[:, 2 * MIX_W:3 * MIX_W]
        zw = lora[:, ZW_OFF:ZW_OFF + 128]
        za = lora[:, ZA_OFF:ZA_OFF + 128]
        zg = lora[:, ZG_OFF:ZG_OFF + 256]
        logd = -float(np.exp(-0.5)) * _sigmoid(w0_ref[...] + _dot(jnp.tanh(zw).astype(BF16), w2_ref[...]))
        a = _sigmoid(a0_ref[...] + _dot(za.astype(BF16), a2_ref[...]))
        g = _dot(_sigmoid(zg).astype(BF16), g2_ref[...])
        kk = k * kk_ref[...]
        kk = kk * lax.rsqrt(jnp.maximum(head_sum(kk * kk), 1e-24))
        k2 = k * (1.0 + (a - 1.0) * ka_ref[...])
        lg = _split_dot_left(tril16, logd)
        return dict(al=-kk, bt=kk * a, r=r, k=k2, v=v, lg=lg, logd=logd, g=g)

    pre = [prepare(l) for l in range(LANES)]
    both = lambda name: jnp.concatenate([p[name] for p in pre], axis=1)
    y_all = _rwkv_chunk(both("al"), both("bt"), both("r"), both("k"), both("v"), both("lg"), both("logd"), sbd_ref)

    inv_n = 1.0 / RWKV_HEAD
    for l in range(LANES):
        p = pre[l]
        y = y_all[:, l * MIX_W:(l + 1) * MIX_W]
        d = y - head_sum(y) * inv_n
        var = head_sum(d * d) * inv_n
        yn = d * lax.rsqrt(var + RWKV_LN_EPS) * lnw_ref[...] + lnb_ref[...]
        bonus = head_sum(p["r"] * p["k"] * rk_ref[...]) * p["v"]
        o_ref[0, l] = ((yn + bonus) * p["g"]).astype(BF16)

    @pl.when(seqs.last(s))
    def _():
        for l in range(LANES):
            sh_out_ref, sht_out_ref, s_out_ref = lane_out[l]
            ext_ref, extt_ref = lane_scr[l]
            sh_out_ref[0] = ext_ref[0:HALO]
            sht_out_ref[0] = extt_ref[0:HALO]
            for h, g, blk in state_blocks(l):
                s_out_ref[0, h] = sbd_ref[g, blk, blk]


def _mixer_out_spec():
    return pl.BlockSpec((1, LANES, CHUNK, MIX_W), lambda s: (s, 0, 0, 0))


def _mixer_out_shape(seqs):
    return jax.ShapeDtypeStruct((seqs.n_steps, LANES, CHUNK, MIX_W), BF16)


def _lane_specs(make):
    return [spec for lane in range(LANES) for spec in make(lane)]


def _rwkv(seqs, proj, sh0, sht0, s0, pr):
    n_heads = MIX_W // RWKV_HEAD
    row = lambda w: pl.BlockSpec((1, w), lambda s: (0, 0))
    lora_w = lambda k: pl.BlockSpec((k, MIX_W), lambda s: (0, 0))
    tail_blk = IN_COLS_PAD // TAIL_BLK - 1

    def lane_in(l):
        st3 = lambda w: pl.BlockSpec((1, HALO, w), lambda s: (seqs.seq(l, s), 0, 0))
        return [pl.BlockSpec((CHUNK, RKV_BLK), lambda s: (seqs.chunk(l, s), 2)),
                pl.BlockSpec((CHUNK, TAIL_BLK), lambda s: (seqs.chunk(l, s), tail_blk)),
                st3(RKV_BLK), st3(TAIL_BLK),
                pl.BlockSpec((1, n_heads, RWKV_HEAD, RWKV_HEAD), lambda s: (seqs.seq(l, s), 0, 0, 0))]

    def lane_out(l):
        st3 = lambda w: pl.BlockSpec((1, HALO, w), lambda s: (seqs.lane_seq(s), 0, 0))
        return [st3(RKV_BLK), st3(TAIL_BLK),
                pl.BlockSpec((1, n_heads, RWKV_HEAD, RWKV_HEAD), lambda s: (seqs.lane_seq(s), 0, 0, 0))]

    lane_shapes = [jax.ShapeDtypeStruct((seqs.lane_seqs, HALO, RKV_BLK), F32),
                   jax.ShapeDtypeStruct((seqs.lane_seqs, HALO, TAIL_BLK), F32),
                   jax.ShapeDtypeStruct((seqs.lane_seqs, n_heads, RWKV_HEAD, RWKV_HEAD), F32)]
    return pl.pallas_call(
        functools.partial(_rwkv_kernel, seqs),
        grid=(seqs.n_steps,),
        in_specs=_lane_specs(lane_in) + [
            row(RKV_BLK), row(TAIL_BLK), row(MIX_W), lora_w(128), row(MIX_W), lora_w(128), lora_w(256),
            row(MIX_W), row(MIX_W), row(MIX_W), row(MIX_W), row(MIX_W)],
        out_specs=[_mixer_out_spec()] + _lane_specs(lane_out),
        out_shape=[_mixer_out_shape(seqs)] + lane_shapes * LANES,
        scratch_shapes=[sh for _ in range(LANES) for sh in (pltpu.VMEM((HALO + CHUNK, RKV_BLK), F32),
                                                             pltpu.VMEM((HALO + CHUNK, TAIL_BLK), F32))]
        + [pltpu.VMEM((LANES * N_GROUPS, GROUP_W, GROUP_W), F32)],
        compiler_params=_cparams(("arbitrary",)),
        name="rwkv",
    )(*([proj, proj, sh0, sht0, s0] * LANES), pr["mu"], pr["mut"], pr["w0"], pr["w2"], pr["a0"], pr["a2"], pr["g2"],
      pr["kk"], pr["ka"], pr["rk"], pr["lnw"], pr["lnb"])


def _head_norm(x):
    mu = jnp.mean(x, axis=-1, keepdims=True)
    d = x - mu
    return d * lax.rsqrt(jnp.mean(d * d, axis=-1, keepdims=True) + HEAD_NORM_EPS)


def _mlstm_kernel(seqs, q_ref, k_ref, v_ref, op_ref, tail_ref, cv0_ref, c0_ref, n0_ref, m0_ref,
                  cw_ref, gb_ref, nw_ref,
                  o_ref, cv_out_ref, c_ref, n_ref, m_ref,
                  ext_ref):
    _, step = seqs.of_chunk(pl.program_id(0))

    @pl.when(seqs.first(step))
    def _():
        ext_ref[0:HALO] = cv0_ref[0]
        c_ref[...] = c0_ref[...]
        n_ref[...] = n0_ref[...]
        m_ref[...] = m0_ref[...]

    x0 = jnp.concatenate([q_ref[...], k_ref[...]], axis=1)
    x1, x2, x3 = _shifted(ext_ref, x0, MLSTM_CONV - 1)
    qk = x3 * cw_ref[0:1, :] + x2 * cw_ref[1:2, :] + x1 * cw_ref[2:3, :] + x0 * cw_ref[3:4, :]
    _roll_halo(ext_ref)
    qk = qk * _sigmoid(qk)
    q_all = qk[:, :MIX_W]
    k_all = qk[:, MIX_W:] * (M_HEAD ** -0.5)

    gates = tail_ref[:, MGATE_BLK * 128:(MGATE_BLK + 1) * 128] + gb_ref[...]
    lsf = jnp.minimum(gates, 0.0) - jnp.log(1.0 + jnp.exp(-jnp.abs(gates)))
    ti = lax.broadcasted_iota(jnp.int32, (CHUNK, CHUNK), 0)
    si = lax.broadcasted_iota(jnp.int32, (CHUNK, CHUNK), 1)
    tril = ti >= si
    bcum_col = _dot_f32(jnp.where(tril, 1.0, 0.0), lsf)
    ig_rows = gates.T[MGATE_LANE:MGATE_LANE + HALO]
    bcum_rows = _dot_f32(lsf.T[MGATE_LANE:MGATE_LANE + HALO], jnp.where(ti <= si, 1.0, 0.0))
    lane = lax.broadcasted_iota(jnp.int32, (1, 128), 1)
    m_row = m_ref[0]
    heads = range(M_HEADS)
    hsl = [slice(h * M_HEAD, (h + 1) * M_HEAD) for h in heads]

    bc = [bcum_col[:, MGATE_LANE + M_HEADS + h:MGATE_LANE + M_HEADS + h + 1] for h in heads]
    ic = [gates[:, MGATE_LANE + h:MGATE_LANE + h + 1] for h in heads]
    log_inter = [bc[h] + m_row[:, h:h + 1] for h in heads]
    log_intra = [jnp.where(tril, bc[h] - bcum_rows[M_HEADS + h:M_HEADS + h + 1, :] + ig_rows[h:h + 1, :], NEG_BIG)
                 for h in heads]
    m_t = [jnp.maximum(log_inter[h], jnp.max(log_intra[h], axis=-1, keepdims=True)) for h in heads]
    w_inter = [jnp.exp(log_inter[h] - m_t[h]) for h in heads]
    w_intra = [jnp.exp(log_intra[h] - m_t[h]) for h in heads]
    m_end = [m_t[h][CHUNK - 1:CHUNK, :] for h in heads]
    w_end = [jnp.exp(bc[h][CHUNK - 1:CHUNK, :] - bc[h] + ic[h] - m_end[h]) for h in heads]
    g_end = [w_inter[h][CHUNK - 1:CHUNK, :] for h in heads]
    q = [q_all[:, hsl[h]] for h in heads]
    k = [k_all[:, hsl[h]] for h in heads]
    v = [v_ref[:, hsl[h]] for h in heads]
    q16 = [q[h].astype(BF16) for h in heads]
    k16 = [k[h].astype(BF16) for h in heads]
    v16 = [v[h].astype(BF16) for h in heads]
    s = [_dot_nt(q16[h], k16[h]) * w_intra[h] for h in heads]
    qc = [_dot(q16[h], c_ref[0, h].astype(BF16)) for h in heads]
    sv = [_dot(s[h].astype(BF16), v16[h]) for h in heads]
    kv = [_dot_tn(k16[h], (w_end[h] * v[h]).astype(BF16)) for h in heads]
    m_new_row = m_row
    for h in heads:
        n_h = n_ref[0, h:h + 1, :]
        num = w_inter[h] * qc[h] + sv[h]
        den = w_inter[h] * jnp.sum(q[h] * n_h, axis=-1, keepdims=True) + jnp.sum(s[h], axis=-1, keepdims=True)
        hh = num * (1.0 / jnp.maximum(jnp.abs(den), jnp.exp(-m_t[h])))
        c_ref[0, h] = g_end[h] * c_ref[0, h] + kv[h]
        n_ref[0, h:h + 1, :] = g_end[h] * n_h + jnp.sum(w_end[h] * k[h], axis=0, keepdims=True)
        m_new_row = jnp.where(lane == h, m_end[h], m_new_row)
        o_ref[0, 0, :, hsl[h]] = (_head_norm(hh) * nw_ref[:, hsl[h]] * _sigmoid(op_ref[:, hsl[h]])).astype(BF16)

    m_ref[0] = m_new_row

    @pl.when(seqs.last(step))
    def _():
        cv_out_ref[0] = ext_ref[0:HALO]


def _mlstm(seqs, proj, cv0, c0, n0, m0, pr):
    n_seq = seqs.bp + seqs.bs
    blk = lambda j: pl.BlockSpec((CHUNK, MIX_BLK), lambda c: (c, j))
    sq = lambda c: seqs.seq(*seqs.of_chunk(c))
    st_cv = pl.BlockSpec((1, HALO, 2 * MIX_W), lambda c: (sq(c), 0, 0))
    st_c = pl.BlockSpec((1, M_HEADS, M_HEAD, M_HEAD), lambda c: (sq(c), 0, 0, 0))
    st_n = pl.BlockSpec((1, M_HEADS, M_HEAD), lambda c: (sq(c), 0, 0))
    st_m = pl.BlockSpec((1, 1, 128), lambda c: (sq(c), 0, 0))
    return pl.pallas_call(
        functools.partial(_mlstm_kernel, seqs),
        grid=(seqs.npc + seqs.bs * seqs.cs,),
        in_specs=[blk(9), blk(10), blk(11), blk(12),
                  pl.BlockSpec((CHUNK, TAIL_BLK), lambda c: (c, IN_COLS_PAD // TAIL_BLK - 1)),
                  st_cv, st_c, st_n, st_m,
                  pl.BlockSpec((MLSTM_CONV, 2 * MIX_W), lambda c: (0, 0)),
                  pl.BlockSpec((1, 128), lambda c: (0, 0)),
                  pl.BlockSpec((1, MIX_W), lambda c: (0, 0))],
        out_specs=[pl.BlockSpec((1, 1, CHUNK, MIX_W), lambda c: seqs.of_chunk(c)[::-1] + (0, 0)),
                   st_cv, st_c, st_n, st_m],
        out_shape=[_mixer_out_shape(seqs),
                   jax.ShapeDtypeStruct((n_seq, HALO, 2 * MIX_W), F32),
                   jax.ShapeDtypeStruct((n_seq, M_HEADS, M_HEAD, M_HEAD), F32),
                   jax.ShapeDtypeStruct((n_seq, M_HEADS, M_HEAD), F32),
                   jax.ShapeDtypeStruct((n_seq, 1, 128), F32)],
        scratch_shapes=[pltpu.VMEM((HALO + CHUNK, 2 * MIX_W), F32)],
        compiler_params=_cparams(("arbitrary",)),
        name="mlstm",
    )(proj, proj, proj, proj, proj, cv0, c0, n0, m0, pr["cw"], pr["gb"], pr["nw"])


RET_LANE_IN = 5


def _ret_kernel(seqs, *refs):
    lane_in = [refs[l * RET_LANE_IN:(l + 1) * RET_LANE_IN] for l in range(LANES)]
    refs = refs[LANES * RET_LANE_IN:]
    cos_ref, sin_ref, o_ref = refs[:3]
    r_refs = refs[3:]
    s = pl.program_id(0)

    @pl.when(seqs.first(s))
    def _():
        for l in range(LANES):
            r_refs[l][...] = lane_in[l][4][...]

    cos, sin = cos_ref[...], sin_ref[...]
    half = M_HEAD // 2
    ti = lax.broadcasted_iota(jnp.int32, (CHUNK, CHUNK), 0)
    si = lax.broadcasted_iota(jnp.int32, (CHUNK, CHUNK), 1)
    diff = (ti - si).astype(F32)
    t_col = lax.broadcasted_iota(jnp.int32, (CHUNK, 1), 0).astype(F32)

    def rot(u):
        u1, u2 = u[:, :half], u[:, half:]
        return jnp.concatenate([u1 * cos - u2 * sin, u1 * sin + u2 * cos], axis=1)

    items = [(l, h) for l in range(LANES) for h in range(M_HEADS)]
    idx = range(len(items))
    hsl = [slice(h * M_HEAD, (h + 1) * M_HEAD) for _, h in items]
    log_gamma = [float(np.log(1.0 - 2.0 ** (-5.0 - h))) for _, h in items]
    decay_in = {h: jnp.where(diff >= 0, jnp.exp(float(np.log(1.0 - 2.0 ** (-5.0 - h))) * jnp.maximum(diff, 0.0)), 0.0)
                for h in range(M_HEADS)}
    q16 = [rot(lane_in[l][0][:, hsl[i]]).astype(BF16) for i, (l, _) in enumerate(items)]
    k = [rot(lane_in[l][1][:, hsl[i]]) * (M_HEAD ** -0.5) for i, (l, _) in enumerate(items)]
    v16 = [lane_in[l][2][:, hsl[i]].astype(BF16) for i, (l, _) in enumerate(items)]
    sm = [_dot_nt(q16[i], k[i].astype(BF16)) * decay_in[h] for i, (_, h) in enumerate(items)]
    qr = [_dot(q16[i], r_refs[l][0, h].astype(BF16)) for i, (l, h) in enumerate(items)]
    sv = [_dot(sm[i].astype(BF16), v16[i]) for i in idx]
    kv = [_dot_tn((k[i] * jnp.exp(log_gamma[i] * (CHUNK - 1.0 - t_col))).astype(BF16), v16[i]) for i in idx]
    for i, (l, h) in enumerate(items):
        o = sv[i] + jnp.exp(log_gamma[i] * (t_col + 1.0)) * qr[i]
        r_refs[l][0, h] = float(np.exp(log_gamma[i] * CHUNK)) * r_refs[l][0, h] + kv[i]
        gt = lane_in[l][3][:, hsl[i]]
        o_ref[0, l, :, hsl[i]] = (_head_norm(o) * (gt * _sigmoid(gt))).astype(BF16)


def _retention(seqs, proj, cos, sin, r0):
    rope = pl.BlockSpec((CHUNK, M_HEAD // 2), lambda s: (seqs.rope_block(s), 0))
    st_r = lambda index: pl.BlockSpec((1, M_HEADS, M_HEAD, M_HEAD), lambda s: (index(s), 0, 0, 0))

    def lane_in(l):
        blk = lambda j: pl.BlockSpec((CHUNK, MIX_BLK), lambda s: (seqs.chunk(l, s), j))
        return [blk(13), blk(14), blk(15), blk(16), st_r(lambda s: seqs.seq(l, s))]

    return pl.pallas_call(
        functools.partial(_ret_kernel, seqs),
        grid=(seqs.n_steps,),
        in_specs=_lane_specs(lane_in) + [rope, rope],
        out_specs=[_mixer_out_spec()] + [st_r(seqs.lane_seq)] * LANES,
        out_shape=[_mixer_out_shape(seqs)]
        + [jax.ShapeDtypeStruct((seqs.lane_seqs, M_HEADS, M_HEAD, M_HEAD), F32)] * LANES,
        compiler_params=_cparams(("arbitrary",)),
        name="retention",
    )(*([proj] * 4 + [r0]) * LANES, cos, sin)


def _halo_rows(prev, n_prompt):
    k = prev.shape[1]
    return jnp.pad(prev, ((n_prompt, 0), (HALO - k, 0), (0, 0)))


def _with_prompt(state, n_prompt):
    return jnp.pad(state, ((n_prompt, 0),) + ((0, 0),) * (state.ndim - 1))


def _rope_tables(tp, ts):
    half = M_HEAD // 2
    freq = ROPE_BASE ** (-jnp.arange(half, dtype=F32) / half)
    pos = jnp.concatenate([jnp.arange(tp), PAST_LEN + jnp.arange(ts)]).astype(F32)
    ang = pos[:, None] * freq
    return jnp.cos(ang), jnp.sin(ang)


def _tail_layout(lora, gates=None):
    lead = lora.shape[:-1]
    z = lambda n: jnp.zeros(lead + (n,), lora.dtype)
    mid = z(128 - ZA_W) if gates is None else jnp.concatenate(
        [z(MGATE_LANE - ZA_W), gates, z(128 - MGATE_LANE - gates.shape[-1])], axis=-1)
    return jnp.concatenate([lora[..., :ZW_W], z(128 - ZW_W), lora[..., ZW_W:ZW_W + ZA_W], mid,
                            lora[..., ZW_W + ZA_W:], z(256 - ZG_W)], axis=-1)


def _tail_lora(t):
    return jnp.concatenate([t[..., ZW_OFF:ZW_OFF + ZW_W], t[..., ZA_OFF:ZA_OFF + ZA_W], t[..., ZG_OFF:ZG_OFF + ZG_W]],
                           axis=-1)


def _in_col_segments():
    rw = 3 * MIX_W + LORA_W
    mb = rw
    tb = mb + 4 * MIX_W + 2 * M_HEADS
    gb = tb + 4 * MIX_W
    tail = IN_COLS_PAD - TAIL_BLK
    return [(0, gb, 3 * D_MODEL), (3 * D_MODEL, 0, 3 * MIX_W),
            (3 * D_MODEL + 3 * MIX_W, mb, 4 * MIX_W), (3 * D_MODEL + 7 * MIX_W, tb, 4 * MIX_W),
            (tail + ZW_OFF, 3 * MIX_W, ZW_W), (tail + ZA_OFF, 3 * MIX_W + ZW_W, ZA_W),
            (tail + MGATE_BLK * 128 + MGATE_LANE, mb + 4 * MIX_W, 2 * M_HEADS),
            (tail + ZG_OFF, 3 * MIX_W + ZW_W + ZA_W, ZG_W)]


PERMUTE_ROWS = 64
PERMUTE_COLS = 512


def _permute_in_kernel(w_ref, o_ref):
    o_ref[:, IN_COLS_PAD - TAIL_BLK:] = jnp.zeros((PERMUTE_ROWS, TAIL_BLK), BF16)
    for dst, src, width in _in_col_segments():
        for off in range(0, width, PERMUTE_COLS):
            n = min(PERMUTE_COLS, width - off)
            o_ref[:, dst + off:dst + off + n] = w_ref[:, src + off:src + off + n].astype(BF16)


def _permute_in_cols(w):
    depth, d, n = w.shape
    return pl.pallas_call(
        _permute_in_kernel,
        grid=(depth, d // PERMUTE_ROWS),
        in_specs=[pl.BlockSpec((None, PERMUTE_ROWS, n), lambda l, i: (l, i, 0))],
        out_specs=pl.BlockSpec((None, PERMUTE_ROWS, IN_COLS_PAD), lambda l, i: (l, i, 0)),
        out_shape=jax.ShapeDtypeStruct((depth, d, IN_COLS_PAD), BF16),
        compiler_params=_cparams(("parallel", "parallel")),
        name="permute_w_in",
    )(w)


def _lora_rows(w, k):
    return jnp.pad(w, ((0, k - w.shape[0]), (0, 0))).astype(BF16)


def kernel(x_prompt, x_sample, state_rwkv_shift, state_rwkv_wkv, state_mlstm_conv, state_mlstm_c, state_mlstm_n, state_mlstm_m, state_ret, state_ffn_conv, p_prompt, p_sample, norm_mix, w_in, rwkv_mu, rwkv_w0, rwkv_w2, rwkv_a0, rwkv_a2, rwkv_g2, rwkv_kk, rwkv_ka, rwkv_rk, rwkv_lnw, rwkv_lnb, mlstm_conv, mlstm_bi, mlstm_bf, mlstm_nw, w_branch, w_out, norm_ffn, ffn_up, ffn_conv, ffn_down, norm_ple, ple_proj, ple_gate, norm_final):
    bp, tp, _ = x_prompt.shape
    bs, ts, _ = x_sample.shape
    mp, ms = bp * tp, bs * ts
    depth = w_in.shape[0]
    seqs = _Seqs(bp, tp, bs, ts)
    x = jnp.concatenate([x_prompt.reshape(mp, D_MODEL), x_sample.reshape(ms, D_MODEL)], axis=0)
    cos, sin = _rope_tables(tp, ts)
    row = lambda a: a.reshape(1, -1)
    new_states = []
    w_in_p = _permute_in_cols(w_in)
    for i in range(depth):
        proj = _norm_matmul(x, row(norm_mix[i]), w_in_p, i, "in_proj")

        shift = state_rwkv_shift[i][:, None, :]
        mu = rwkv_mu[i]
        rwkv_pr = dict(
            mu=row(mu[:3 * MIX_W]), mut=row(_tail_layout(mu[3 * MIX_W:])),
            w0=row(rwkv_w0[i]), w2=_lora_rows(rwkv_w2[i], 128),
            a0=row(rwkv_a0[i]), a2=_lora_rows(rwkv_a2[i], 128),
            g2=_lora_rows(rwkv_g2[i], 256),
            kk=row(rwkv_kk[i]), ka=row(rwkv_ka[i]), rk=row(rwkv_rk[i]),
            lnw=row(rwkv_lnw[i]), lnb=row(rwkv_lnb[i]))
        o_r, *rwkv_st = _rwkv(
            seqs, proj,
            _halo_rows(shift[:, :, :3 * MIX_W], bp),
            _halo_rows(_tail_layout(shift[:, :, 3 * MIX_W:]), bp),
            _with_prompt(state_rwkv_wkv[i], bp), rwkv_pr)

        gate_bias = jnp.pad(jnp.concatenate([mlstm_bi[i], mlstm_bf[i]]),
                            (MGATE_LANE, 128 - MGATE_LANE - 2 * M_HEADS))
        mlstm_pr = dict(cw=mlstm_conv[i], gb=row(gate_bias), nw=row(mlstm_nw[i]))
        m0 = jnp.pad(state_mlstm_m[i], ((0, 0), (0, 128 - M_HEADS)))[:, None, :]
        o_m, cv_new, c_new, n_new, m_new = _mlstm(
            seqs, proj, _halo_rows(state_mlstm_conv[i], bp), _with_prompt(state_mlstm_c[i], bp),
            _with_prompt(state_mlstm_n[i], bp), _with_prompt(m0, bp), mlstm_pr)

        o_t, *ret_st = _retention(seqs, proj, cos, sin, _with_prompt(state_ret[i], bp))
        sh_new, sht_new, wkv_new = (seqs.unlane(rwkv_st[j::3]) for j in range(3))
        ret_new = seqs.unlane(ret_st)

        x = _merge(seqs, mp, ms, x, proj, o_r, o_m, o_t, w_branch[i].astype(BF16), w_out[i].astype(BF16))

        wa_wb, wd = ffn_up[i].astype(BF16), ffn_down[i].astype(BF16)
        x, fcv_p = _ffn(x, row(norm_ffn[i]), wa_wb, wd, ffn_conv[i], bp, tp, 0)
        x, fcv_s = _ffn(x, row(norm_ffn[i]), wa_wb, wd, ffn_conv[i], bs, ts, mp,
                        st=jnp.pad(state_ffn_conv[i], ((0, 0), (HALO - FFN_CONV + 1, 0), (0, 0))))
        fcv_new = jnp.concatenate([fcv_p.reshape(bp, -1, HALO, D_FF)[:, -1], fcv_s], axis=0)

        p = jnp.concatenate([p_prompt[i].reshape(mp, PLE_DIM), p_sample[i].reshape(ms, PLE_DIM)], axis=0)
        x = _ple(x, row(norm_ple[i]), ple_gate[i].astype(BF16), p, ple_proj[i].astype(BF16),
                 row(norm_final), split_rows=(mp, ms) if i == depth - 1 else None)

        shift_new = jnp.concatenate([sh_new[:, HALO - 1, :], _tail_lora(sht_new[:, HALO - 1, :])], axis=-1)
        new_states.append((shift_new, wkv_new, cv_new[:, HALO - MLSTM_CONV + 1:, :], c_new, n_new,
                           m_new[:, 0, :M_HEADS], ret_new, fcv_new[:, HALO - FFN_CONV + 1:, :]))

    stacked = [jnp.stack(s, axis=0) for s in zip(*new_states)]
    y_prompt = x[0].reshape(bp, tp, D_MODEL)
    y_sample = x[1].reshape(bs, ts, D_MODEL)
    return (y_prompt, y_sample) + tuple(s[:, :bp] for s in stacked) + tuple(s[:, bp:] for s in stacked)
```

```python
import functools

import numpy as np
import jax
import jax.numpy as jnp
from jax import lax
from jax.experimental import pallas as pl
from jax.experimental.pallas import tpu as pltpu

F32 = jnp.float32
BF16 = jnp.bfloat16

D_MODEL = 2048
CHUNK = 64
MIX_W = 1024
RWKV_HEAD = 64
RWKV_GROUP = 2
GROUP_W = RWKV_HEAD * RWKV_GROUP
N_GROUPS = MIX_W // GROUP_W
LORA_W = 64 + 64 + 160
RWKV_LN_EPS = 64e-5
M_HEADS = 4
M_HEAD = 256
MLSTM_CONV = 4
D_FF = 5632
FFN_CONV = 3
PLE_DIM = 256
PAST_LEN = 4096
ROPE_BASE = 10000.0
NORM_EPS = 1e-6
HEAD_NORM_EPS = 1e-5
HALO = 8
NEG_BIG = -1e30

GATE_BLK = 2048
RKV_BLK = 3072
MIX_BLK = 1024
TAIL_BLK = 512
ZW_OFF, ZW_W = 0, 64
ZA_OFF, ZA_W = 128, 64
ZG_OFF, ZG_W = 256, 160
MGATE_BLK, MGATE_LANE = 1, 64
IN_COLS_PAD = 17920
VMEM_LIMIT = 56 * 1024 * 1024


def _cparams(sem):
    return pltpu.CompilerParams(dimension_semantics=sem, vmem_limit_bytes=VMEM_LIMIT)


def _pick(n, prefs):
    for p in prefs:
        if n % p == 0:
            return p
    raise ValueError(f"no tile for {n}")


def _sigmoid(x):
    return jax.nn.sigmoid(x)


def _rms(x, g):
    return x * lax.rsqrt(jnp.mean(x * x, axis=-1, keepdims=True) + NORM_EPS) * g


def _dot(a, b):
    return jnp.dot(a, b, preferred_element_type=F32)


def _dot_nt(a, b):
    return lax.dot_general(a, b, (((1,), (1,)), ((), ())), preferred_element_type=F32)


def _dot_tn(a, b):
    return lax.dot_general(a, b, (((0,), (0,)), ((), ())), preferred_element_type=F32)


def _dot_f32(a, b):
    return jnp.dot(a, b, preferred_element_type=F32, precision=lax.Precision.HIGHEST)


def _split_dot_left(w_bf16, x):
    hi = x.astype(BF16)
    lo = (x - hi.astype(F32)).astype(BF16)
    return _dot(w_bf16, hi) + _dot(w_bf16, lo)


def _norm_mm_kernel(x_ref, g_ref, w_ref, o_ref, xn_ref):
    @pl.when(pl.program_id(1) == 0)
    def _():
        xn_ref[...] = _rms(x_ref[...], g_ref[...]).astype(BF16)

    o_ref[...] = _dot_nt(xn_ref[...], w_ref[...])


def _norm_matmul(x, g, wt, layer, name):
    m, d = x.shape
    n = wt.shape[1]
    tm = _pick(m, (1024, 512, 256, 128, 64))
    tn = _pick(n, (1280, 1024, 512, 256, 128))
    return pl.pallas_call(
        _norm_mm_kernel,
        grid=(m // tm, n // tn),
        in_specs=[pl.BlockSpec((tm, d), lambda i, j: (i, 0)),
                  pl.BlockSpec((1, d), lambda i, j: (0, 0)),
                  pl.BlockSpec((None, tn, d), lambda i, j: (layer, j, 0))],
        out_specs=pl.BlockSpec((tm, tn), lambda i, j: (i, j)),
        out_shape=jax.ShapeDtypeStruct((m, n), F32),
        scratch_shapes=[pltpu.VMEM((tm, d), BF16)],
        compiler_params=_cparams(("parallel", "arbitrary")),
        name=name,
    )(x, g, wt)


FFN_TN = 512
FFN_RB = 256


def _ffn_kernel(x_ref, g_ref, wa_ref, wb_ref, wd_ref, cw_ref, *rest, tm, tiles_per_seq, seq_rows):
    if tiles_per_seq:
        o_ref, fcv_ref, xn_ref, ext_ref, carry_ref = rest
    else:
        st_ref, o_ref, fcv_ref, xn_ref, ext_ref = rest
    i, j = pl.program_id(0), pl.program_id(1)
    rb_rows = min(FFN_RB, tm)

    @pl.when(j == 0)
    def _():
        x = x_ref[...]
        xn_ref[...] = _rms(x, g_ref[...]).astype(BF16)
        o_ref[...] = x

    if tiles_per_seq:
        @pl.when(i % tiles_per_seq == 0)
        def _():
            ext_ref[0:HALO] = jnp.zeros((HALO, FFN_TN), F32)

        @pl.when(i % tiles_per_seq != 0)
        def _():
            ext_ref[0:HALO] = carry_ref[j]

    def up(rb):
        r0 = rb * rb_rows
        xn = xn_ref[r0:r0 + rb_rows]
        x0 = _dot(xn, wa_ref[...])
        ext_ref[HALO + r0:HALO + r0 + rb_rows] = x0
        return x0, _dot(xn, wb_ref[...])

    def gated(rb, x0, b):
        r0 = rb * rb_rows
        x1 = ext_ref[HALO - 1 + r0:HALO - 1 + r0 + rb_rows]
        x2 = ext_ref[HALO - 2 + r0:HALO - 2 + r0 + rb_rows]
        if not tiles_per_seq:
            loc = lax.broadcasted_iota(jnp.int32, (rb_rows, 1), 0) % seq_rows
            seq0 = r0 // seq_rows
            prev = lambda row: jnp.concatenate(
                [jnp.broadcast_to(st_ref[seq0 + q, row:row + 1, :], (seq_rows, FFN_TN))
                 for q in range(rb_rows // seq_rows)], axis=0)
            s1, s2 = prev(HALO - 1), prev(HALO - 2)
            x1 = jnp.where(loc == 0, s1, x1)
            x2 = jnp.where(loc == 0, s2, jnp.where(loc == 1, s1, x2))
        a = x2 * cw_ref[0:1, :] + x1 * cw_ref[1:2, :] + x0 * cw_ref[2:3, :]
        return (0.5 * a * (1.0 + lax.erf(a * float(np.sqrt(0.5)))) * b).astype(BF16)

    n_rb = tm // rb_rows
    pending = up(0)
    for rb in range(n_rb):
        following = up(rb + 1) if rb + 1 < n_rb else None
        act = gated(rb, *pending)
        o_ref[rb * rb_rows:(rb + 1) * rb_rows] += _dot(act, wd_ref[...])
        pending = following

    if tiles_per_seq:
        carry_ref[j] = ext_ref[tm:tm + HALO]
        fcv_ref[0] = ext_ref[tm:tm + HALO]
    else:
        for q in range(tm // seq_rows):
            fcv_ref[q] = ext_ref[(q + 1) * seq_rows:(q + 1) * seq_rows + HALO]


def _ffn(x, g, wa_wb, wd, cw, n_seq, seq_rows, row0, st=None):
    m = x.shape[0]
    nj = D_FF // FFN_TN
    if st is None:
        tm = _pick(seq_rows, (1024, 512, 256, 128, 64))
        tiles_per_seq = seq_rows // tm
        seqs_per_tile = 1
    else:
        tm = _pick(n_seq * seq_rows, (1024, 512, 256, 128, 64))
        assert tm % seq_rows == 0 and min(FFN_RB, tm) % seq_rows == 0
        tiles_per_seq = 0
        seqs_per_tile = tm // seq_rows
    assert row0 % tm == 0
    i0 = row0 // tm
    n_tiles = n_seq * seq_rows // tm
    in_specs = [pl.BlockSpec((tm, D_MODEL), lambda i, j: (i0 + i, 0)),
                pl.BlockSpec((1, D_MODEL), lambda i, j: (0, 0)),
                pl.BlockSpec((D_MODEL, FFN_TN), lambda i, j: (0, j)),
                pl.BlockSpec((D_MODEL, FFN_TN), lambda i, j: (0, nj + j)),
                pl.BlockSpec((FFN_TN, D_MODEL), lambda i, j: (j, 0)),
                pl.BlockSpec((FFN_CONV, FFN_TN), lambda i, j: (0, j))]
    args = [x, g, wa_wb, wa_wb, wd, cw]
    scratch = [pltpu.VMEM((tm, D_MODEL), BF16), pltpu.VMEM((HALO + tm, FFN_TN), F32)]
    if st is None:
        scratch.append(pltpu.VMEM((nj, HALO, FFN_TN), F32))
        fcv_rows = n_tiles
    else:
        in_specs.append(pl.BlockSpec((seqs_per_tile, HALO, FFN_TN), lambda i, j: (i, 0, j)))
        args.append(st)
        fcv_rows = n_seq
    return pl.pallas_call(
        functools.partial(_ffn_kernel, tm=tm, tiles_per_seq=tiles_per_seq, seq_rows=seq_rows),
        grid=(n_tiles, nj),
        in_specs=in_specs,
        out_specs=[pl.BlockSpec((tm, D_MODEL), lambda i, j: (i0 + i, 0)),
                   pl.BlockSpec((seqs_per_tile, HALO, FFN_TN), lambda i, j: (i, 0, j))],
        out_shape=[jax.ShapeDtypeStruct((m, D_MODEL), F32),
                   jax.ShapeDtypeStruct((fcv_rows, HALO, D_FF), F32)],
        scratch_shapes=scratch,
        input_output_aliases={0: 0},
        compiler_params=_cparams(("arbitrary", "arbitrary")),
        name="ffn_prompt" if st is None else "ffn_sample",
    )(*args)


MERGE_NC = 512


def _merge_kernel(x_ref, g0_ref, g1_ref, g2_ref, o0_ref, o1_ref, o2_ref, wb_ref, wo_ref, out_ref, mg_ref):
    branches = ((g0_ref, o0_ref), (g1_ref, o1_ref), (g2_ref, o2_ref))
    for nc in range(D_MODEL // MERGE_NC):
        cs = slice(nc * MERGE_NC, (nc + 1) * MERGE_NC)
        acc = None
        for j, (g_ref, o_ref) in enumerate(branches):
            t = _sigmoid(g_ref[:, cs]) * _dot(o_ref[...].reshape(-1, MIX_W), wb_ref[j, :, cs])
            acc = t if acc is None else acc + t
        mg_ref[:, cs] = acc.astype(BF16)
    for nc in range(D_MODEL // MERGE_NC):
        cs = slice(nc * MERGE_NC, (nc + 1) * MERGE_NC)
        out_ref[:, cs] = x_ref[:, cs] + _dot(mg_ref[...], wo_ref[:, cs])


def _merge(seqs, mp, ms, x, proj, o_r, o_m, o_t, wb, wo):
    m = x.shape[0]
    tm = _pick(np.gcd(mp // LANES, ms // LANES), (256, 128, 64))
    resident = dict(pipeline_mode=pl.Buffered(1))

    def mixer_block(i):
        lane, blk = seqs.lane_row_block(i, tm, mp, ms)
        return blk, lane, 0, 0

    mixer_spec = pl.BlockSpec((tm // CHUNK, None, CHUNK, MIX_W), mixer_block)
    return pl.pallas_call(
        _merge_kernel,
        grid=(m // tm,),
        in_specs=[pl.BlockSpec((tm, D_MODEL), lambda i: (i, 0)),
                  pl.BlockSpec((tm, GATE_BLK), lambda i: (i, 0)),
                  pl.BlockSpec((tm, GATE_BLK), lambda i: (i, 1)),
                  pl.BlockSpec((tm, GATE_BLK), lambda i: (i, 2)),
                  mixer_spec, mixer_spec, mixer_spec,
                  pl.BlockSpec((3, MIX_W, D_MODEL), lambda i: (0, 0, 0), **resident),
                  pl.BlockSpec((D_MODEL, D_MODEL), lambda i: (0, 0), **resident)],
        out_specs=pl.BlockSpec((tm, D_MODEL), lambda i: (i, 0)),
        out_shape=jax.ShapeDtypeStruct((m, D_MODEL), F32),
        scratch_shapes=[pltpu.VMEM((tm, D_MODEL), BF16)],
        compiler_params=_cparams(("parallel",)),
        name="merge",
    )(x, proj, proj, proj, o_r, o_m, o_t, wb, wo)


def _ple_kernel(x_ref, g_ref, wg_ref, p_ref, wp_ref, gf_ref, *o_refs, prompt_tiles):
    x = x_ref[...]
    gate = _sigmoid(_dot(_rms(x, g_ref[...]).astype(BF16), wg_ref[...]))
    y = x + _dot(p_ref[...].astype(BF16), wp_ref[...]) * gate
    if prompt_tiles is None:
        o_refs[0][...] = y
        return
    y = _rms(y, gf_ref[...])
    i = pl.program_id(0)

    @pl.when(i < prompt_tiles)
    def _():
        o_refs[0][...] = y

    @pl.when(i >= prompt_tiles)
    def _():
        o_refs[1][...] = y


def _ple(x, g, wg, p, wp, gf, split_rows=None):
    m = x.shape[0]
    tm = _pick(m if split_rows is None else np.gcd(*split_rows), (512, 256, 128, 64))
    resident = dict(pipeline_mode=pl.Buffered(1))
    if split_rows is None:
        prompt_tiles = None
        out_specs = pl.BlockSpec((tm, D_MODEL), lambda i: (i, 0))
        out_shape = jax.ShapeDtypeStruct((m, D_MODEL), F32)
    else:
        prompt_tiles = split_rows[0] // tm
        out_specs = [pl.BlockSpec((tm, D_MODEL), lambda i: (jnp.minimum(i, prompt_tiles - 1), 0)),
                     pl.BlockSpec((tm, D_MODEL), lambda i: (jnp.maximum(i - prompt_tiles, 0), 0))]
        out_shape = [jax.ShapeDtypeStruct((r, D_MODEL), F32) for r in split_rows]
    return pl.pallas_call(
        functools.partial(_ple_kernel, prompt_tiles=prompt_tiles),
        grid=(m // tm,),
        in_specs=[pl.BlockSpec((tm, D_MODEL), lambda i: (i, 0)),
                  pl.BlockSpec((1, D_MODEL), lambda i: (0, 0)),
                  pl.BlockSpec((D_MODEL, D_MODEL), lambda i: (0, 0), **resident),
                  pl.BlockSpec((tm, PLE_DIM), lambda i: (i, 0)),
                  pl.BlockSpec((PLE_DIM, D_MODEL), lambda i: (0, 0), **resident),
                  pl.BlockSpec((1, D_MODEL), lambda i: (0, 0))],
        out_specs=out_specs,
        out_shape=out_shape,
        compiler_params=_cparams(("arbitrary",)),
        name="ple" if split_rows is None else "ple_final",
    )(x, g, wg, p, wp, gf)


LANES = 2


class _Seqs:
    def __init__(self, bp, tp, bs, ts):
        assert tp % CHUNK == 0 and ts % CHUNK == 0 and bp % LANES == 0 and bs % LANES == 0
        self.cp, self.cs = tp // CHUNK, ts // CHUNK
        self.bp, self.bs = bp, bs
        self.npc = bp * self.cp
        self.lane_p = self.npc // LANES
        self.lane_s = bs * self.cs // LANES
        self.n_steps = self.lane_p + self.lane_s
        self.lane_seqs = (bp + bs) // LANES

    def _split(self, s):
        in_p = s < self.lane_p
        ss = s - self.lane_p
        pos = jnp.where(in_p, s % self.cp, ss % self.cs)
        return in_p, ss, pos

    def chunk(self, lane, s):
        in_p, ss, _ = self._split(s)
        return jnp.where(in_p, lane * self.lane_p + s, self.npc + lane * self.lane_s + ss)

    def seq(self, lane, s):
        in_p, ss, _ = self._split(s)
        return jnp.where(in_p, lane * (self.bp // LANES) + s // self.cp,
                         self.bp + lane * (self.bs // LANES) + ss // self.cs)

    def lane_seq(self, s):
        in_p, ss, _ = self._split(s)
        return jnp.where(in_p, s // self.cp, self.bp // LANES + ss // self.cs)

    def first(self, s):
        return self._split(s)[2] == 0

    def last(self, s):
        in_p, _, pos = self._split(s)
        return pos == jnp.where(in_p, self.cp - 1, self.cs - 1)

    def rope_block(self, s):
        in_p, _, pos = self._split(s)
        return jnp.where(in_p, pos, self.cp + pos)

    def of_chunk(self, c):
        in_p = c < self.npc
        cc = c - self.npc
        return (jnp.where(in_p, c // self.lane_p, cc // self.lane_s),
                jnp.where(in_p, c % self.lane_p, self.lane_p + cc % self.lane_s))

    def unlane(self, per_lane):
        hp = self.bp // LANES
        return jnp.concatenate([a[:hp] for a in per_lane] + [a[hp:] for a in per_lane], axis=0)

    def lane_row_block(self, i, tm, mp, ms):
        r0 = i * tm
        in_p = r0 < mp
        rs = r0 - mp
        lane = jnp.where(in_p, r0 // (mp // LANES), rs // (ms // LANES))
        local = jnp.where(in_p, r0 % (mp // LANES), mp // LANES + rs % (ms // LANES))
        return lane, local // tm


def _shifted(ext_ref, x, n_prev):
    ext_ref[HALO:HALO + CHUNK] = x
    return [ext_ref[HALO - j:HALO - j + CHUNK] for j in range(1, n_prev + 1)]


def _roll_halo(ext_ref):
    ext_ref[0:HALO] = ext_ref[CHUNK:CHUNK + HALO]


def _bd_mask():
    r = lax.broadcasted_iota(jnp.int32, (GROUP_W, GROUP_W), 0) // RWKV_HEAD
    c = lax.broadcasted_iota(jnp.int32, (GROUP_W, GROUP_W), 1) // RWKV_HEAD
    return r == c


def _rwkv_chunk(al, bt, r, k, v, lg, logd, sbd_ref):
    groups = range(al.shape[1] // GROUP_W)
    bdm = _bd_mask()
    tt = lax.broadcasted_iota(jnp.int32, (CHUNK, GROUP_W), 0)
    ss = lax.broadcasted_iota(jnp.int32, (CHUNK, GROUP_W), 1) % RWKV_HEAD
    low_s, low_i = tt > ss, tt >= ss
    eye = jnp.where(tt == ss, 1.0, 0.0)

    def sl(x, i):
        return x[:, i * GROUP_W:(i + 1) * GROUP_W]

    def bd(x16):
        return jnp.where(bdm, jnp.concatenate([x16] * RWKV_GROUP, axis=0), jnp.zeros((), BF16))

    def stack16(a, b):
        return jnp.concatenate([a, b], axis=0).astype(BF16)

    e_in = jnp.exp(lg)
    e_inv = jnp.exp(-lg)
    at = al * jnp.exp(lg - logd)
    rt = r * e_in
    kh = k * e_inv
    bh = bt * e_inv
    e_end = e_in[CHUNK - 1:CHUNK, :]
    bh_end = (bh * e_end).astype(BF16)
    kh_end = (kh * e_end).astype(BF16)
    v16 = v.astype(BF16)

    lhs = [stack16(sl(at, i), sl(rt, i)) for i in groups]
    x = [_dot_nt(lhs[i], jnp.concatenate([bd(sl(bh, i).astype(BF16)), bd(sl(kh, i).astype(BF16))], axis=0))
         for i in groups]
    p = [jnp.where(low_s, x[i][:CHUNK, :GROUP_W], 0.0) for i in groups]
    q = [jnp.where(low_s, x[i][:CHUNK, GROUP_W:], 0.0) for i in groups]
    rb = [jnp.where(low_i, x[i][CHUNK:, :GROUP_W], 0.0) for i in groups]
    rk = [jnp.where(low_i, x[i][CHUNK:, GROUP_W:], 0.0) for i in groups]
    p16 = [p[i].astype(BF16) for i in groups]
    a = [_dot(p16[i], bd(p16[i])) for i in groups]
    t = [eye + p[i] for i in groups]
    for _ in range(4):
        res = [_dot(stack16(t[i], a[i]), bd(a[i].astype(BF16))) for i in groups]
        t = [t[i] + res[i][:CHUNK] for i in groups]
        a = [res[i][CHUNK:] for i in groups]
    t = [t[i] + _dot(t[i].astype(BF16), bd(a[i].astype(BF16))) for i in groups]
    xm = [_dot_nt(lhs[i], sbd_ref[i].astype(BF16)) for i in groups]
    xv = [_dot(stack16(q[i], rk[i]), bd(sl(v16, i))) for i in groups]
    u16 = [_dot(t[i].astype(BF16), bd((xm[i][:CHUNK] + xv[i][:CHUNK]).astype(BF16))).astype(BF16) for i in groups]
    y = [xm[i][CHUNK:] + xv[i][CHUNK:] + _dot(rb[i].astype(BF16), bd(u16[i])) for i in groups]
    for i in groups:
        upd = _dot_tn(jnp.concatenate([u16[i], sl(v16, i)], axis=0),
                      jnp.concatenate([sl(bh_end, i), sl(kh_end, i)], axis=0))
        sbd_ref[i] = sbd_ref[i] * sl(e_end, i) + jnp.where(bdm, upd, 0.0)
    return jnp.concatenate(y, axis=1)


RWKV_LANE_IN, RWKV_SHARED, RWKV_LANE_OUT, RWKV_LANE_SCRATCH = 5, 12, 3, 2


def _rwkv_kernel(seqs, *refs):
    lane_in = [refs[l * RWKV_LANE_IN:(l + 1) * RWKV_LANE_IN] for l in range(LANES)]
    refs = refs[LANES * RWKV_LANE_IN:]
    (mu_ref, mut_ref, w0_ref, w2_ref, a0_ref, a2_ref, g2_ref, kk_ref, ka_ref, rk_ref, lnw_ref, lnb_ref) = \
        refs[:RWKV_SHARED]
    o_ref = refs[RWKV_SHARED]
    refs = refs[RWKV_SHARED + 1:]
    lane_out = [refs[l * RWKV_LANE_OUT:(l + 1) * RWKV_LANE_OUT] for l in range(LANES)]
    refs = refs[LANES * RWKV_LANE_OUT:]
    lane_scr = [refs[l * RWKV_LANE_SCRATCH:(l + 1) * RWKV_LANE_SCRATCH] for l in range(LANES)]
    sbd_ref = refs[LANES * RWKV_LANE_SCRATCH]
    s = pl.program_id(0)
    n_heads = MIX_W // RWKV_HEAD

    def state_blocks(l):
        for h in range(n_heads):
            g, hh = divmod(h, RWKV_GROUP)
            yield h, l * N_GROUPS + g, slice(hh * RWKV_HEAD, (hh + 1) * RWKV_HEAD)

    @pl.when(seqs.first(s))
    def _():
        sbd_ref[...] = jnp.zeros_like(sbd_ref)
        for l in range(LANES):
            _, _, sh0_ref, sht0_ref, s0_ref = lane_in[l]
            ext_ref, extt_ref = lane_scr[l]
            ext_ref[0:HALO] = sh0_ref[0]
            extt_ref[0:HALO] = sht0_ref[0]
            for h, g, blk in state_blocks(l):
                sbd_ref[g, blk, blk] = s0_ref[0, h]

    ones_bd = jnp.where(_bd_mask(), 1.0, 0.0).astype(BF16)
    ti = lax.broadcasted_iota(jnp.int32, (CHUNK, CHUNK), 0)
    si = lax.broadcasted_iota(jnp.int32, (CHUNK, CHUNK), 1)
    tril16 = jnp.where(ti >= si, 1.0, 0.0).astype(BF16)

    def head_sum(x):
        x16 = x.astype(BF16)
        return jnp.concatenate(
            [_dot(x16[:, i * GROUP_W:(i + 1) * GROUP_W], ones_bd) for i in range(N_GROUPS)], axis=1)

    def prepare(l):
        z_ref, tail_ref = lane_in[l][:2]
        ext_ref, extt_ref = lane_scr[l]
        z = z_ref[...]
        (zp,) = _shifted(ext_ref, z, 1)
        zs = z + mu_ref[...] * (zp - z)
        _roll_halo(ext_ref)
        zt = tail_ref[...]
        (ztp,) = _shifted(extt_ref, zt, 1)
        lora = zt + mut_ref[...] * (ztp - zt)
        _roll_halo(extt_ref)
        r = zs[:, 0:MIX_W]
        k = zs[:, MIX_W:2 * MIX_W]
        v = zs[:, 2 * MIX_W:3 * MIX_W]
        zw = lora[:, ZW_OFF:ZW_OFF + 128]
        za = lora[:, ZA_OFF:ZA_OFF + 128]
        zg = lora[:, ZG_OFF:ZG_OFF + 256]
        logd = -float(np.exp(-0.5)) * _sigmoid(w0_ref[...] + _dot(jnp.tanh(zw).astype(BF16), w2_ref[...]))
        a = _sigmoid(a0_ref[...] + _dot(za.astype(BF16), a2_ref[...]))
        g = _dot(_sigmoid(zg).astype(BF16), g2_ref[...])
        kk = k * kk_ref[...]
        kk = kk * lax.rsqrt(jnp.maximum(head_sum(kk * kk), 1e-24))
        k2 = k * (1.0 + (a - 1.0) * ka_ref[...])
        lg = _split_dot_left(tril16, logd)
        return dict(al=-kk, bt=kk * a, r=r, k=k2, v=v, lg=lg, logd=logd, g=g)

    pre = [prepare(l) for l in range(LANES)]
    both = lambda name: jnp.concatenate([p[name] for p in pre], axis=1)
    y_all = _rwkv_chunk(both("al"), both("bt"), both("r"), both("k"), both("v"), both("lg"), both("logd"), sbd_ref)

    inv_n = 1.0 / RWKV_HEAD
    for l in range(LANES):
        p = pre[l]
        y = y_all[:, l * MIX_W:(l + 1) * MIX_W]
        d = y - head_sum(y) * inv_n
        var = head_sum(d * d) * inv_n
        yn = d * lax.rsqrt(var + RWKV_LN_EPS) * lnw_ref[...] + lnb_ref[...]
        bonus = head_sum(p["r"] * p["k"] * rk_ref[...]) * p["v"]
        o_ref[0, l] = ((yn + bonus) * p["g"]).astype(BF16)

    @pl.when(seqs.last(s))
    def _():
        for l in range(LANES):
            sh_out_ref, sht_out_ref, s_out_ref = lane_out[l]
            ext_ref, extt_ref = lane_scr[l]
            sh_out_ref[0] = ext_ref[0:HALO]
            sht_out_ref[0] = extt_ref[0:HALO]
            for h, g, blk in state_blocks(l):
                s_out_ref[0, h] = sbd_ref[g, blk, blk]


def _mixer_out_spec():
    return pl.BlockSpec((1, LANES, CHUNK, MIX_W), lambda s: (s, 0, 0, 0))


def _mixer_out_shape(seqs):
    return jax.ShapeDtypeStruct((seqs.n_steps, LANES, CHUNK, MIX_W), BF16)


def _lane_specs(make):
    return [spec for lane in range(LANES) for spec in make(lane)]


def _rwkv(seqs, proj, sh0, sht0, s0, pr):
    n_heads = MIX_W // RWKV_HEAD
    row = lambda w: pl.BlockSpec((1, w), lambda s: (0, 0))
    lora_w = lambda k: pl.BlockSpec((k, MIX_W), lambda s: (0, 0))
    tail_blk = IN_COLS_PAD // TAIL_BLK - 1

    def lane_in(l):
        st3 = lambda w: pl.BlockSpec((1, HALO, w), lambda s: (seqs.seq(l, s), 0, 0))
        return [pl.BlockSpec((CHUNK, RKV_BLK), lambda s: (seqs.chunk(l, s), 2)),
                pl.BlockSpec((CHUNK, TAIL_BLK), lambda s: (seqs.chunk(l, s), tail_blk)),
                st3(RKV_BLK), st3(TAIL_BLK),
                pl.BlockSpec((1, n_heads, RWKV_HEAD, RWKV_HEAD), lambda s: (seqs.seq(l, s), 0, 0, 0))]

    def lane_out(l):
        st3 = lambda w: pl.BlockSpec((1, HALO, w), lambda s: (seqs.lane_seq(s), 0, 0))
        return [st3(RKV_BLK), st3(TAIL_BLK),
                pl.BlockSpec((1, n_heads, RWKV_HEAD, RWKV_HEAD), lambda s: (seqs.lane_seq(s), 0, 0, 0))]

    lane_shapes = [jax.ShapeDtypeStruct((seqs.lane_seqs, HALO, RKV_BLK), F32),
                   jax.ShapeDtypeStruct((seqs.lane_seqs, HALO, TAIL_BLK), F32),
                   jax.ShapeDtypeStruct((seqs.lane_seqs, n_heads, RWKV_HEAD, RWKV_HEAD), F32)]
    return pl.pallas_call(
        functools.partial(_rwkv_kernel, seqs),
        grid=(seqs.n_steps,),
        in_specs=_lane_specs(lane_in) + [
            row(RKV_BLK), row(TAIL_BLK), row(MIX_W), lora_w(128), row(MIX_W), lora_w(128), lora_w(256),
            row(MIX_W), row(MIX_W), row(MIX_W), row(MIX_W), row(MIX_W)],
        out_specs=[_mixer_out_spec()] + _lane_specs(lane_out),
        out_shape=[_mixer_out_shape(seqs)] + lane_shapes * LANES,
        scratch_shapes=[sh for _ in range(LANES) for sh in (pltpu.VMEM((HALO + CHUNK, RKV_BLK), F32),
                                                             pltpu.VMEM((HALO + CHUNK, TAIL_BLK), F32))]
        + [pltpu.VMEM((LANES * N_GROUPS, GROUP_W, GROUP_W), F32)],
        compiler_params=_cparams(("arbitrary",)),
        name="rwkv",
    )(*([proj, proj, sh0, sht0, s0] * LANES), pr["mu"], pr["mut"], pr["w0"], pr["w2"], pr["a0"], pr["a2"], pr["g2"],
      pr["kk"], pr["ka"], pr["rk"], pr["lnw"], pr["lnb"])


def _head_norm(x):
    mu = jnp.mean(x, axis=-1, keepdims=True)
    d = x - mu
    return d * lax.rsqrt(jnp.mean(d * d, axis=-1, keepdims=True) + HEAD_NORM_EPS)


def _mlstm_kernel(seqs, q_ref, k_ref, v_ref, op_ref, tail_ref, cv0_ref, c0_ref, n0_ref, m0_ref,
                  cw_ref, gb_ref, nw_ref,
                  o_ref, cv_out_ref, c_ref, n_ref, m_ref,
                  ext_ref):
    _, step = seqs.of_chunk(pl.program_id(0))

    @pl.when(seqs.first(step))
    def _():
        ext_ref[0:HALO] = cv0_ref[0]
        c_ref[...] = c0_ref[...]
        n_ref[...] = n0_ref[...]
        m_ref[...] = m0_ref[...]

    x0 = jnp.concatenate([q_ref[...], k_ref[...]], axis=1)
    x1, x2, x3 = _shifted(ext_ref, x0, MLSTM_CONV - 1)
    qk = x3 * cw_ref[0:1, :] + x2 * cw_ref[1:2, :] + x1 * cw_ref[2:3, :] + x0 * cw_ref[3:4, :]
    _roll_halo(ext_ref)
    qk = qk * _sigmoid(qk)
    q_all = qk[:, :MIX_W]
    k_all = qk[:, MIX_W:] * (M_HEAD ** -0.5)

    gates = tail_ref[:, MGATE_BLK * 128:(MGATE_BLK + 1) * 128] + gb_ref[...]
    lsf = jnp.minimum(gates, 0.0) - jnp.log(1.0 + jnp.exp(-jnp.abs(gates)))
    ti = lax.broadcasted_iota(jnp.int32, (CHUNK, CHUNK), 0)
    si = lax.broadcasted_iota(jnp.int32, (CHUNK, CHUNK), 1)
    tril = ti >= si
    bcum_col = _dot_f32(jnp.where(tril, 1.0, 0.0), lsf)
    ig_rows = gates.T[MGATE_LANE:MGATE_LANE + HALO]
    bcum_rows = _dot_f32(lsf.T[MGATE_LANE:MGATE_LANE + HALO], jnp.where(ti <= si, 1.0, 0.0))
    lane = lax.broadcasted_iota(jnp.int32, (1, 128), 1)
    m_row = m_ref[0]
    heads = range(M_HEADS)
    hsl = [slice(h * M_HEAD, (h + 1) * M_HEAD) for h in heads]

    bc = [bcum_col[:, MGATE_LANE + M_HEADS + h:MGATE_LANE + M_HEADS + h + 1] for h in heads]
    ic = [gates[:, MGATE_LANE + h:MGATE_LANE + h + 1] for h in heads]
    log_inter = [bc[h] + m_row[:, h:h + 1] for h in heads]
    log_intra = [jnp.where(tril, bc[h] - bcum_rows[M_HEADS + h:M_HEADS + h + 1, :] + ig_rows[h:h + 1, :], NEG_BIG)
                 for h in heads]
    m_t = [jnp.maximum(log_inter[h], jnp.max(log_intra[h], axis=-1, keepdims=True)) for h in heads]
    w_inter = [jnp.exp(log_inter[h] - m_t[h]) for h in heads]
    w_intra = [jnp.exp(log_intra[h] - m_t[h]) for h in heads]
    m_end = [m_t[h][CHUNK - 1:CHUNK, :] for h in heads]
    w_end = [jnp.exp(bc[h][CHUNK - 1:CHUNK, :] - bc[h] + ic[h] - m_end[h]) for h in heads]
    g_end = [w_inter[h][CHUNK - 1:CHUNK, :] for h in heads]
    q = [q_all[:, hsl[h]] for h in heads]
    k = [k_all[:, hsl[h]] for h in heads]
    v = [v_ref[:, hsl[h]] for h in heads]
    q16 = [q[h].astype(BF16) for h in heads]
    k16 = [k[h].astype(BF16) for h in heads]
    v16 = [v[h].astype(BF16) for h in heads]
    s = [_dot_nt(q16[h], k16[h]) * w_intra[h] for h in heads]
    qc = [_dot(q16[h], c_ref[0, h].astype(BF16)) for h in heads]
    sv = [_dot(s[h].astype(BF16), v16[h]) for h in heads]
    kv = [_dot_tn(k16[h], (w_end[h] * v[h]).astype(BF16)) for h in heads]
    m_new_row = m_row
    for h in heads:
        n_h = n_ref[0, h:h + 1, :]
        num = w_inter[h] * qc[h] + sv[h]
        den = w_inter[h] * jnp.sum(q[h] * n_h, axis=-1, keepdims=True) + jnp.sum(s[h], axis=-1, keepdims=True)
        hh = num * (1.0 / jnp.maximum(jnp.abs(den), jnp.exp(-m_t[h])))
        c_ref[0, h] = g_end[h] * c_ref[0, h] + kv[h]
        n_ref[0, h:h + 1, :] = g_end[h] * n_h + jnp.sum(w_end[h] * k[h], axis=0, keepdims=True)
        m_new_row = jnp.where(lane == h, m_end[h], m_new_row)
        o_ref[0, 0, :, hsl[h]] = (_head_norm(hh) * nw_ref[:, hsl[h]] * _sigmoid(op_ref[:, hsl[h]])).astype(BF16)

    m_ref[0] = m_new_row

    @pl.when(seqs.last(step))
    def _():
        cv_out_ref[0] = ext_ref[0:HALO]


def _mlstm(seqs, proj, cv0, c0, n0, m0, pr):
    n_seq = seqs.bp + seqs.bs
    blk = lambda j: pl.BlockSpec((CHUNK, MIX_BLK), lambda c: (c, j))
    sq = lambda c: seqs.seq(*seqs.of_chunk(c))
    st_cv = pl.BlockSpec((1, HALO, 2 * MIX_W), lambda c: (sq(c), 0, 0))
    st_c = pl.BlockSpec((1, M_HEADS, M_HEAD, M_HEAD), lambda c: (sq(c), 0, 0, 0))
    st_n = pl.BlockSpec((1, M_HEADS, M_HEAD), lambda c: (sq(c), 0, 0))
    st_m = pl.BlockSpec((1, 1, 128), lambda c: (sq(c), 0, 0))
    return pl.pallas_call(
        functools.partial(_mlstm_kernel, seqs),
        grid=(seqs.npc + seqs.bs * seqs.cs,),
        in_specs=[blk(9), blk(10), blk(11), blk(12),
                  pl.BlockSpec((CHUNK, TAIL_BLK), lambda c: (c, IN_COLS_PAD // TAIL_BLK - 1)),
                  st_cv, st_c, st_n, st_m,
                  pl.BlockSpec((MLSTM_CONV, 2 * MIX_W), lambda c: (0, 0)),
                  pl.BlockSpec((1, 128), lambda c: (0, 0)),
                  pl.BlockSpec((1, MIX_W), lambda c: (0, 0))],
        out_specs=[pl.BlockSpec((1, 1, CHUNK, MIX_W), lambda c: seqs.of_chunk(c)[::-1] + (0, 0)),
                   st_cv, st_c, st_n, st_m],
        out_shape=[_mixer_out_shape(seqs),
                   jax.ShapeDtypeStruct((n_seq, HALO, 2 * MIX_W), F32),
                   jax.ShapeDtypeStruct((n_seq, M_HEADS, M_HEAD, M_HEAD), F32),
                   jax.ShapeDtypeStruct((n_seq, M_HEADS, M_HEAD), F32),
                   jax.ShapeDtypeStruct((n_seq, 1, 128), F32)],
        scratch_shapes=[pltpu.VMEM((HALO + CHUNK, 2 * MIX_W), F32)],
        compiler_params=_cparams(("arbitrary",)),
        name="mlstm",
    )(proj, proj, proj, proj, proj, cv0, c0, n0, m0, pr["cw"], pr["gb"], pr["nw"])


RET_LANE_IN = 5


def _ret_kernel(seqs, *refs):
    lane_in = [refs[l * RET_LANE_IN:(l + 1) * RET_LANE_IN] for l in range(LANES)]
    refs = refs[LANES * RET_LANE_IN:]
    cos_ref, sin_ref, o_ref = refs[:3]
    r_refs = refs[3:]
    s = pl.program_id(0)

    @pl.when(seqs.first(s))
    def _():
        for l in range(LANES):
            r_refs[l][...] = lane_in[l][4][...]

    cos, sin = cos_ref[...], sin_ref[...]
    half = M_HEAD // 2
    ti = lax.broadcasted_iota(jnp.int32, (CHUNK, CHUNK), 0)
    si = lax.broadcasted_iota(jnp.int32, (CHUNK, CHUNK), 1)
    diff = (ti - si).astype(F32)
    t_col = lax.broadcasted_iota(jnp.int32, (CHUNK, 1), 0).astype(F32)

    def rot(u):
        u1, u2 = u[:, :half], u[:, half:]
        return jnp.concatenate([u1 * cos - u2 * sin, u1 * sin + u2 * cos], axis=1)

    items = [(l, h) for l in range(LANES) for h in range(M_HEADS)]
    idx = range(len(items))
    hsl = [slice(h * M_HEAD, (h + 1) * M_HEAD) for _, h in items]
    log_gamma = [float(np.log(1.0 - 2.0 ** (-5.0 - h))) for _, h in items]
    decay_in = {h: jnp.where(diff >= 0, jnp.exp(float(np.log(1.0 - 2.0 ** (-5.0 - h))) * jnp.maximum(diff, 0.0)), 0.0)
                for h in range(M_HEADS)}
    q16 = [rot(lane_in[l][0][:, hsl[i]]).astype(BF16) for i, (l, _) in enumerate(items)]
    k = [rot(lane_in[l][1][:, hsl[i]]) * (M_HEAD ** -0.5) for i, (l, _) in enumerate(items)]
    v16 = [lane_in[l][2][:, hsl[i]].astype(BF16) for i, (l, _) in enumerate(items)]
    sm = [_dot_nt(q16[i], k[i].astype(BF16)) * decay_in[h] for i, (_, h) in enumerate(items)]
    qr = [_dot(q16[i], r_refs[l][0, h].astype(BF16)) for i, (l, h) in enumerate(items)]
    sv = [_dot(sm[i].astype(BF16), v16[i]) for i in idx]
    kv = [_dot_tn((k[i] * jnp.exp(log_gamma[i] * (CHUNK - 1.0 - t_col))).astype(BF16), v16[i]) for i in idx]
    for i, (l, h) in enumerate(items):
        o = sv[i] + jnp.exp(log_gamma[i] * (t_col + 1.0)) * qr[i]
        r_refs[l][0, h] = float(np.exp(log_gamma[i] * CHUNK)) * r_refs[l][0, h] + kv[i]
        gt = lane_in[l][3][:, hsl[i]]
        o_ref[0, l, :, hsl[i]] = (_head_norm(o) * (gt * _sigmoid(gt))).astype(BF16)


def _retention(seqs, proj, cos, sin, r0):
    rope = pl.BlockSpec((CHUNK, M_HEAD // 2), lambda s: (seqs.rope_block(s), 0))
    st_r = lambda index: pl.BlockSpec((1, M_HEADS, M_HEAD, M_HEAD), lambda s: (index(s), 0, 0, 0))

    def lane_in(l):
        blk = lambda j: pl.BlockSpec((CHUNK, MIX_BLK), lambda s: (seqs.chunk(l, s), j))
        return [blk(13), blk(14), blk(15), blk(16), st_r(lambda s: seqs.seq(l, s))]

    return pl.pallas_call(
        functools.partial(_ret_kernel, seqs),
        grid=(seqs.n_steps,),
        in_specs=_lane_specs(lane_in) + [rope, rope],
        out_specs=[_mixer_out_spec()] + [st_r(seqs.lane_seq)] * LANES,
        out_shape=[_mixer_out_shape(seqs)]
        + [jax.ShapeDtypeStruct((seqs.lane_seqs, M_HEADS, M_HEAD, M_HEAD), F32)] * LANES,
        compiler_params=_cparams(("arbitrary",)),
        name="retention",
    )(*([proj] * 4 + [r0]) * LANES, cos, sin)


def _halo_rows(prev, n_prompt):
    k = prev.shape[1]
    return jnp.pad(prev, ((n_prompt, 0), (HALO - k, 0), (0, 0)))


def _with_prompt(state, n_prompt):
    return jnp.pad(state, ((n_prompt, 0),) + ((0, 0),) * (state.ndim - 1))


def _rope_tables(tp, ts):
    half = M_HEAD // 2
    freq = ROPE_BASE ** (-jnp.arange(half, dtype=F32) / half)
    pos = jnp.concatenate([jnp.arange(tp), PAST_LEN + jnp.arange(ts)]).astype(F32)
    ang = pos[:, None] * freq
    return jnp.cos(ang), jnp.sin(ang)


def _tail_layout(lora, gates=None):
    lead = lora.shape[:-1]
    z = lambda n: jnp.zeros(lead + (n,), lora.dtype)
    mid = z(128 - ZA_W) if gates is None else jnp.concatenate(
        [z(MGATE_LANE - ZA_W), gates, z(128 - MGATE_LANE - gates.shape[-1])], axis=-1)
    return jnp.concatenate([lora[..., :ZW_W], z(128 - ZW_W), lora[..., ZW_W:ZW_W + ZA_W], mid,
                            lora[..., ZW_W + ZA_W:], z(256 - ZG_W)], axis=-1)


def _tail_lora(t):
    return jnp.concatenate([t[..., ZW_OFF:ZW_OFF + ZW_W], t[..., ZA_OFF:ZA_OFF + ZA_W], t[..., ZG_OFF:ZG_OFF + ZG_W]],
                           axis=-1)


def _in_col_segments():
    rw = 3 * MIX_W + LORA_W
    mb = rw
    tb = mb + 4 * MIX_W + 2 * M_HEADS
    gb = tb + 4 * MIX_W
    tail = IN_COLS_PAD - TAIL_BLK
    return [(0, gb, 3 * D_MODEL), (3 * D_MODEL, 0, 3 * MIX_W),
            (3 * D_MODEL + 3 * MIX_W, mb, 4 * MIX_W), (3 * D_MODEL + 7 * MIX_W, tb, 4 * MIX_W),
            (tail + ZW_OFF, 3 * MIX_W, ZW_W), (tail + ZA_OFF, 3 * MIX_W + ZW_W, ZA_W),
            (tail + MGATE_BLK * 128 + MGATE_LANE, mb + 4 * MIX_W, 2 * M_HEADS),
            (tail + ZG_OFF, 3 * MIX_W + ZW_W + ZA_W, ZG_W)]


PERMUTE_LANES = 128
PERMUTE_RUN = 512


def _permute_in_kernel(w_ref, o_ref):
    o_ref[IN_COLS_PAD - TAIL_BLK:, :] = jnp.zeros((TAIL_BLK, PERMUTE_LANES), BF16)
    for dst, src, width in _in_col_segments():
        for off in range(0, width, PERMUTE_RUN):
            n = min(PERMUTE_RUN, width - off)
            o_ref[dst + off:dst + off + n, :] = w_ref[src + off:src + off + n, :].astype(BF16)


def _permute_in_cols(wt):
    depth, n, d = wt.shape
    return pl.pallas_call(
        _permute_in_kernel,
        grid=(depth, d // PERMUTE_LANES),
        in_specs=[pl.BlockSpec((None, n, PERMUTE_LANES), lambda l, i: (l, 0, i))],
        out_specs=pl.BlockSpec((None, IN_COLS_PAD, PERMUTE_LANES), lambda l, i: (l, 0, i)),
        out_shape=jax.ShapeDtypeStruct((depth, IN_COLS_PAD, d), BF16),
        compiler_params=_cparams(("parallel", "parallel")),
        name="permute_w_in",
    )(wt)


def _lora_rows(w, k):
    return jnp.pad(w, ((0, k - w.shape[0]), (0, 0))).astype(BF16)


def kernel(x_prompt, x_sample, state_rwkv_shift, state_rwkv_wkv, state_mlstm_conv, state_mlstm_c, state_mlstm_n, state_mlstm_m, state_ret, state_ffn_conv, p_prompt, p_sample, norm_mix, w_in, rwkv_mu, rwkv_w0, rwkv_w2, rwkv_a0, rwkv_a2, rwkv_g2, rwkv_kk, rwkv_ka, rwkv_rk, rwkv_lnw, rwkv_lnb, mlstm_conv, mlstm_bi, mlstm_bf, mlstm_nw, w_branch, w_out, norm_ffn, ffn_up, ffn_conv, ffn_down, norm_ple, ple_proj, ple_gate, norm_final):
    bp, tp, _ = x_prompt.shape
    bs, ts, _ = x_sample.shape
    mp, ms = bp * tp, bs * ts
    depth = w_in.shape[0]
    seqs = _Seqs(bp, tp, bs, ts)
    x = jnp.concatenate([x_prompt.reshape(mp, D_MODEL), x_sample.reshape(ms, D_MODEL)], axis=0)
    cos, sin = _rope_tables(tp, ts)
    row = lambda a: a.reshape(1, -1)
    new_states = []
    w_in_p = _permute_in_cols(jnp.swapaxes(w_in, 1, 2))
    for i in range(depth):
        proj = _norm_matmul(x, row(norm_mix[i]), w_in_p, i, "in_proj")

        shift = state_rwkv_shift[i][:, None, :]
        mu = rwkv_mu[i]
        rwkv_pr = dict(
            mu=row(mu[:3 * MIX_W]), mut=row(_tail_layout(mu[3 * MIX_W:])),
            w0=row(rwkv_w0[i]), w2=_lora_rows(rwkv_w2[i], 128),
            a0=row(rwkv_a0[i]), a2=_lora_rows(rwkv_a2[i], 128),
            g2=_lora_rows(rwkv_g2[i], 256),
            kk=row(rwkv_kk[i]), ka=row(rwkv_ka[i]), rk=row(rwkv_rk[i]),
            lnw=row(rwkv_lnw[i]), lnb=row(rwkv_lnb[i]))
        o_r, *rwkv_st = _rwkv(
            seqs, proj,
            _halo_rows(shift[:, :, :3 * MIX_W], bp),
            _halo_rows(_tail_layout(shift[:, :, 3 * MIX_W:]), bp),
            _with_prompt(state_rwkv_wkv[i], bp), rwkv_pr)

        gate_bias = jnp.pad(jnp.concatenate([mlstm_bi[i], mlstm_bf[i]]),
                            (MGATE_LANE, 128 - MGATE_LANE - 2 * M_HEADS))
        mlstm_pr = dict(cw=mlstm_conv[i], gb=row(gate_bias), nw=row(mlstm_nw[i]))
        m0 = jnp.pad(state_mlstm_m[i], ((0, 0), (0, 128 - M_HEADS)))[:, None, :]
        o_m, cv_new, c_new, n_new, m_new = _mlstm(
            seqs, proj, _halo_rows(state_mlstm_conv[i], bp), _with_prompt(state_mlstm_c[i], bp),
            _with_prompt(state_mlstm_n[i], bp), _with_prompt(m0, bp), mlstm_pr)

        o_t, *ret_st = _retention(seqs, proj, cos, sin, _with_prompt(state_ret[i], bp))
        sh_new, sht_new, wkv_new = (seqs.unlane(rwkv_st[j::3]) for j in range(3))
        ret_new = seqs.unlane(ret_st)

        x = _merge(seqs, mp, ms, x, proj, o_r, o_m, o_t, w_branch[i].astype(BF16), w_out[i].astype(BF16))

        wa_wb, wd = ffn_up[i].astype(BF16), ffn_down[i].astype(BF16)
        x, fcv_p = _ffn(x, row(norm_ffn[i]), wa_wb, wd, ffn_conv[i], bp, tp, 0)
        x, fcv_s = _ffn(x, row(norm_ffn[i]), wa_wb, wd, ffn_conv[i], bs, ts, mp,
                        st=jnp.pad(state_ffn_conv[i], ((0, 0), (HALO - FFN_CONV + 1, 0), (0, 0))))
        fcv_new = jnp.concatenate([fcv_p.reshape(bp, -1, HALO, D_FF)[:, -1], fcv_s], axis=0)

        p = jnp.concatenate([p_prompt[i].reshape(mp, PLE_DIM), p_sample[i].reshape(ms, PLE_DIM)], axis=0)
        x = _ple(x, row(norm_ple[i]), ple_gate[i].astype(BF16), p, ple_proj[i].astype(BF16),
                 row(norm_final), split_rows=(mp, ms) if i == depth - 1 else None)

        shift_new = jnp.concatenate([sh_new[:, HALO - 1, :], _tail_lora(sht_new[:, HALO - 1, :])], axis=-1)
        new_states.append((shift_new, wkv_new, cv_new[:, HALO - MLSTM_CONV + 1:, :], c_new, n_new,
                           m_new[:, 0, :M_HEADS], ret_new, fcv_new[:, HALO - FFN_CONV + 1:, :]))

    stacked = [jnp.stack(s, axis=0) for s in zip(*new_states)]
    y_prompt = x[0].reshape(bp, tp, D_MODEL)
    y_sample = x[1].reshape(bs, ts, D_MODEL)
    return (y_prompt, y_sample) + tuple(s[:, :bp] for s in stacked) + tuple(s[:, bp:] for s in stacked)
```

```python
import functools

import numpy as np
import jax
import jax.numpy as jnp
from jax import lax
from jax.experimental import pallas as pl
from jax.experimental.pallas import tpu as pltpu

F32 = jnp.float32
BF16 = jnp.bfloat16

D_MODEL = 2048
CHUNK = 64
MIX_W = 1024
RWKV_HEAD = 64
RWKV_GROUP = 2
GROUP_W = RWKV_HEAD * RWKV_GROUP
N_GROUPS = MIX_W // GROUP_W
LORA_W = 64 + 64 + 160
RWKV_LN_EPS = 64e-5
M_HEADS = 4
M_HEAD = 256
MLSTM_CONV = 4
D_FF = 5632
FFN_CONV = 3
PLE_DIM = 256
PAST_LEN = 4096
ROPE_BASE = 10000.0
NORM_EPS = 1e-6
HEAD_NORM_EPS = 1e-5
HALO = 8
NEG_BIG = -1e30

GATE_BLK = 2048
RKV_BLK = 3072
MIX_BLK = 1024
TAIL_BLK = 512
ZW_OFF, ZW_W = 0, 64
ZA_OFF, ZA_W = 128, 64
ZG_OFF, ZG_W = 256, 160
MGATE_BLK, MGATE_LANE = 1, 64
IN_COLS_PAD = 17920
VMEM_LIMIT = 56 * 1024 * 1024


def _cparams(sem):
    return pltpu.CompilerParams(dimension_semantics=sem, vmem_limit_bytes=VMEM_LIMIT)


def _pick(n, prefs):
    for p in prefs:
        if n % p == 0:
            return p
    raise ValueError(f"no tile for {n}")


def _sigmoid(x):
    return jax.nn.sigmoid(x)


def _rms(x, g):
    return x * lax.rsqrt(jnp.mean(x * x, axis=-1, keepdims=True) + NORM_EPS) * g


def _dot(a, b):
    return jnp.dot(a, b, preferred_element_type=F32)


def _dot_nt(a, b):
    return lax.dot_general(a, b, (((1,), (1,)), ((), ())), preferred_element_type=F32)


def _dot_tn(a, b):
    return lax.dot_general(a, b, (((0,), (0,)), ((), ())), preferred_element_type=F32)


def _dot_f32(a, b):
    return jnp.dot(a, b, preferred_element_type=F32, precision=lax.Precision.HIGHEST)


def _split_dot_left(w_bf16, x):
    hi = x.astype(BF16)
    lo = (x - hi.astype(F32)).astype(BF16)
    return _dot(w_bf16, hi) + _dot(w_bf16, lo)


def _two_source_specs(rows, tm, block_cols, n_grid_axes):
    prompt_tiles = rows[0] // tm
    idx = (lambda f: (lambda i, j: (f(i), 0))) if n_grid_axes == 2 else (lambda f: (lambda i: (f(i), 0)))
    return prompt_tiles, [
        pl.BlockSpec((tm, block_cols), idx(lambda i: jnp.minimum(i, prompt_tiles - 1))),
        pl.BlockSpec((tm, block_cols), idx(lambda i: jnp.maximum(i - prompt_tiles, 0)),
                     **(dict(pipeline_mode=pl.Buffered(1)) if rows[1] == tm else {}))]


def _norm_mm_kernel(*refs, prompt_tiles):
    x_refs, (g_ref, w_ref, o_ref, xn_ref) = refs[:-4], refs[-4:]
    i = pl.program_id(0)
    first_col = pl.program_id(1) == 0
    if prompt_tiles is None:
        sources = [(first_col, x_refs[0])]
    else:
        sources = [(first_col & (i < prompt_tiles), x_refs[0]), (first_col & (i >= prompt_tiles), x_refs[1])]
    for cond, x_ref in sources:
        @pl.when(cond)
        def _(x_ref=x_ref):
            xn_ref[...] = _rms(x_ref[...], g_ref[...]).astype(BF16)

    o_ref[...] = _dot_nt(xn_ref[...], w_ref[...])


def _norm_matmul(xs, g, wt, layer, name):
    d = xs[0].shape[1]
    rows = [x.shape[0] for x in xs]
    m = sum(rows)
    n = wt.shape[1]
    tm = _pick(int(np.gcd.reduce(rows)), (1024, 512, 256, 128, 64))
    tn = _pick(n, (1280, 1024, 512, 256, 128))
    if len(xs) == 1:
        prompt_tiles, x_specs = None, [pl.BlockSpec((tm, d), lambda i, j: (i, 0))]
    else:
        prompt_tiles, x_specs = _two_source_specs(rows, tm, d, 2)
    return pl.pallas_call(
        functools.partial(_norm_mm_kernel, prompt_tiles=prompt_tiles),
        grid=(m // tm, n // tn),
        in_specs=x_specs + [
                  pl.BlockSpec((1, d), lambda i, j: (0, 0)),
                  pl.BlockSpec((None, tn, d), lambda i, j: (layer, j, 0))],
        out_specs=pl.BlockSpec((tm, tn), lambda i, j: (i, j)),
        out_shape=jax.ShapeDtypeStruct((m, n), F32),
        scratch_shapes=[pltpu.VMEM((tm, d), BF16)],
        compiler_params=_cparams(("parallel", "arbitrary")),
        name=name,
    )(*xs, g, wt)


FFN_TN = 512
FFN_RB = 256


def _ffn_kernel(x_ref, g_ref, wa_ref, wb_ref, wd_ref, cw_ref, *rest, tm, tiles_per_seq, seq_rows):
    if tiles_per_seq:
        o_ref, fcv_ref, xn_ref, ext_ref, carry_ref = rest
    else:
        st_ref, o_ref, fcv_ref, xn_ref, ext_ref = rest
    i, j = pl.program_id(0), pl.program_id(1)
    rb_rows = min(FFN_RB, tm)

    @pl.when(j == 0)
    def _():
        x = x_ref[...]
        xn_ref[...] = _rms(x, g_ref[...]).astype(BF16)
        o_ref[...] = x

    if tiles_per_seq:
        @pl.when(i % tiles_per_seq == 0)
        def _():
            ext_ref[0:HALO] = jnp.zeros((HALO, FFN_TN), F32)

        @pl.when(i % tiles_per_seq != 0)
        def _():
            ext_ref[0:HALO] = carry_ref[j]

    def up(rb):
        r0 = rb * rb_rows
        xn = xn_ref[r0:r0 + rb_rows]
        x0 = _dot(xn, wa_ref[...])
        ext_ref[HALO + r0:HALO + r0 + rb_rows] = x0
        return x0, _dot(xn, wb_ref[...])

    def gated(rb, x0, b):
        r0 = rb * rb_rows
        x1 = ext_ref[HALO - 1 + r0:HALO - 1 + r0 + rb_rows]
        x2 = ext_ref[HALO - 2 + r0:HALO - 2 + r0 + rb_rows]
        if not tiles_per_seq:
            loc = lax.broadcasted_iota(jnp.int32, (rb_rows, 1), 0) % seq_rows
            seq0 = r0 // seq_rows
            prev = lambda row: jnp.concatenate(
                [jnp.broadcast_to(st_ref[seq0 + q, row:row + 1, :], (seq_rows, FFN_TN))
                 for q in range(rb_rows // seq_rows)], axis=0)
            s1, s2 = prev(HALO - 1), prev(HALO - 2)
            x1 = jnp.where(loc == 0, s1, x1)
            x2 = jnp.where(loc == 0, s2, jnp.where(loc == 1, s1, x2))
        a = x2 * cw_ref[0:1, :] + x1 * cw_ref[1:2, :] + x0 * cw_ref[2:3, :]
        return (0.5 * a * (1.0 + lax.erf(a * float(np.sqrt(0.5)))) * b).astype(BF16)

    n_rb = tm // rb_rows
    pending = up(0)
    for rb in range(n_rb):
        following = up(rb + 1) if rb + 1 < n_rb else None
        act = gated(rb, *pending)
        o_ref[rb * rb_rows:(rb + 1) * rb_rows] += _dot(act, wd_ref[...])
        pending = following

    if tiles_per_seq:
        carry_ref[j] = ext_ref[tm:tm + HALO]
        fcv_ref[0] = ext_ref[tm:tm + HALO]
    else:
        for q in range(tm // seq_rows):
            fcv_ref[q] = ext_ref[(q + 1) * seq_rows:(q + 1) * seq_rows + HALO]


def _ffn(x, g, wa_wb, wd, layer, cw, n_seq, seq_rows, row0, st=None):
    m = x.shape[0]
    nj = D_FF // FFN_TN
    if st is None:
        tm = _pick(seq_rows, (1024, 512, 256, 128, 64))
        tiles_per_seq = seq_rows // tm
        seqs_per_tile = 1
    else:
        tm = _pick(n_seq * seq_rows, (1024, 512, 256, 128, 64))
        assert tm % seq_rows == 0 and min(FFN_RB, tm) % seq_rows == 0
        tiles_per_seq = 0
        seqs_per_tile = tm // seq_rows
    assert row0 % tm == 0
    i0 = row0 // tm
    n_tiles = n_seq * seq_rows // tm
    in_specs = [pl.BlockSpec((tm, D_MODEL), lambda i, j: (i0 + i, 0)),
                pl.BlockSpec((1, D_MODEL), lambda i, j: (0, 0)),
                pl.BlockSpec((None, D_MODEL, FFN_TN), lambda i, j: (layer, 0, j)),
                pl.BlockSpec((None, D_MODEL, FFN_TN), lambda i, j: (layer, 0, nj + j)),
                pl.BlockSpec((None, FFN_TN, D_MODEL), lambda i, j: (layer, j, 0)),
                pl.BlockSpec((FFN_CONV, FFN_TN), lambda i, j: (0, j))]
    args = [x, g, wa_wb, wa_wb, wd, cw]
    scratch = [pltpu.VMEM((tm, D_MODEL), BF16), pltpu.VMEM((HALO + tm, FFN_TN), F32)]
    if st is None:
        scratch.append(pltpu.VMEM((nj, HALO, FFN_TN), F32))
        fcv_rows = n_tiles
    else:
        in_specs.append(pl.BlockSpec((seqs_per_tile, HALO, FFN_TN), lambda i, j: (i, 0, j)))
        args.append(st)
        fcv_rows = n_seq
    return pl.pallas_call(
        functools.partial(_ffn_kernel, tm=tm, tiles_per_seq=tiles_per_seq, seq_rows=seq_rows),
        grid=(n_tiles, nj),
        in_specs=in_specs,
        out_specs=[pl.BlockSpec((tm, D_MODEL), lambda i, j: (i0 + i, 0)),
                   pl.BlockSpec((seqs_per_tile, HALO, FFN_TN), lambda i, j: (i, 0, j))],
        out_shape=[jax.ShapeDtypeStruct((m, D_MODEL), F32),
                   jax.ShapeDtypeStruct((fcv_rows, HALO, D_FF), F32)],
        scratch_shapes=scratch,
        input_output_aliases={0: 0},
        compiler_params=_cparams(("arbitrary", "arbitrary")),
        name="ffn_prompt" if st is None else "ffn_sample",
    )(*args)


MERGE_NC = 512


def _merge_kernel(*refs, prompt_tiles):
    x_refs = refs[:-10]
    g0_ref, g1_ref, g2_ref, o0_ref, o1_ref, o2_ref, wb_ref, wo_ref, out_ref, mg_ref = refs[-10:]
    i = pl.program_id(0)
    branches = ((g0_ref, o0_ref), (g1_ref, o1_ref), (g2_ref, o2_ref))
    for nc in range(D_MODEL // MERGE_NC):
        cs = slice(nc * MERGE_NC, (nc + 1) * MERGE_NC)
        acc = None
        for j, (g_ref, o_ref) in enumerate(branches):
            t = _sigmoid(g_ref[:, cs]) * _dot(o_ref[...].reshape(-1, MIX_W), wb_ref[j, :, cs])
            acc = t if acc is None else acc + t
        mg_ref[:, cs] = acc.astype(BF16)
    for nc in range(D_MODEL // MERGE_NC):
        cs = slice(nc * MERGE_NC, (nc + 1) * MERGE_NC)
        x = x_refs[0][:, cs]
        if prompt_tiles is not None:
            x = jnp.where(i < prompt_tiles, x, x_refs[1][:, cs])
        out_ref[:, cs] = x + _dot(mg_ref[...], wo_ref[:, cs])


def _merge(seqs, mp, ms, xs, proj, o_r, o_m, o_t, wb, wo, layer):
    m = mp + ms
    tm = _pick(np.gcd(mp // LANES, ms // LANES), (256, 128, 64))
    if len(xs) == 1:
        prompt_tiles, x_specs = None, [pl.BlockSpec((tm, D_MODEL), lambda i: (i, 0))]
    else:
        prompt_tiles, x_specs = _two_source_specs((mp, ms), tm, D_MODEL, 1)
    resident = dict(pipeline_mode=pl.Buffered(1))

    def mixer_block(i):
        lane, blk = seqs.lane_row_block(i, tm, mp, ms)
        return blk, lane, 0, 0

    mixer_spec = pl.BlockSpec((tm // CHUNK, None, CHUNK, MIX_W), mixer_block)
    return pl.pallas_call(
        functools.partial(_merge_kernel, prompt_tiles=prompt_tiles),
        grid=(m // tm,),
        in_specs=x_specs + [
                  pl.BlockSpec((tm, GATE_BLK), lambda i: (i, 0)),
                  pl.BlockSpec((tm, GATE_BLK), lambda i: (i, 1)),
                  pl.BlockSpec((tm, GATE_BLK), lambda i: (i, 2)),
                  mixer_spec, mixer_spec, mixer_spec,
                  pl.BlockSpec((None, 3, MIX_W, D_MODEL), lambda i: (layer, 0, 0, 0), **resident),
                  pl.BlockSpec((None, D_MODEL, D_MODEL), lambda i: (layer, 0, 0), **resident)],
        out_specs=pl.BlockSpec((tm, D_MODEL), lambda i: (i, 0)),
        out_shape=jax.ShapeDtypeStruct((m, D_MODEL), F32),
        scratch_shapes=[pltpu.VMEM((tm, D_MODEL), BF16)],
        compiler_params=_cparams(("parallel",)),
        name="merge",
    )(*xs, proj, proj, proj, o_r, o_m, o_t, wb, wo)


def _ple_kernel(x_ref, g_ref, wg_ref, p_ref, wp_ref, gf_ref, *o_refs, prompt_tiles):
    x = x_ref[...]
    gate = _sigmoid(_dot(_rms(x, g_ref[...]).astype(BF16), wg_ref[...]))
    y = x + _dot(p_ref[...].astype(BF16), wp_ref[...]) * gate
    if prompt_tiles is None:
        o_refs[0][...] = y
        return
    y = _rms(y, gf_ref[...])
    i = pl.program_id(0)

    @pl.when(i < prompt_tiles)
    def _():
        o_refs[0][...] = y

    @pl.when(i >= prompt_tiles)
    def _():
        o_refs[1][...] = y


def _ple(x, g, wg, p, wp, layer, gf, split_rows=None):
    m = x.shape[0]
    tm = _pick(m if split_rows is None else np.gcd(*split_rows), (512, 256, 128, 64))
    resident = dict(pipeline_mode=pl.Buffered(1))
    if split_rows is None:
        prompt_tiles = None
        out_specs = pl.BlockSpec((tm, D_MODEL), lambda i: (i, 0))
        out_shape = jax.ShapeDtypeStruct((m, D_MODEL), F32)
    else:
        prompt_tiles = split_rows[0] // tm
        out_specs = [pl.BlockSpec((tm, D_MODEL), lambda i: (jnp.minimum(i, prompt_tiles - 1), 0)),
                     pl.BlockSpec((tm, D_MODEL), lambda i: (jnp.maximum(i - prompt_tiles, 0), 0))]
        out_shape = [jax.ShapeDtypeStruct((r, D_MODEL), F32) for r in split_rows]
    return pl.pallas_call(
        functools.partial(_ple_kernel, prompt_tiles=prompt_tiles),
        grid=(m // tm,),
        in_specs=[pl.BlockSpec((tm, D_MODEL), lambda i: (i, 0)),
                  pl.BlockSpec((1, D_MODEL), lambda i: (0, 0)),
                  pl.BlockSpec((None, D_MODEL, D_MODEL), lambda i: (layer, 0, 0), **resident),
                  pl.BlockSpec((tm, PLE_DIM), lambda i: (i, 0)),
                  pl.BlockSpec((None, PLE_DIM, D_MODEL), lambda i: (layer, 0, 0), **resident),
                  pl.BlockSpec((1, D_MODEL), lambda i: (0, 0))],
        out_specs=out_specs,
        out_shape=out_shape,
        compiler_params=_cparams(("arbitrary",)),
        name="ple" if split_rows is None else "ple_final",
    )(x, g, wg, p, wp, gf)


LANES = 2


class _Seqs:
    def __init__(self, bp, tp, bs, ts):
        assert tp % CHUNK == 0 and ts % CHUNK == 0 and bp % LANES == 0 and bs % LANES == 0
        self.cp, self.cs = tp // CHUNK, ts // CHUNK
        self.bp, self.bs = bp, bs
        self.npc = bp * self.cp
        self.lane_p = self.npc // LANES
        self.lane_s = bs * self.cs // LANES
        self.n_steps = self.lane_p + self.lane_s
        self.lane_seqs = (bp + bs) // LANES

    def _split(self, s):
        in_p = s < self.lane_p
        ss = s - self.lane_p
        pos = jnp.where(in_p, s % self.cp, ss % self.cs)
        return in_p, ss, pos

    def chunk(self, lane, s):
        in_p, ss, _ = self._split(s)
        return jnp.where(in_p, lane * self.lane_p + s, self.npc + lane * self.lane_s + ss)

    def seq(self, lane, s):
        in_p, ss, _ = self._split(s)
        return jnp.where(in_p, lane * (self.bp // LANES) + s // self.cp,
                         self.bp + lane * (self.bs // LANES) + ss // self.cs)

    def lane_seq(self, s):
        in_p, ss, _ = self._split(s)
        return jnp.where(in_p, s // self.cp, self.bp // LANES + ss // self.cs)

    def first(self, s):
        return self._split(s)[2] == 0

    def last(self, s):
        in_p, _, pos = self._split(s)
        return pos == jnp.where(in_p, self.cp - 1, self.cs - 1)

    def rope_block(self, s):
        in_p, _, pos = self._split(s)
        return jnp.where(in_p, pos, self.cp + pos)

    def of_chunk(self, c):
        in_p = c < self.npc
        cc = c - self.npc
        return (jnp.where(in_p, c // self.lane_p, cc // self.lane_s),
                jnp.where(in_p, c % self.lane_p, self.lane_p + cc % self.lane_s))

    def unlane(self, per_lane):
        hp = self.bp // LANES
        return jnp.concatenate([a[:hp] for a in per_lane] + [a[hp:] for a in per_lane], axis=0)

    def lane_row_block(self, i, tm, mp, ms):
        r0 = i * tm
        in_p = r0 < mp
        rs = r0 - mp
        lane = jnp.where(in_p, r0 // (mp // LANES), rs // (ms // LANES))
        local = jnp.where(in_p, r0 % (mp // LANES), mp // LANES + rs % (ms // LANES))
        return lane, local // tm


def _shifted(ext_ref, x, n_prev):
    ext_ref[HALO:HALO + CHUNK] = x
    return [ext_ref[HALO - j:HALO - j + CHUNK] for j in range(1, n_prev + 1)]


def _roll_halo(ext_ref):
    ext_ref[0:HALO] = ext_ref[CHUNK:CHUNK + HALO]


def _bd_mask():
    r = lax.broadcasted_iota(jnp.int32, (GROUP_W, GROUP_W), 0) // RWKV_HEAD
    c = lax.broadcasted_iota(jnp.int32, (GROUP_W, GROUP_W), 1) // RWKV_HEAD
    return r == c


def _rwkv_chunk(al, bt, r, k, v, lg, logd, sbd_ref):
    groups = range(al.shape[1] // GROUP_W)
    bdm = _bd_mask()
    tt = lax.broadcasted_iota(jnp.int32, (CHUNK, GROUP_W), 0)
    ss = lax.broadcasted_iota(jnp.int32, (CHUNK, GROUP_W), 1) % RWKV_HEAD
    low_s, low_i = tt > ss, tt >= ss
    eye = jnp.where(tt == ss, 1.0, 0.0)

    def sl(x, i):
        return x[:, i * GROUP_W:(i + 1) * GROUP_W]

    def bd(x16):
        return jnp.where(bdm, jnp.concatenate([x16] * RWKV_GROUP, axis=0), jnp.zeros((), BF16))

    def stack16(a, b):
        return jnp.concatenate([a, b], axis=0).astype(BF16)

    e_in = jnp.exp(lg)
    e_inv = jnp.exp(-lg)
    at = al * jnp.exp(lg - logd)
    rt = r * e_in
    kh = k * e_inv
    bh = bt * e_inv
    e_end = e_in[CHUNK - 1:CHUNK, :]
    bh_end = (bh * e_end).astype(BF16)
    kh_end = (kh * e_end).astype(BF16)
    v16 = v.astype(BF16)

    lhs = [stack16(sl(at, i), sl(rt, i)) for i in groups]
    x = [_dot_nt(lhs[i], jnp.concatenate([bd(sl(bh, i).astype(BF16)), bd(sl(kh, i).astype(BF16))], axis=0))
         for i in groups]
    p = [jnp.where(low_s, x[i][:CHUNK, :GROUP_W], 0.0) for i in groups]
    q = [jnp.where(low_s, x[i][:CHUNK, GROUP_W:], 0.0) for i in groups]
    rb = [jnp.where(low_i, x[i][CHUNK:, :GROUP_W], 0.0) for i in groups]
    rk = [jnp.where(low_i, x[i][CHUNK:, GROUP_W:], 0.0) for i in groups]
    p16 = [p[i].astype(BF16) for i in groups]
    a = [_dot(p16[i], bd(p16[i])) for i in groups]
    t = [eye + p[i] for i in groups]
    for _ in range(4):
        res = [_dot(stack16(t[i], a[i]), bd(a[i].astype(BF16))) for i in groups]
        t = [t[i] + res[i][:CHUNK] for i in groups]
        a = [res[i][CHUNK:] for i in groups]
    t = [t[i] + _dot(t[i].astype(BF16), bd(a[i].astype(BF16))) for i in groups]
    xm = [_dot_nt(lhs[i], sbd_ref[i].astype(BF16)) for i in groups]
    xv = [_dot(stack16(q[i], rk[i]), bd(sl(v16, i))) for i in groups]
    u16 = [_dot(t[i].astype(BF16), bd((xm[i][:CHUNK] + xv[i][:CHUNK]).astype(BF16))).astype(BF16) for i in groups]
    y = [xm[i][CHUNK:] + xv[i][CHUNK:] + _dot(rb[i].astype(BF16), bd(u16[i])) for i in groups]
    for i in groups:
        upd = _dot_tn(jnp.concatenate([u16[i], sl(v16, i)], axis=0),
                      jnp.concatenate([sl(bh_end, i), sl(kh_end, i)], axis=0))
        sbd_ref[i] = sbd_ref[i] * sl(e_end, i) + jnp.where(bdm, upd, 0.0)
    return jnp.concatenate(y, axis=1)


RWKV_LANE_IN, RWKV_SHARED, RWKV_LANE_OUT, RWKV_LANE_SCRATCH = 5, 12, 3, 2


def _rwkv_kernel(seqs, *refs):
    lane_in = [refs[l * RWKV_LANE_IN:(l + 1) * RWKV_LANE_IN] for l in range(LANES)]
    refs = refs[LANES * RWKV_LANE_IN:]
    (mu_ref, mut_ref, w0_ref, w2_ref, a0_ref, a2_ref, g2_ref, kk_ref, ka_ref, rk_ref, lnw_ref, lnb_ref) = \
        refs[:RWKV_SHARED]
    o_ref = refs[RWKV_SHARED]
    refs = refs[RWKV_SHARED + 1:]
    lane_out = [refs[l * RWKV_LANE_OUT:(l + 1) * RWKV_LANE_OUT] for l in range(LANES)]
    refs = refs[LANES * RWKV_LANE_OUT:]
    lane_scr = [refs[l * RWKV_LANE_SCRATCH:(l + 1) * RWKV_LANE_SCRATCH] for l in range(LANES)]
    sbd_ref = refs[LANES * RWKV_LANE_SCRATCH]
    s = pl.program_id(0)
    n_heads = MIX_W // RWKV_HEAD

    def state_blocks(l):
        for h in range(n_heads):
            g, hh = divmod(h, RWKV_GROUP)
            yield h, l * N_GROUPS + g, slice(hh * RWKV_HEAD, (hh + 1) * RWKV_HEAD)

    @pl.when(seqs.first(s))
    def _():
        sbd_ref[...] = jnp.zeros_like(sbd_ref)
        for l in range(LANES):
            _, _, sh0_ref, sht0_ref, s0_ref = lane_in[l]
            ext_ref, extt_ref = lane_scr[l]
            ext_ref[0:HALO] = sh0_ref[0]
            extt_ref[0:HALO] = sht0_ref[0]
            for h, g, blk in state_blocks(l):
                sbd_ref[g, blk, blk] = s0_ref[0, h]

    ones_bd = jnp.where(_bd_mask(), 1.0, 0.0).astype(BF16)
    ti = lax.broadcasted_iota(jnp.int32, (CHUNK, CHUNK), 0)
    si = lax.broadcasted_iota(jnp.int32, (CHUNK, CHUNK), 1)
    tril16 = jnp.where(ti >= si, 1.0, 0.0).astype(BF16)

    def head_sum(x):
        x16 = x.astype(BF16)
        return jnp.concatenate(
            [_dot(x16[:, i * GROUP_W:(i + 1) * GROUP_W], ones_bd) for i in range(N_GROUPS)], axis=1)

    def prepare(l):
        z_ref, tail_ref = lane_in[l][:2]
        ext_ref, extt_ref = lane_scr[l]
        z = z_ref[...]
        (zp,) = _shifted(ext_ref, z, 1)
        zs = z + mu_ref[...] * (zp - z)
        _roll_halo(ext_ref)
        zt = tail_ref[...]
        (ztp,) = _shifted(extt_ref, zt, 1)
        lora = zt + mut_ref[...] * (ztp - zt)
        _roll_halo(extt_ref)
        r = zs[:, 0:MIX_W]
        k = zs[:, MIX_W:2 * MIX_W]
        v = zs[:, 2 * MIX_W:3 * MIX_W]
        zw = lora[:, ZW_OFF:ZW_OFF + 128]
        za = lora[:, ZA_OFF:ZA_OFF + 128]
        zg = lora[:, ZG_OFF:ZG_OFF + 256]
        logd = -float(np.exp(-0.5)) * _sigmoid(w0_ref[...] + _dot(jnp.tanh(zw).astype(BF16), w2_ref[...]))
        a = _sigmoid(a0_ref[...] + _dot(za.astype(BF16), a2_ref[...]))
        g = _dot(_sigmoid(zg).astype(BF16), g2_ref[...])
        kk = k * kk_ref[...]
        kk = kk * lax.rsqrt(jnp.maximum(head_sum(kk * kk), 1e-24))
        k2 = k * (1.0 + (a - 1.0) * ka_ref[...])
        lg = _split_dot_left(tril16, logd)
        return dict(al=-kk, bt=kk * a, r=r, k=k2, v=v, lg=lg, logd=logd, g=g)

    pre = [prepare(l) for l in range(LANES)]
    both = lambda name: jnp.concatenate([p[name] for p in pre], axis=1)
    y_all = _rwkv_chunk(both("al"), both("bt"), both("r"), both("k"), both("v"), both("lg"), both("logd"), sbd_ref)

    inv_n = 1.0 / RWKV_HEAD
    for l in range(LANES):
        p = pre[l]
        y = y_all[:, l * MIX_W:(l + 1) * MIX_W]
        d = y - head_sum(y) * inv_n
        var = head_sum(d * d) * inv_n
        yn = d * lax.rsqrt(var + RWKV_LN_EPS) * lnw_ref[...] + lnb_ref[...]
        bonus = head_sum(p["r"] * p["k"] * rk_ref[...]) * p["v"]
        o_ref[0, l] = ((yn + bonus) * p["g"]).astype(BF16)

    @pl.when(seqs.last(s))
    def _():
        for l in range(LANES):
            sh_out_ref, sht_out_ref, s_out_ref = lane_out[l]
            ext_ref, extt_ref = lane_scr[l]
            sh_out_ref[0] = ext_ref[0:HALO]
            sht_out_ref[0] = extt_ref[0:HALO]
            for h, g, blk in state_blocks(l):
                s_out_ref[0, h] = sbd_ref[g, blk, blk]


def _mixer_out_spec():
    return pl.BlockSpec((1, LANES, CHUNK, MIX_W), lambda s: (s, 0, 0, 0))


def _mixer_out_shape(seqs):
    return jax.ShapeDtypeStruct((seqs.n_steps, LANES, CHUNK, MIX_W), BF16)


def _lane_specs(make):
    return [spec for lane in range(LANES) for spec in make(lane)]


def _rwkv(seqs, proj, sh0, sht0, s0, pr):
    n_heads = MIX_W // RWKV_HEAD
    row = lambda w: pl.BlockSpec((1, w), lambda s: (0, 0))
    lora_w = lambda k: pl.BlockSpec((k, MIX_W), lambda s: (0, 0))
    tail_blk = IN_COLS_PAD // TAIL_BLK - 1

    def lane_in(l):
        st3 = lambda w: pl.BlockSpec((1, HALO, w), lambda s: (seqs.seq(l, s), 0, 0))
        return [pl.BlockSpec((CHUNK, RKV_BLK), lambda s: (seqs.chunk(l, s), 2)),
                pl.BlockSpec((CHUNK, TAIL_BLK), lambda s: (seqs.chunk(l, s), tail_blk)),
                st3(RKV_BLK), st3(TAIL_BLK),
                pl.BlockSpec((1, n_heads, RWKV_HEAD, RWKV_HEAD), lambda s: (seqs.seq(l, s), 0, 0, 0))]

    def lane_out(l):
        st3 = lambda w: pl.BlockSpec((1, HALO, w), lambda s: (seqs.lane_seq(s), 0, 0))
        return [st3(RKV_BLK), st3(TAIL_BLK),
                pl.BlockSpec((1, n_heads, RWKV_HEAD, RWKV_HEAD), lambda s: (seqs.lane_seq(s), 0, 0, 0))]

    lane_shapes = [jax.ShapeDtypeStruct((seqs.lane_seqs, HALO, RKV_BLK), F32),
                   jax.ShapeDtypeStruct((seqs.lane_seqs, HALO, TAIL_BLK), F32),
                   jax.ShapeDtypeStruct((seqs.lane_seqs, n_heads, RWKV_HEAD, RWKV_HEAD), F32)]
    return pl.pallas_call(
        functools.partial(_rwkv_kernel, seqs),
        grid=(seqs.n_steps,),
        in_specs=_lane_specs(lane_in) + [
            row(RKV_BLK), row(TAIL_BLK), row(MIX_W), lora_w(128), row(MIX_W), lora_w(128), lora_w(256),
            row(MIX_W), row(MIX_W), row(MIX_W), row(MIX_W), row(MIX_W)],
        out_specs=[_mixer_out_spec()] + _lane_specs(lane_out),
        out_shape=[_mixer_out_shape(seqs)] + lane_shapes * LANES,
        scratch_shapes=[sh for _ in range(LANES) for sh in (pltpu.VMEM((HALO + CHUNK, RKV_BLK), F32),
                                                             pltpu.VMEM((HALO + CHUNK, TAIL_BLK), F32))]
        + [pltpu.VMEM((LANES * N_GROUPS, GROUP_W, GROUP_W), F32)],
        compiler_params=_cparams(("arbitrary",)),
        name="rwkv",
    )(*([proj, proj, sh0, sht0, s0] * LANES), pr["mu"], pr["mut"], pr["w0"], pr["w2"], pr["a0"], pr["a2"], pr["g2"],
      pr["kk"], pr["ka"], pr["rk"], pr["lnw"], pr["lnb"])


def _head_norm(x):
    mu = jnp.mean(x, axis=-1, keepdims=True)
    d = x - mu
    return d * lax.rsqrt(jnp.mean(d * d, axis=-1, keepdims=True) + HEAD_NORM_EPS)


def _mlstm_kernel(seqs, q_ref, k_ref, v_ref, op_ref, tail_ref, cv0_ref, c0_ref, n0_ref, m0_ref,
                  cw_ref, gb_ref, nw_ref,
                  o_ref, cv_out_ref, c_ref, n_ref, m_ref,
                  ext_ref):
    _, step = seqs.of_chunk(pl.program_id(0))

    @pl.when(seqs.first(step))
    def _():
        ext_ref[0:HALO] = cv0_ref[0]
        c_ref[...] = c0_ref[...]
        n_ref[...] = n0_ref[...]
        m_ref[...] = m0_ref[...]

    x0 = jnp.concatenate([q_ref[...], k_ref[...]], axis=1)
    x1, x2, x3 = _shifted(ext_ref, x0, MLSTM_CONV - 1)
    qk = x3 * cw_ref[0:1, :] + x2 * cw_ref[1:2, :] + x1 * cw_ref[2:3, :] + x0 * cw_ref[3:4, :]
    _roll_halo(ext_ref)
    qk = qk * _sigmoid(qk)
    q_all = qk[:, :MIX_W]
    k_all = qk[:, MIX_W:] * (M_HEAD ** -0.5)

    gates = tail_ref[:, MGATE_BLK * 128:(MGATE_BLK + 1) * 128] + gb_ref[...]
    lsf = jnp.minimum(gates, 0.0) - jnp.log(1.0 + jnp.exp(-jnp.abs(gates)))
    ti = lax.broadcasted_iota(jnp.int32, (CHUNK, CHUNK), 0)
    si = lax.broadcasted_iota(jnp.int32, (CHUNK, CHUNK), 1)
    tril = ti >= si
    bcum_col = _dot_f32(jnp.where(tril, 1.0, 0.0), lsf)
    ig_rows = gates.T[MGATE_LANE:MGATE_LANE + HALO]
    bcum_rows = _dot_f32(lsf.T[MGATE_LANE:MGATE_LANE + HALO], jnp.where(ti <= si, 1.0, 0.0))
    lane = lax.broadcasted_iota(jnp.int32, (1, 128), 1)
    m_row = m_ref[0]
    heads = range(M_HEADS)
    hsl = [slice(h * M_HEAD, (h + 1) * M_HEAD) for h in heads]

    bc = [bcum_col[:, MGATE_LANE + M_HEADS + h:MGATE_LANE + M_HEADS + h + 1] for h in heads]
    ic = [gates[:, MGATE_LANE + h:MGATE_LANE + h + 1] for h in heads]
    log_inter = [bc[h] + m_row[:, h:h + 1] for h in heads]
    log_intra = [jnp.where(tril, bc[h] - bcum_rows[M_HEADS + h:M_HEADS + h + 1, :] + ig_rows[h:h + 1, :], NEG_BIG)
                 for h in heads]
    m_t = [jnp.maximum(log_inter[h], jnp.max(log_intra[h], axis=-1, keepdims=True)) for h in heads]
    w_inter = [jnp.exp(log_inter[h] - m_t[h]) for h in heads]
    w_intra = [jnp.exp(log_intra[h] - m_t[h]) for h in heads]
    m_end = [m_t[h][CHUNK - 1:CHUNK, :] for h in heads]
    w_end = [jnp.exp(bc[h][CHUNK - 1:CHUNK, :] - bc[h] + ic[h] - m_end[h]) for h in heads]
    g_end = [w_inter[h][CHUNK - 1:CHUNK, :] for h in heads]
    q = [q_all[:, hsl[h]] for h in heads]
    k = [k_all[:, hsl[h]] for h in heads]
    v = [v_ref[:, hsl[h]] for h in heads]
    q16 = [q[h].astype(BF16) for h in heads]
    k16 = [k[h].astype(BF16) for h in heads]
    v16 = [v[h].astype(BF16) for h in heads]
    s = [_dot_nt(q16[h], k16[h]) * w_intra[h] for h in heads]
    qc = [_dot(q16[h], c_ref[0, h].astype(BF16)) for h in heads]
    sv = [_dot(s[h].astype(BF16), v16[h]) for h in heads]
    kv = [_dot_tn(k16[h], (w_end[h] * v[h]).astype(BF16)) for h in heads]
    m_new_row = m_row
    for h in heads:
        n_h = n_ref[0, h:h + 1, :]
        num = w_inter[h] * qc[h] + sv[h]
        den = w_inter[h] * jnp.sum(q[h] * n_h, axis=-1, keepdims=True) + jnp.sum(s[h], axis=-1, keepdims=True)
        hh = num * (1.0 / jnp.maximum(jnp.abs(den), jnp.exp(-m_t[h])))
        c_ref[0, h] = g_end[h] * c_ref[0, h] + kv[h]
        n_ref[0, h:h + 1, :] = g_end[h] * n_h + jnp.sum(w_end[h] * k[h], axis=0, keepdims=True)
        m_new_row = jnp.where(lane == h, m_end[h], m_new_row)
        o_ref[0, 0, :, hsl[h]] = (_head_norm(hh) * nw_ref[:, hsl[h]] * _sigmoid(op_ref[:, hsl[h]])).astype(BF16)

    m_ref[0] = m_new_row

    @pl.when(seqs.last(step))
    def _():
        cv_out_ref[0] = ext_ref[0:HALO]


def _mlstm(seqs, proj, cv0, c0, n0, m0, pr):
    n_seq = seqs.bp + seqs.bs
    blk = lambda j: pl.BlockSpec((CHUNK, MIX_BLK), lambda c: (c, j))
    sq = lambda c: seqs.seq(*seqs.of_chunk(c))
    st_cv = pl.BlockSpec((1, HALO, 2 * MIX_W), lambda c: (sq(c), 0, 0))
    st_c = pl.BlockSpec((1, M_HEADS, M_HEAD, M_HEAD), lambda c: (sq(c), 0, 0, 0))
    st_n = pl.BlockSpec((1, M_HEADS, M_HEAD), lambda c: (sq(c), 0, 0))
    st_m = pl.BlockSpec((1, 1, 128), lambda c: (sq(c), 0, 0))
    return pl.pallas_call(
        functools.partial(_mlstm_kernel, seqs),
        grid=(seqs.npc + seqs.bs * seqs.cs,),
        in_specs=[blk(9), blk(10), blk(11), blk(12),
                  pl.BlockSpec((CHUNK, TAIL_BLK), lambda c: (c, IN_COLS_PAD // TAIL_BLK - 1)),
                  st_cv, st_c, st_n, st_m,
                  pl.BlockSpec((MLSTM_CONV, 2 * MIX_W), lambda c: (0, 0)),
                  pl.BlockSpec((1, 128), lambda c: (0, 0)),
                  pl.BlockSpec((1, MIX_W), lambda c: (0, 0))],
        out_specs=[pl.BlockSpec((1, 1, CHUNK, MIX_W), lambda c: seqs.of_chunk(c)[::-1] + (0, 0)),
                   st_cv, st_c, st_n, st_m],
        out_shape=[_mixer_out_shape(seqs),
                   jax.ShapeDtypeStruct((n_seq, HALO, 2 * MIX_W), F32),
                   jax.ShapeDtypeStruct((n_seq, M_HEADS, M_HEAD, M_HEAD), F32),
                   jax.ShapeDtypeStruct((n_seq, M_HEADS, M_HEAD), F32),
                   jax.ShapeDtypeStruct((n_seq, 1, 128), F32)],
        scratch_shapes=[pltpu.VMEM((HALO + CHUNK, 2 * MIX_W), F32)],
        compiler_params=_cparams(("arbitrary",)),
        name="mlstm",
    )(proj, proj, proj, proj, proj, cv0, c0, n0, m0, pr["cw"], pr["gb"], pr["nw"])


RET_LANE_IN = 5


def _ret_kernel(seqs, *refs):
    lane_in = [refs[l * RET_LANE_IN:(l + 1) * RET_LANE_IN] for l in range(LANES)]
    refs = refs[LANES * RET_LANE_IN:]
    cos_ref, sin_ref, o_ref = refs[:3]
    r_refs = refs[3:]
    s = pl.program_id(0)

    @pl.when(seqs.first(s))
    def _():
        for l in range(LANES):
            r_refs[l][...] = lane_in[l][4][...]

    cos, sin = cos_ref[...], sin_ref[...]
    half = M_HEAD // 2
    ti = lax.broadcasted_iota(jnp.int32, (CHUNK, CHUNK), 0)
    si = lax.broadcasted_iota(jnp.int32, (CHUNK, CHUNK), 1)
    diff = (ti - si).astype(F32)
    t_col = lax.broadcasted_iota(jnp.int32, (CHUNK, 1), 0).astype(F32)

    def rot(u):
        u1, u2 = u[:, :half], u[:, half:]
        return jnp.concatenate([u1 * cos - u2 * sin, u1 * sin + u2 * cos], axis=1)

    items = [(l, h) for l in range(LANES) for h in range(M_HEADS)]
    idx = range(len(items))
    hsl = [slice(h * M_HEAD, (h + 1) * M_HEAD) for _, h in items]
    log_gamma = [float(np.log(1.0 - 2.0 ** (-5.0 - h))) for _, h in items]
    decay_in = {h: jnp.where(diff >= 0, jnp.exp(float(np.log(1.0 - 2.0 ** (-5.0 - h))) * jnp.maximum(diff, 0.0)), 0.0)
                for h in range(M_HEADS)}
    q16 = [rot(lane_in[l][0][:, hsl[i]]).astype(BF16) for i, (l, _) in enumerate(items)]
    k = [rot(lane_in[l][1][:, hsl[i]]) * (M_HEAD ** -0.5) for i, (l, _) in enumerate(items)]
    v16 = [lane_in[l][2][:, hsl[i]].astype(BF16) for i, (l, _) in enumerate(items)]
    sm = [_dot_nt(q16[i], k[i].astype(BF16)) * decay_in[h] for i, (_, h) in enumerate(items)]
    qr = [_dot(q16[i], r_refs[l][0, h].astype(BF16)) for i, (l, h) in enumerate(items)]
    sv = [_dot(sm[i].astype(BF16), v16[i]) for i in idx]
    kv = [_dot_tn((k[i] * jnp.exp(log_gamma[i] * (CHUNK - 1.0 - t_col))).astype(BF16), v16[i]) for i in idx]
    for i, (l, h) in enumerate(items):
        o = sv[i] + jnp.exp(log_gamma[i] * (t_col + 1.0)) * qr[i]
        r_refs[l][0, h] = float(np.exp(log_gamma[i] * CHUNK)) * r_refs[l][0, h] + kv[i]
        gt = lane_in[l][3][:, hsl[i]]
        o_ref[0, l, :, hsl[i]] = (_head_norm(o) * (gt * _sigmoid(gt))).astype(BF16)


def _retention(seqs, proj, cos, sin, r0):
    rope = pl.BlockSpec((CHUNK, M_HEAD // 2), lambda s: (seqs.rope_block(s), 0))
    st_r = lambda index: pl.BlockSpec((1, M_HEADS, M_HEAD, M_HEAD), lambda s: (index(s), 0, 0, 0))

    def lane_in(l):
        blk = lambda j: pl.BlockSpec((CHUNK, MIX_BLK), lambda s: (seqs.chunk(l, s), j))
        return [blk(13), blk(14), blk(15), blk(16), st_r(lambda s: seqs.seq(l, s))]

    return pl.pallas_call(
        functools.partial(_ret_kernel, seqs),
        grid=(seqs.n_steps,),
        in_specs=_lane_specs(lane_in) + [rope, rope],
        out_specs=[_mixer_out_spec()] + [st_r(seqs.lane_seq)] * LANES,
        out_shape=[_mixer_out_shape(seqs)]
        + [jax.ShapeDtypeStruct((seqs.lane_seqs, M_HEADS, M_HEAD, M_HEAD), F32)] * LANES,
        compiler_params=_cparams(("arbitrary",)),
        name="retention",
    )(*([proj] * 4 + [r0]) * LANES, cos, sin)


def _halo_rows(prev, n_prompt):
    k = prev.shape[1]
    return jnp.pad(prev, ((n_prompt, 0), (HALO - k, 0), (0, 0)))


def _with_prompt(state, n_prompt):
    return jnp.pad(state, ((n_prompt, 0),) + ((0, 0),) * (state.ndim - 1))


def _rope_tables(tp, ts):
    half = M_HEAD // 2
    freq = ROPE_BASE ** (-jnp.arange(half, dtype=F32) / half)
    pos = jnp.concatenate([jnp.arange(tp), PAST_LEN + jnp.arange(ts)]).astype(F32)
    ang = pos[:, None] * freq
    return jnp.cos(ang), jnp.sin(ang)


def _tail_layout(lora, gates=None):
    lead = lora.shape[:-1]
    z = lambda n: jnp.zeros(lead + (n,), lora.dtype)
    mid = z(128 - ZA_W) if gates is None else jnp.concatenate(
        [z(MGATE_LANE - ZA_W), gates, z(128 - MGATE_LANE - gates.shape[-1])], axis=-1)
    return jnp.concatenate([lora[..., :ZW_W], z(128 - ZW_W), lora[..., ZW_W:ZW_W + ZA_W], mid,
                            lora[..., ZW_W + ZA_W:], z(256 - ZG_W)], axis=-1)


def _tail_lora(t):
    return jnp.concatenate([t[..., ZW_OFF:ZW_OFF + ZW_W], t[..., ZA_OFF:ZA_OFF + ZA_W], t[..., ZG_OFF:ZG_OFF + ZG_W]],
                           axis=-1)


def _in_col_segments():
    rw = 3 * MIX_W + LORA_W
    mb = rw
    tb = mb + 4 * MIX_W + 2 * M_HEADS
    gb = tb + 4 * MIX_W
    tail = IN_COLS_PAD - TAIL_BLK
    return [(0, gb, 3 * D_MODEL), (3 * D_MODEL, 0, 3 * MIX_W),
            (3 * D_MODEL + 3 * MIX_W, mb, 4 * MIX_W), (3 * D_MODEL + 7 * MIX_W, tb, 4 * MIX_W),
            (tail + ZW_OFF, 3 * MIX_W, ZW_W), (tail + ZA_OFF, 3 * MIX_W + ZW_W, ZA_W),
            (tail + MGATE_BLK * 128 + MGATE_LANE, mb + 4 * MIX_W, 2 * M_HEADS),
            (tail + ZG_OFF, 3 * MIX_W + ZW_W + ZA_W, ZG_W)]


PERMUTE_LANES = 128
PERMUTE_RUN = 512


def _permute_in_kernel(w_ref, o_ref):
    o_ref[IN_COLS_PAD - TAIL_BLK:, :] = jnp.zeros((TAIL_BLK, PERMUTE_LANES), BF16)
    for dst, src, width in _in_col_segments():
        for off in range(0, width, PERMUTE_RUN):
            n = min(PERMUTE_RUN, width - off)
            o_ref[dst + off:dst + off + n, :] = w_ref[src + off:src + off + n, :].astype(BF16)


def _permute_in_cols(wt):
    depth, n, d = wt.shape
    return pl.pallas_call(
        _permute_in_kernel,
        grid=(depth, d // PERMUTE_LANES),
        in_specs=[pl.BlockSpec((None, n, PERMUTE_LANES), lambda l, i: (l, 0, i))],
        out_specs=pl.BlockSpec((None, IN_COLS_PAD, PERMUTE_LANES), lambda l, i: (l, 0, i)),
        out_shape=jax.ShapeDtypeStruct((depth, IN_COLS_PAD, d), BF16),
        compiler_params=_cparams(("parallel", "parallel")),
        name="permute_w_in",
    )(wt)


def _lora_rows(w, k):
    return jnp.pad(w, ((0, k - w.shape[0]), (0, 0))).astype(BF16)


def kernel(x_prompt, x_sample, state_rwkv_shift, state_rwkv_wkv, state_mlstm_conv, state_mlstm_c, state_mlstm_n, state_mlstm_m, state_ret, state_ffn_conv, p_prompt, p_sample, norm_mix, w_in, rwkv_mu, rwkv_w0, rwkv_w2, rwkv_a0, rwkv_a2, rwkv_g2, rwkv_kk, rwkv_ka, rwkv_rk, rwkv_lnw, rwkv_lnb, mlstm_conv, mlstm_bi, mlstm_bf, mlstm_nw, w_branch, w_out, norm_ffn, ffn_up, ffn_conv, ffn_down, norm_ple, ple_proj, ple_gate, norm_final):
    bp, tp, _ = x_prompt.shape
    bs, ts, _ = x_sample.shape
    mp, ms = bp * tp, bs * ts
    depth = w_in.shape[0]
    seqs = _Seqs(bp, tp, bs, ts)
    xs = (x_prompt.reshape(mp, D_MODEL), x_sample.reshape(ms, D_MODEL))
    cos, sin = _rope_tables(tp, ts)
    row = lambda a: a.reshape(1, -1)
    new_states = []
    w_in_p = _permute_in_cols(jnp.swapaxes(w_in, 1, 2))
    w_branch16, w_out16, ffn_up16, ffn_down16, ple_gate16, ple_proj16 = (
        w.astype(BF16) for w in (w_branch, w_out, ffn_up, ffn_down, ple_gate, ple_proj))
    for i in range(depth):
        proj = _norm_matmul(xs, row(norm_mix[i]), w_in_p, i, "in_proj")

        shift = state_rwkv_shift[i][:, None, :]
        mu = rwkv_mu[i]
        rwkv_pr = dict(
            mu=row(mu[:3 * MIX_W]), mut=row(_tail_layout(mu[3 * MIX_W:])),
            w0=row(rwkv_w0[i]), w2=_lora_rows(rwkv_w2[i], 128),
            a0=row(rwkv_a0[i]), a2=_lora_rows(rwkv_a2[i], 128),
            g2=_lora_rows(rwkv_g2[i], 256),
            kk=row(rwkv_kk[i]), ka=row(rwkv_ka[i]), rk=row(rwkv_rk[i]),
            lnw=row(rwkv_lnw[i]), lnb=row(rwkv_lnb[i]))
        o_r, *rwkv_st = _rwkv(
            seqs, proj,
            _halo_rows(shift[:, :, :3 * MIX_W], bp),
            _halo_rows(_tail_layout(shift[:, :, 3 * MIX_W:]), bp),
            _with_prompt(state_rwkv_wkv[i], bp), rwkv_pr)

        gate_bias = jnp.pad(jnp.concatenate([mlstm_bi[i], mlstm_bf[i]]),
                            (MGATE_LANE, 128 - MGATE_LANE - 2 * M_HEADS))
        mlstm_pr = dict(cw=mlstm_conv[i], gb=row(gate_bias), nw=row(mlstm_nw[i]))
        m0 = jnp.pad(state_mlstm_m[i], ((0, 0), (0, 128 - M_HEADS)))[:, None, :]
        o_m, cv_new, c_new, n_new, m_new = _mlstm(
            seqs, proj, _halo_rows(state_mlstm_conv[i], bp), _with_prompt(state_mlstm_c[i], bp),
            _with_prompt(state_mlstm_n[i], bp), _with_prompt(m0, bp), mlstm_pr)

        o_t, *ret_st = _retention(seqs, proj, cos, sin, _with_prompt(state_ret[i], bp))
        sh_new, sht_new, wkv_new = (seqs.unlane(rwkv_st[j::3]) for j in range(3))
        ret_new = seqs.unlane(ret_st)

        x = _merge(seqs, mp, ms, xs, proj, o_r, o_m, o_t, w_branch16, w_out16, i)

        x, fcv_p = _ffn(x, row(norm_ffn[i]), ffn_up16, ffn_down16, i, ffn_conv[i], bp, tp, 0)
        x, fcv_s = _ffn(x, row(norm_ffn[i]), ffn_up16, ffn_down16, i, ffn_conv[i], bs, ts, mp,
                        st=jnp.pad(state_ffn_conv[i], ((0, 0), (HALO - FFN_CONV + 1, 0), (0, 0))))
        fcv_new = jnp.concatenate([fcv_p.reshape(bp, -1, HALO, D_FF)[:, -1], fcv_s], axis=0)

        p = jnp.concatenate([p_prompt[i].reshape(mp, PLE_DIM), p_sample[i].reshape(ms, PLE_DIM)], axis=0)
        x = _ple(x, row(norm_ple[i]), ple_gate16, p, ple_proj16, i,
                 row(norm_final), split_rows=(mp, ms) if i == depth - 1 else None)
        xs = (x,)

        shift_new = jnp.concatenate([sh_new[:, HALO - 1, :], _tail_lora(sht_new[:, HALO - 1, :])], axis=-1)
        new_states.append((shift_new, wkv_new, cv_new[:, HALO - MLSTM_CONV + 1:, :], c_new, n_new,
                           m_new[:, 0, :M_HEADS], ret_new, fcv_new[:, HALO - FFN_CONV + 1:, :]))

    stacked = [jnp.stack(s, axis=0) for s in zip(*new_states)]
    y_prompt = x[0].reshape(bp, tp, D_MODEL)
    y_sample = x[1].reshape(bs, ts, D_MODEL)
    return (y_prompt, y_sample) + tuple(s[:, :bp] for s in stacked) + tuple(s[:, bp:] for s in stacked)
```

```python
import functools

import numpy as np
import jax
import jax.numpy as jnp
from jax import lax
from jax.experimental import pallas as pl
from jax.experimental.pallas import tpu as pltpu

F32 = jnp.float32
BF16 = jnp.bfloat16

D_MODEL = 2048
CHUNK = 64
MIX_W = 1024
RWKV_HEAD = 64
RWKV_GROUP = 2
GROUP_W = RWKV_HEAD * RWKV_GROUP
N_GROUPS = MIX_W // GROUP_W
LORA_W = 64 + 64 + 160
RWKV_LN_EPS = 64e-5
M_HEADS = 4
M_HEAD = 256
MLSTM_CONV = 4
D_FF = 5632
FFN_CONV = 3
PLE_DIM = 256
PAST_LEN = 4096
ROPE_BASE = 10000.0
NORM_EPS = 1e-6
HEAD_NORM_EPS = 1e-5
LANE = 128
HALO = 8
NEG_BIG = -1e30

GATE_BLK = 2048
RKV_BLK = 3072
MIX_BLK = 1024
TAIL_BLK = 512
ZW_OFF, ZW_W = 0, 64
ZA_OFF, ZA_W = LANE, 64
ZG_OFF, ZG_W = 2 * LANE, 160
MGATE_BLK, MGATE_LANE = 1, 64
IN_COLS_PAD = 17920
VMEM_LIMIT = 56 * 1024 * 1024


def _cparams(sem):
    return pltpu.CompilerParams(dimension_semantics=sem, vmem_limit_bytes=VMEM_LIMIT)


def _pick(n, prefs):
    for p in prefs:
        if n % p == 0:
            return p
    raise ValueError(f"no tile for {n}")


def _sigmoid(x):
    return jax.nn.sigmoid(x)


def _rms(x, g):
    return x * lax.rsqrt(jnp.mean(x * x, axis=-1, keepdims=True) + NORM_EPS) * g


def _dot(a, b):
    return jnp.dot(a, b, preferred_element_type=F32)


def _dot_nt(a, b):
    return lax.dot_general(a, b, (((1,), (1,)), ((), ())), preferred_element_type=F32)


def _dot_tn(a, b):
    return lax.dot_general(a, b, (((0,), (0,)), ((), ())), preferred_element_type=F32)


def _dot_f32(a, b):
    return jnp.dot(a, b, preferred_element_type=F32, precision=lax.Precision.HIGHEST)


def _split_dot_left(w_bf16, x):
    hi = x.astype(BF16)
    lo = (x - hi.astype(F32)).astype(BF16)
    return _dot(w_bf16, hi) + _dot(w_bf16, lo)


def _two_source_specs(rows, tm, block_cols, n_grid_axes):
    prompt_tiles = rows[0] // tm
    idx = (lambda f: (lambda i, j: (f(i), 0))) if n_grid_axes == 2 else (lambda f: (lambda i: (f(i), 0)))
    return prompt_tiles, [
        pl.BlockSpec((tm, block_cols), idx(lambda i: jnp.minimum(i, prompt_tiles - 1))),
        pl.BlockSpec((tm, block_cols), idx(lambda i: jnp.maximum(i - prompt_tiles, 0)),
                     **(dict(pipeline_mode=pl.Buffered(1)) if rows[1] == tm else {}))]


def _norm_mm_kernel(*refs, prompt_tiles):
    x_refs, (g_ref, w_ref, o_ref, xn_ref) = refs[:-4], refs[-4:]
    i = pl.program_id(0)
    first_col = pl.program_id(1) == 0
    if prompt_tiles is None:
        sources = [(first_col, x_refs[0])]
    else:
        sources = [(first_col & (i < prompt_tiles), x_refs[0]), (first_col & (i >= prompt_tiles), x_refs[1])]
    for cond, x_ref in sources:
        @pl.when(cond)
        def _(x_ref=x_ref):
            xn_ref[...] = _rms(x_ref[...], g_ref[...]).astype(BF16)

    o_ref[...] = _dot_nt(xn_ref[...], w_ref[...])


def _norm_matmul(xs, g, wt, layer, name):
    d = xs[0].shape[1]
    rows = [x.shape[0] for x in xs]
    m = sum(rows)
    n = wt.shape[1]
    tm = _pick(int(np.gcd.reduce(rows)), (1024, 512, 256, 128, 64))
    tn = _pick(n, (1280, 1024, 512, 256, 128))
    if len(xs) == 1:
        prompt_tiles, x_specs = None, [pl.BlockSpec((tm, d), lambda i, j: (i, 0))]
    else:
        prompt_tiles, x_specs = _two_source_specs(rows, tm, d, 2)
    return pl.pallas_call(
        functools.partial(_norm_mm_kernel, prompt_tiles=prompt_tiles),
        grid=(m // tm, n // tn),
        in_specs=x_specs + [
                  pl.BlockSpec((1, d), lambda i, j: (0, 0)),
                  pl.BlockSpec((None, tn, d), lambda i, j: (layer, j, 0))],
        out_specs=pl.BlockSpec((tm, tn), lambda i, j: (i, j)),
        out_shape=jax.ShapeDtypeStruct((m, n), F32),
        scratch_shapes=[pltpu.VMEM((tm, d), BF16)],
        compiler_params=_cparams(("parallel", "arbitrary")),
        name=name,
    )(*xs, g, wt)


FFN_TN = 512
FFN_RB = 256


def _ffn_kernel(x_ref, g_ref, wa_ref, wb_ref, wd_ref, cw_ref, *rest, tm, tiles_per_seq, seq_rows):
    if tiles_per_seq:
        o_ref, fcv_ref, xn_ref, ext_ref, carry_ref = rest
    else:
        st_ref, o_ref, fcv_ref, xn_ref, ext_ref = rest
    i, j = pl.program_id(0), pl.program_id(1)
    rb_rows = min(FFN_RB, tm)

    @pl.when(j == 0)
    def _():
        x = x_ref[...]
        xn_ref[...] = _rms(x, g_ref[...]).astype(BF16)
        o_ref[...] = x

    if tiles_per_seq:
        @pl.when(i % tiles_per_seq == 0)
        def _():
            ext_ref[0:HALO] = jnp.zeros((HALO, FFN_TN), F32)

        @pl.when(i % tiles_per_seq != 0)
        def _():
            ext_ref[0:HALO] = carry_ref[j]

    def up(rb):
        r0 = rb * rb_rows
        xn = xn_ref[r0:r0 + rb_rows]
        x0 = _dot(xn, wa_ref[...])
        ext_ref[HALO + r0:HALO + r0 + rb_rows] = x0
        return x0, _dot(xn, wb_ref[...])

    def gated(rb, x0, b):
        r0 = rb * rb_rows
        x1 = ext_ref[HALO - 1 + r0:HALO - 1 + r0 + rb_rows]
        x2 = ext_ref[HALO - 2 + r0:HALO - 2 + r0 + rb_rows]
        if not tiles_per_seq:
            loc = lax.broadcasted_iota(jnp.int32, (rb_rows, 1), 0) % seq_rows
            seq0 = r0 // seq_rows
            prev = lambda row: jnp.concatenate(
                [jnp.broadcast_to(st_ref[seq0 + q, row:row + 1, :], (seq_rows, FFN_TN))
                 for q in range(rb_rows // seq_rows)], axis=0)
            s1, s2 = prev(HALO - 1), prev(HALO - 2)
            x1 = jnp.where(loc == 0, s1, x1)
            x2 = jnp.where(loc == 0, s2, jnp.where(loc == 1, s1, x2))
        a = x2 * cw_ref[0:1, :] + x1 * cw_ref[1:2, :] + x0 * cw_ref[2:3, :]
        return (0.5 * a * (1.0 + lax.erf(a * float(np.sqrt(0.5)))) * b).astype(BF16)

    n_rb = tm // rb_rows
    pending = up(0)
    for rb in range(n_rb):
        following = up(rb + 1) if rb + 1 < n_rb else None
        act = gated(rb, *pending)
        o_ref[rb * rb_rows:(rb + 1) * rb_rows] += _dot(act, wd_ref[...])
        pending = following

    if tiles_per_seq:
        carry_ref[j] = ext_ref[tm:tm + HALO]
        fcv_ref[0] = ext_ref[tm:tm + HALO]
    else:
        for q in range(tm // seq_rows):
            fcv_ref[q] = ext_ref[(q + 1) * seq_rows:(q + 1) * seq_rows + HALO]


def _ffn(x, g, wa_wb, wd, layer, cw, n_seq, seq_rows, row0, st=None):
    m = x.shape[0]
    nj = D_FF // FFN_TN
    if st is None:
        tm = _pick(seq_rows, (1024, 512, 256, 128, 64))
        tiles_per_seq = seq_rows // tm
        seqs_per_tile = 1
    else:
        tm = _pick(n_seq * seq_rows, (1024, 512, 256, 128, 64))
        assert tm % seq_rows == 0 and min(FFN_RB, tm) % seq_rows == 0
        tiles_per_seq = 0
        seqs_per_tile = tm // seq_rows
    assert row0 % tm == 0
    i0 = row0 // tm
    n_tiles = n_seq * seq_rows // tm
    in_specs = [pl.BlockSpec((tm, D_MODEL), lambda i, j: (i0 + i, 0)),
                pl.BlockSpec((1, D_MODEL), lambda i, j: (0, 0)),
                pl.BlockSpec((None, D_MODEL, FFN_TN), lambda i, j: (layer, 0, j)),
                pl.BlockSpec((None, D_MODEL, FFN_TN), lambda i, j: (layer, 0, nj + j)),
                pl.BlockSpec((None, FFN_TN, D_MODEL), lambda i, j: (layer, j, 0)),
                pl.BlockSpec((FFN_CONV, FFN_TN), lambda i, j: (0, j))]
    args = [x, g, wa_wb, wa_wb, wd, cw]
    scratch = [pltpu.VMEM((tm, D_MODEL), BF16), pltpu.VMEM((HALO + tm, FFN_TN), F32)]
    if st is None:
        scratch.append(pltpu.VMEM((nj, HALO, FFN_TN), F32))
        fcv_rows = n_tiles
    else:
        in_specs.append(pl.BlockSpec((seqs_per_tile, HALO, FFN_TN), lambda i, j: (i, 0, j)))
        args.append(st)
        fcv_rows = n_seq
    return pl.pallas_call(
        functools.partial(_ffn_kernel, tm=tm, tiles_per_seq=tiles_per_seq, seq_rows=seq_rows),
        grid=(n_tiles, nj),
        in_specs=in_specs,
        out_specs=[pl.BlockSpec((tm, D_MODEL), lambda i, j: (i0 + i, 0)),
                   pl.BlockSpec((seqs_per_tile, HALO, FFN_TN), lambda i, j: (i, 0, j))],
        out_shape=[jax.ShapeDtypeStruct((m, D_MODEL), F32),
                   jax.ShapeDtypeStruct((fcv_rows, HALO, D_FF), F32)],
        scratch_shapes=scratch,
        input_output_aliases={0: 0},
        compiler_params=_cparams(("arbitrary", "arbitrary")),
        name="ffn_prompt" if st is None else "ffn_sample",
    )(*args)


MERGE_NC = 512


def _merge_kernel(*refs, prompt_tiles):
    x_refs = refs[:-10]
    g0_ref, g1_ref, g2_ref, o0_ref, o1_ref, o2_ref, wb_ref, wo_ref, out_ref, mg_ref = refs[-10:]
    i = pl.program_id(0)
    branches = ((g0_ref, o0_ref), (g1_ref, o1_ref), (g2_ref, o2_ref))
    for nc in range(D_MODEL // MERGE_NC):
        cs = slice(nc * MERGE_NC, (nc + 1) * MERGE_NC)
        acc = None
        for j, (g_ref, o_ref) in enumerate(branches):
            t = _sigmoid(g_ref[:, cs]) * _dot(o_ref[...].reshape(-1, MIX_W), wb_ref[j, :, cs])
            acc = t if acc is None else acc + t
        mg_ref[:, cs] = acc.astype(BF16)
    for nc in range(D_MODEL // MERGE_NC):
        cs = slice(nc * MERGE_NC, (nc + 1) * MERGE_NC)
        x = x_refs[0][:, cs]
        if prompt_tiles is not None:
            x = jnp.where(i < prompt_tiles, x, x_refs[1][:, cs])
        out_ref[:, cs] = x + _dot(mg_ref[...], wo_ref[:, cs])


def _merge(seqs, mp, ms, xs, proj, o_r, o_m, o_t, wb, wo, layer):
    m = mp + ms
    tm = _pick(np.gcd(mp // LANES, ms // LANES), (256, 128, 64))
    if len(xs) == 1:
        prompt_tiles, x_specs = None, [pl.BlockSpec((tm, D_MODEL), lambda i: (i, 0))]
    else:
        prompt_tiles, x_specs = _two_source_specs((mp, ms), tm, D_MODEL, 1)
    resident = dict(pipeline_mode=pl.Buffered(1))

    def mixer_block(i):
        lane, blk = seqs.lane_row_block(i, tm, mp, ms)
        return blk, lane, 0, 0

    mixer_spec = pl.BlockSpec((tm // CHUNK, None, CHUNK, MIX_W), mixer_block)
    return pl.pallas_call(
        functools.partial(_merge_kernel, prompt_tiles=prompt_tiles),
        grid=(m // tm,),
        in_specs=x_specs + [
                  pl.BlockSpec((tm, GATE_BLK), lambda i: (i, 0)),
                  pl.BlockSpec((tm, GATE_BLK), lambda i: (i, 1)),
                  pl.BlockSpec((tm, GATE_BLK), lambda i: (i, 2)),
                  mixer_spec, mixer_spec, mixer_spec,
                  pl.BlockSpec((None, 3, MIX_W, D_MODEL), lambda i: (layer, 0, 0, 0), **resident),
                  pl.BlockSpec((None, D_MODEL, D_MODEL), lambda i: (layer, 0, 0), **resident)],
        out_specs=pl.BlockSpec((tm, D_MODEL), lambda i: (i, 0)),
        out_shape=jax.ShapeDtypeStruct((m, D_MODEL), F32),
        scratch_shapes=[pltpu.VMEM((tm, D_MODEL), BF16)],
        compiler_params=_cparams(("parallel",)),
        name="merge",
    )(*xs, proj, proj, proj, o_r, o_m, o_t, wb, wo)


def _ple_kernel(x_ref, g_ref, wg_ref, p_ref, wp_ref, gf_ref, *o_refs, prompt_tiles):
    x = x_ref[...]
    gate = _sigmoid(_dot(_rms(x, g_ref[...]).astype(BF16), wg_ref[...]))
    y = x + _dot(p_ref[...].astype(BF16), wp_ref[...]) * gate
    if prompt_tiles is None:
        o_refs[0][...] = y
        return
    y = _rms(y, gf_ref[...])
    i = pl.program_id(0)

    @pl.when(i < prompt_tiles)
    def _():
        o_refs[0][...] = y

    @pl.when(i >= prompt_tiles)
    def _():
        o_refs[1][...] = y


def _ple(x, g, wg, p, wp, layer, gf, split_rows=None):
    m = x.shape[0]
    tm = _pick(m if split_rows is None else np.gcd(*split_rows), (512, 256, 128, 64))
    resident = dict(pipeline_mode=pl.Buffered(1))
    if split_rows is None:
        prompt_tiles = None
        out_specs = pl.BlockSpec((tm, D_MODEL), lambda i: (i, 0))
        out_shape = jax.ShapeDtypeStruct((m, D_MODEL), F32)
    else:
        prompt_tiles = split_rows[0] // tm
        out_specs = [pl.BlockSpec((tm, D_MODEL), lambda i: (jnp.minimum(i, prompt_tiles - 1), 0)),
                     pl.BlockSpec((tm, D_MODEL), lambda i: (jnp.maximum(i - prompt_tiles, 0), 0))]
        out_shape = [jax.ShapeDtypeStruct((r, D_MODEL), F32) for r in split_rows]
    return pl.pallas_call(
        functools.partial(_ple_kernel, prompt_tiles=prompt_tiles),
        grid=(m // tm,),
        in_specs=[pl.BlockSpec((tm, D_MODEL), lambda i: (i, 0)),
                  pl.BlockSpec((1, D_MODEL), lambda i: (0, 0)),
                  pl.BlockSpec((None, D_MODEL, D_MODEL), lambda i: (layer, 0, 0), **resident),
                  pl.BlockSpec((tm, PLE_DIM), lambda i: (i, 0)),
                  pl.BlockSpec((None, PLE_DIM, D_MODEL), lambda i: (layer, 0, 0), **resident),
                  pl.BlockSpec((1, D_MODEL), lambda i: (0, 0))],
        out_specs=out_specs,
        out_shape=out_shape,
        compiler_params=_cparams(("arbitrary",)),
        name="ple" if split_rows is None else "ple_final",
    )(x, g, wg, p, wp, gf)


LANES = 2


class _Seqs:
    def __init__(self, bp, tp, bs, ts):
        assert tp % CHUNK == 0 and ts % CHUNK == 0 and bp % LANES == 0 and bs % LANES == 0
        self.cp, self.cs = tp // CHUNK, ts // CHUNK
        self.bp, self.bs = bp, bs
        self.npc = bp * self.cp
        self.lane_p = self.npc // LANES
        self.lane_s = bs * self.cs // LANES
        self.n_steps = self.lane_p + self.lane_s
        self.lane_seqs = (bp + bs) // LANES

    def _split(self, s):
        in_p = s < self.lane_p
        ss = s - self.lane_p
        pos = jnp.where(in_p, s % self.cp, ss % self.cs)
        return in_p, ss, pos

    def chunk(self, lane, s):
        in_p, ss, _ = self._split(s)
        return jnp.where(in_p, lane * self.lane_p + s, self.npc + lane * self.lane_s + ss)

    def seq(self, lane, s):
        in_p, ss, _ = self._split(s)
        return jnp.where(in_p, lane * (self.bp // LANES) + s // self.cp,
                         self.bp + lane * (self.bs // LANES) + ss // self.cs)

    def lane_seq(self, s):
        in_p, ss, _ = self._split(s)
        return jnp.where(in_p, s // self.cp, self.bp // LANES + ss // self.cs)

    def first(self, s):
        return self._split(s)[2] == 0

    def last(self, s):
        in_p, _, pos = self._split(s)
        return pos == jnp.where(in_p, self.cp - 1, self.cs - 1)

    def rope_block(self, s):
        in_p, _, pos = self._split(s)
        return jnp.where(in_p, pos, self.cp + pos)

    def of_chunk(self, c):
        in_p = c < self.npc
        cc = c - self.npc
        return (jnp.where(in_p, c // self.lane_p, cc // self.lane_s),
                jnp.where(in_p, c % self.lane_p, self.lane_p + cc % self.lane_s))

    def unlane(self, per_lane):
        hp = self.bp // LANES
        return jnp.concatenate([a[:hp] for a in per_lane] + [a[hp:] for a in per_lane], axis=0)

    def lane_row_block(self, i, tm, mp, ms):
        r0 = i * tm
        in_p = r0 < mp
        rs = r0 - mp
        lane = jnp.where(in_p, r0 // (mp // LANES), rs // (ms // LANES))
        local = jnp.where(in_p, r0 % (mp // LANES), mp // LANES + rs % (ms // LANES))
        return lane, local // tm


def _shifted(ext_ref, x, n_prev):
    ext_ref[HALO:HALO + CHUNK] = x
    return [ext_ref[HALO - j:HALO - j + CHUNK] for j in range(1, n_prev + 1)]


def _roll_halo(ext_ref):
    ext_ref[0:HALO] = ext_ref[CHUNK:CHUNK + HALO]


def _bd_mask():
    r = lax.broadcasted_iota(jnp.int32, (GROUP_W, GROUP_W), 0) // RWKV_HEAD
    c = lax.broadcasted_iota(jnp.int32, (GROUP_W, GROUP_W), 1) // RWKV_HEAD
    return r == c


def _rwkv_chunk(al, bt, r, k, v, lg, logd, sbd_ref):
    groups = range(al.shape[1] // GROUP_W)
    bdm = _bd_mask()
    tt = lax.broadcasted_iota(jnp.int32, (CHUNK, GROUP_W), 0)
    ss = lax.broadcasted_iota(jnp.int32, (CHUNK, GROUP_W), 1) % RWKV_HEAD
    low_s, low_i = tt > ss, tt >= ss
    eye = jnp.where(tt == ss, 1.0, 0.0)

    def sl(x, i):
        return x[:, i * GROUP_W:(i + 1) * GROUP_W]

    def bd(x16):
        return jnp.where(bdm, jnp.concatenate([x16] * RWKV_GROUP, axis=0), jnp.zeros((), BF16))

    def stack16(a, b):
        return jnp.concatenate([a, b], axis=0).astype(BF16)

    e_in = jnp.exp(lg)
    e_inv = jnp.exp(-lg)
    at = al * jnp.exp(lg - logd)
    rt = r * e_in
    kh = k * e_inv
    bh = bt * e_inv
    e_end = e_in[CHUNK - 1:CHUNK, :]
    bh_end = (bh * e_end).astype(BF16)
    kh_end = (kh * e_end).astype(BF16)
    v16 = v.astype(BF16)

    lhs = [stack16(sl(at, i), sl(rt, i)) for i in groups]
    x = [_dot_nt(lhs[i], jnp.concatenate([bd(sl(bh, i).astype(BF16)), bd(sl(kh, i).astype(BF16))], axis=0))
         for i in groups]
    p = [jnp.where(low_s, x[i][:CHUNK, :GROUP_W], 0.0) for i in groups]
    q = [jnp.where(low_s, x[i][:CHUNK, GROUP_W:], 0.0) for i in groups]
    rb = [jnp.where(low_i, x[i][CHUNK:, :GROUP_W], 0.0) for i in groups]
    rk = [jnp.where(low_i, x[i][CHUNK:, GROUP_W:], 0.0) for i in groups]
    p16 = [p[i].astype(BF16) for i in groups]
    a = [_dot(p16[i], bd(p16[i])) for i in groups]
    t = [eye + p[i] for i in groups]
    for _ in range(4):
        res = [_dot(stack16(t[i], a[i]), bd(a[i].astype(BF16))) for i in groups]
        t = [t[i] + res[i][:CHUNK] for i in groups]
        a = [res[i][CHUNK:] for i in groups]
    t = [t[i] + _dot(t[i].astype(BF16), bd(a[i].astype(BF16))) for i in groups]
    xm = [_dot_nt(lhs[i], sbd_ref[i].astype(BF16)) for i in groups]
    xv = [_dot(stack16(q[i], rk[i]), bd(sl(v16, i))) for i in groups]
    u16 = [_dot(t[i].astype(BF16), bd((xm[i][:CHUNK] + xv[i][:CHUNK]).astype(BF16))).astype(BF16) for i in groups]
    y = [xm[i][CHUNK:] + xv[i][CHUNK:] + _dot(rb[i].astype(BF16), bd(u16[i])) for i in groups]
    for i in groups:
        upd = _dot_tn(jnp.concatenate([u16[i], sl(v16, i)], axis=0),
                      jnp.concatenate([sl(bh_end, i), sl(kh_end, i)], axis=0))
        sbd_ref[i] = sbd_ref[i] * sl(e_end, i) + jnp.where(bdm, upd, 0.0)
    return jnp.concatenate(y, axis=1)


RWKV_LANE_IN, RWKV_SHARED, RWKV_LANE_OUT, RWKV_LANE_SCRATCH = 5, 12, 3, 2


def _rwkv_kernel(seqs, *refs):
    lane_in = [refs[l * RWKV_LANE_IN:(l + 1) * RWKV_LANE_IN] for l in range(LANES)]
    refs = refs[LANES * RWKV_LANE_IN:]
    (mu_ref, mut_ref, w0_ref, w2_ref, a0_ref, a2_ref, g2_ref, kk_ref, ka_ref, rk_ref, lnw_ref, lnb_ref) = \
        refs[:RWKV_SHARED]
    o_ref = refs[RWKV_SHARED]
    refs = refs[RWKV_SHARED + 1:]
    lane_out = [refs[l * RWKV_LANE_OUT:(l + 1) * RWKV_LANE_OUT] for l in range(LANES)]
    refs = refs[LANES * RWKV_LANE_OUT:]
    lane_scr = [refs[l * RWKV_LANE_SCRATCH:(l + 1) * RWKV_LANE_SCRATCH] for l in range(LANES)]
    sbd_ref = refs[LANES * RWKV_LANE_SCRATCH]
    s = pl.program_id(0)
    n_heads = MIX_W // RWKV_HEAD

    def state_blocks(l):
        for h in range(n_heads):
            g, hh = divmod(h, RWKV_GROUP)
            yield h, l * N_GROUPS + g, slice(hh * RWKV_HEAD, (hh + 1) * RWKV_HEAD)

    @pl.when(seqs.first(s))
    def _():
        sbd_ref[...] = jnp.zeros_like(sbd_ref)
        for l in range(LANES):
            _, _, sh0_ref, sht0_ref, s0_ref = lane_in[l]
            ext_ref, extt_ref = lane_scr[l]
            ext_ref[0:HALO] = sh0_ref[0]
            extt_ref[0:HALO] = sht0_ref[0]
            for h, g, blk in state_blocks(l):
                sbd_ref[g, blk, blk] = s0_ref[0, h]

    ones_bd = jnp.where(_bd_mask(), 1.0, 0.0).astype(BF16)
    ti = lax.broadcasted_iota(jnp.int32, (CHUNK, CHUNK), 0)
    si = lax.broadcasted_iota(jnp.int32, (CHUNK, CHUNK), 1)
    tril16 = jnp.where(ti >= si, 1.0, 0.0).astype(BF16)

    def head_sum(x):
        x16 = x.astype(BF16)
        return jnp.concatenate(
            [_dot(x16[:, i * GROUP_W:(i + 1) * GROUP_W], ones_bd) for i in range(N_GROUPS)], axis=1)

    def prepare(l):
        z_ref, tail_ref = lane_in[l][:2]
        ext_ref, extt_ref = lane_scr[l]
        z = z_ref[...]
        (zp,) = _shifted(ext_ref, z, 1)
        zs = z + mu_ref[...] * (zp - z)
        _roll_halo(ext_ref)
        zt = tail_ref[...]
        (ztp,) = _shifted(extt_ref, zt, 1)
        lora = zt + mut_ref[...] * (ztp - zt)
        _roll_halo(extt_ref)
        r = zs[:, 0:MIX_W]
        k = zs[:, MIX_W:2 * MIX_W]
        v = zs[:, 2 * MIX_W:3 * MIX_W]
        zw = lora[:, ZW_OFF:ZW_OFF + LANE]
        za = lora[:, ZA_OFF:ZA_OFF + LANE]
        zg = lora[:, ZG_OFF:ZG_OFF + 2 * LANE]
        logd = -float(np.exp(-0.5)) * _sigmoid(w0_ref[...] + _dot(jnp.tanh(zw).astype(BF16), w2_ref[...]))
        a = _sigmoid(a0_ref[...] + _dot(za.astype(BF16), a2_ref[...]))
        g = _dot(_sigmoid(zg).astype(BF16), g2_ref[...])
        kk = k * kk_ref[...]
        kk = kk * lax.rsqrt(jnp.maximum(head_sum(kk * kk), 1e-24))
        k2 = k * (1.0 + (a - 1.0) * ka_ref[...])
        lg = _split_dot_left(tril16, logd)
        return dict(al=-kk, bt=kk * a, r=r, k=k2, v=v, lg=lg, logd=logd, g=g)

    pre = [prepare(l) for l in range(LANES)]
    both = lambda name: jnp.concatenate([p[name] for p in pre], axis=1)
    y_all = _rwkv_chunk(both("al"), both("bt"), both("r"), both("k"), both("v"), both("lg"), both("logd"), sbd_ref)

    inv_n = 1.0 / RWKV_HEAD
    for l in range(LANES):
        p = pre[l]
        y = y_all[:, l * MIX_W:(l + 1) * MIX_W]
        d = y - head_sum(y) * inv_n
        var = head_sum(d * d) * inv_n
        yn = d * lax.rsqrt(var + RWKV_LN_EPS) * lnw_ref[...] + lnb_ref[...]
        bonus = head_sum(p["r"] * p["k"] * rk_ref[...]) * p["v"]
        o_ref[0, l] = ((yn + bonus) * p["g"]).astype(BF16)

    @pl.when(seqs.last(s))
    def _():
        for l in range(LANES):
            sh_out_ref, sht_out_ref, s_out_ref = lane_out[l]
            ext_ref, extt_ref = lane_scr[l]
            sh_out_ref[0] = ext_ref[0:HALO]
            sht_out_ref[0] = extt_ref[0:HALO]
            for h, g, blk in state_blocks(l):
                s_out_ref[0, h] = sbd_ref[g, blk, blk]


def _mixer_out_spec():
    return pl.BlockSpec((1, LANES, CHUNK, MIX_W), lambda s: (s, 0, 0, 0))


def _mixer_out_shape(seqs):
    return jax.ShapeDtypeStruct((seqs.n_steps, LANES, CHUNK, MIX_W), BF16)


def _lane_specs(make):
    return [spec for lane in range(LANES) for spec in make(lane)]


def _rwkv(seqs, proj, sh0, sht0, s0, pr):
    n_heads = MIX_W // RWKV_HEAD
    row = lambda w: pl.BlockSpec((1, w), lambda s: (0, 0))
    lora_w = lambda k: pl.BlockSpec((k, MIX_W), lambda s: (0, 0))
    tail_blk = IN_COLS_PAD // TAIL_BLK - 1

    def lane_in(l):
        st3 = lambda w: pl.BlockSpec((1, HALO, w), lambda s: (seqs.seq(l, s), 0, 0))
        return [pl.BlockSpec((CHUNK, RKV_BLK), lambda s: (seqs.chunk(l, s), 2)),
                pl.BlockSpec((CHUNK, TAIL_BLK), lambda s: (seqs.chunk(l, s), tail_blk)),
                st3(RKV_BLK), st3(TAIL_BLK),
                pl.BlockSpec((1, n_heads, RWKV_HEAD, RWKV_HEAD), lambda s: (seqs.seq(l, s), 0, 0, 0))]

    def lane_out(l):
        st3 = lambda w: pl.BlockSpec((1, HALO, w), lambda s: (seqs.lane_seq(s), 0, 0))
        return [st3(RKV_BLK), st3(TAIL_BLK),
                pl.BlockSpec((1, n_heads, RWKV_HEAD, RWKV_HEAD), lambda s: (seqs.lane_seq(s), 0, 0, 0))]

    lane_shapes = [jax.ShapeDtypeStruct((seqs.lane_seqs, HALO, RKV_BLK), F32),
                   jax.ShapeDtypeStruct((seqs.lane_seqs, HALO, TAIL_BLK), F32),
                   jax.ShapeDtypeStruct((seqs.lane_seqs, n_heads, RWKV_HEAD, RWKV_HEAD), F32)]
    return pl.pallas_call(
        functools.partial(_rwkv_kernel, seqs),
        grid=(seqs.n_steps,),
        in_specs=_lane_specs(lane_in) + [
            row(RKV_BLK), row(TAIL_BLK), row(MIX_W), lora_w(LANE), row(MIX_W), lora_w(LANE), lora_w(2 * LANE),
            row(MIX_W), row(MIX_W), row(MIX_W), row(MIX_W), row(MIX_W)],
        out_specs=[_mixer_out_spec()] + _lane_specs(lane_out),
        out_shape=[_mixer_out_shape(seqs)] + lane_shapes * LANES,
        scratch_shapes=[sh for _ in range(LANES) for sh in (pltpu.VMEM((HALO + CHUNK, RKV_BLK), F32),
                                                             pltpu.VMEM((HALO + CHUNK, TAIL_BLK), F32))]
        + [pltpu.VMEM((LANES * N_GROUPS, GROUP_W, GROUP_W), F32)],
        compiler_params=_cparams(("arbitrary",)),
        name="rwkv",
    )(*([proj, proj, sh0, sht0, s0] * LANES), pr["mu"], pr["mut"], pr["w0"], pr["w2"], pr["a0"], pr["a2"], pr["g2"],
      pr["kk"], pr["ka"], pr["rk"], pr["lnw"], pr["lnb"])


def _head_norm(x):
    mu = jnp.mean(x, axis=-1, keepdims=True)
    d = x - mu
    return d * lax.rsqrt(jnp.mean(d * d, axis=-1, keepdims=True) + HEAD_NORM_EPS)


def _mlstm_kernel(seqs, q_ref, k_ref, v_ref, op_ref, tail_ref, cv0_ref, c0_ref, n0_ref, m0_ref,
                  cw_ref, gb_ref, nw_ref,
                  o_ref, cv_out_ref, c_ref, n_ref, m_ref,
                  ext_ref):
    _, step = seqs.of_chunk(pl.program_id(0))

    @pl.when(seqs.first(step))
    def _():
        ext_ref[0:HALO] = cv0_ref[0]
        c_ref[...] = c0_ref[...]
        n_ref[...] = n0_ref[...]
        m_ref[...] = m0_ref[...]

    x0 = jnp.concatenate([q_ref[...], k_ref[...]], axis=1)
    x1, x2, x3 = _shifted(ext_ref, x0, MLSTM_CONV - 1)
    qk = x3 * cw_ref[0:1, :] + x2 * cw_ref[1:2, :] + x1 * cw_ref[2:3, :] + x0 * cw_ref[3:4, :]
    _roll_halo(ext_ref)
    qk = qk * _sigmoid(qk)
    q_all = qk[:, :MIX_W]
    k_all = qk[:, MIX_W:] * (M_HEAD ** -0.5)

    gates = tail_ref[:, MGATE_BLK * LANE:(MGATE_BLK + 1) * LANE] + gb_ref[...]
    lsf = jnp.minimum(gates, 0.0) - jnp.log(1.0 + jnp.exp(-jnp.abs(gates)))
    ti = lax.broadcasted_iota(jnp.int32, (CHUNK, CHUNK), 0)
    si = lax.broadcasted_iota(jnp.int32, (CHUNK, CHUNK), 1)
    tril = ti >= si
    bcum_col = _dot_f32(jnp.where(tril, 1.0, 0.0), lsf)
    ig_rows = gates.T[MGATE_LANE:MGATE_LANE + HALO]
    bcum_rows = _dot_f32(lsf.T[MGATE_LANE:MGATE_LANE + HALO], jnp.where(ti <= si, 1.0, 0.0))
    lane = lax.broadcasted_iota(jnp.int32, (1, LANE), 1)
    m_row = m_ref[0]
    heads = range(M_HEADS)
    hsl = [slice(h * M_HEAD, (h + 1) * M_HEAD) for h in heads]

    bc = [bcum_col[:, MGATE_LANE + M_HEADS + h:MGATE_LANE + M_HEADS + h + 1] for h in heads]
    ic = [gates[:, MGATE_LANE + h:MGATE_LANE + h + 1] for h in heads]
    log_inter = [bc[h] + m_row[:, h:h + 1] for h in heads]
    log_intra = [jnp.where(tril, bc[h] - bcum_rows[M_HEADS + h:M_HEADS + h + 1, :] + ig_rows[h:h + 1, :], NEG_BIG)
                 for h in heads]
    m_t = [jnp.maximum(log_inter[h], jnp.max(log_intra[h], axis=-1, keepdims=True)) for h in heads]
    w_inter = [jnp.exp(log_inter[h] - m_t[h]) for h in heads]
    w_intra = [jnp.exp(log_intra[h] - m_t[h]) for h in heads]
    m_end = [m_t[h][CHUNK - 1:CHUNK, :] for h in heads]
    w_end = [jnp.exp(bc[h][CHUNK - 1:CHUNK, :] - bc[h] + ic[h] - m_end[h]) for h in heads]
    g_end = [w_inter[h][CHUNK - 1:CHUNK, :] for h in heads]
    q = [q_all[:, hsl[h]] for h in heads]
    k = [k_all[:, hsl[h]] for h in heads]
    v = [v_ref[:, hsl[h]] for h in heads]
    q16 = [q[h].astype(BF16) for h in heads]
    k16 = [k[h].astype(BF16) for h in heads]
    v16 = [v[h].astype(BF16) for h in heads]
    s = [_dot_nt(q16[h], k16[h]) * w_intra[h] for h in heads]
    qc = [_dot(q16[h], c_ref[0, h].astype(BF16)) for h in heads]
    sv = [_dot(s[h].astype(BF16), v16[h]) for h in heads]
    kv = [_dot_tn(k16[h], (w_end[h] * v[h]).astype(BF16)) for h in heads]
    m_new_row = m_row
    for h in heads:
        n_h = n_ref[0, h:h + 1, :]
        num = w_inter[h] * qc[h] + sv[h]
        den = w_inter[h] * jnp.sum(q[h] * n_h, axis=-1, keepdims=True) + jnp.sum(s[h], axis=-1, keepdims=True)
        hh = num * (1.0 / jnp.maximum(jnp.abs(den), jnp.exp(-m_t[h])))
        c_ref[0, h] = g_end[h] * c_ref[0, h] + kv[h]
        n_ref[0, h:h + 1, :] = g_end[h] * n_h + jnp.sum(w_end[h] * k[h], axis=0, keepdims=True)
        m_new_row = jnp.where(lane == h, m_end[h], m_new_row)
        o_ref[0, 0, :, hsl[h]] = (_head_norm(hh) * nw_ref[:, hsl[h]] * _sigmoid(op_ref[:, hsl[h]])).astype(BF16)

    m_ref[0] = m_new_row

    @pl.when(seqs.last(step))
    def _():
        cv_out_ref[0] = ext_ref[0:HALO]


def _mlstm(seqs, proj, cv0, c0, n0, m0, pr):
    n_seq = seqs.bp + seqs.bs
    blk = lambda j: pl.BlockSpec((CHUNK, MIX_BLK), lambda c: (c, j))
    sq = lambda c: seqs.seq(*seqs.of_chunk(c))
    st_cv = pl.BlockSpec((1, HALO, 2 * MIX_W), lambda c: (sq(c), 0, 0))
    st_c = pl.BlockSpec((1, M_HEADS, M_HEAD, M_HEAD), lambda c: (sq(c), 0, 0, 0))
    st_n = pl.BlockSpec((1, M_HEADS, M_HEAD), lambda c: (sq(c), 0, 0))
    st_m = pl.BlockSpec((1, 1, LANE), lambda c: (sq(c), 0, 0))
    return pl.pallas_call(
        functools.partial(_mlstm_kernel, seqs),
        grid=(seqs.npc + seqs.bs * seqs.cs,),
        in_specs=[blk(9), blk(10), blk(11), blk(12),
                  pl.BlockSpec((CHUNK, TAIL_BLK), lambda c: (c, IN_COLS_PAD // TAIL_BLK - 1)),
                  st_cv, st_c, st_n, st_m,
                  pl.BlockSpec((MLSTM_CONV, 2 * MIX_W), lambda c: (0, 0)),
                  pl.BlockSpec((1, LANE), lambda c: (0, 0)),
                  pl.BlockSpec((1, MIX_W), lambda c: (0, 0))],
        out_specs=[pl.BlockSpec((1, 1, CHUNK, MIX_W), lambda c: seqs.of_chunk(c)[::-1] + (0, 0)),
                   st_cv, st_c, st_n, st_m],
        out_shape=[_mixer_out_shape(seqs),
                   jax.ShapeDtypeStruct((n_seq, HALO, 2 * MIX_W), F32),
                   jax.ShapeDtypeStruct((n_seq, M_HEADS, M_HEAD, M_HEAD), F32),
                   jax.ShapeDtypeStruct((n_seq, M_HEADS, M_HEAD), F32),
                   jax.ShapeDtypeStruct((n_seq, 1, LANE), F32)],
        scratch_shapes=[pltpu.VMEM((HALO + CHUNK, 2 * MIX_W), F32)],
        compiler_params=_cparams(("arbitrary",)),
        name="mlstm",
    )(proj, proj, proj, proj, proj, cv0, c0, n0, m0, pr["cw"], pr["gb"], pr["nw"])


RET_LANE_IN = 5


def _ret_kernel(seqs, *refs):
    lane_in = [refs[l * RET_LANE_IN:(l + 1) * RET_LANE_IN] for l in range(LANES)]
    refs = refs[LANES * RET_LANE_IN:]
    cos_ref, sin_ref, o_ref = refs[:3]
    r_refs = refs[3:]
    s = pl.program_id(0)

    @pl.when(seqs.first(s))
    def _():
        for l in range(LANES):
            r_refs[l][...] = lane_in[l][4][...]

    cos, sin = cos_ref[...], sin_ref[...]
    half = M_HEAD // 2
    ti = lax.broadcasted_iota(jnp.int32, (CHUNK, CHUNK), 0)
    si = lax.broadcasted_iota(jnp.int32, (CHUNK, CHUNK), 1)
    diff = (ti - si).astype(F32)
    t_col = lax.broadcasted_iota(jnp.int32, (CHUNK, 1), 0).astype(F32)

    def rot(u):
        u1, u2 = u[:, :half], u[:, half:]
        return jnp.concatenate([u1 * cos - u2 * sin, u1 * sin + u2 * cos], axis=1)

    items = [(l, h) for l in range(LANES) for h in range(M_HEADS)]
    idx = range(len(items))
    hsl = [slice(h * M_HEAD, (h + 1) * M_HEAD) for _, h in items]
    log_gamma = [float(np.log(1.0 - 2.0 ** (-5.0 - h))) for _, h in items]
    decay_in = {h: jnp.where(diff >= 0, jnp.exp(float(np.log(1.0 - 2.0 ** (-5.0 - h))) * jnp.maximum(diff, 0.0)), 0.0)
                for h in range(M_HEADS)}
    q16 = [rot(lane_in[l][0][:, hsl[i]]).astype(BF16) for i, (l, _) in enumerate(items)]
    k = [rot(lane_in[l][1][:, hsl[i]]) * (M_HEAD ** -0.5) for i, (l, _) in enumerate(items)]
    v16 = [lane_in[l][2][:, hsl[i]].astype(BF16) for i, (l, _) in enumerate(items)]
    sm = [_dot_nt(q16[i], k[i].astype(BF16)) * decay_in[h] for i, (_, h) in enumerate(items)]
    qr = [_dot(q16[i], r_refs[l][0, h].astype(BF16)) for i, (l, h) in enumerate(items)]
    sv = [_dot(sm[i].astype(BF16), v16[i]) for i in idx]
    kv = [_dot_tn((k[i] * jnp.exp(log_gamma[i] * (CHUNK - 1.0 - t_col))).astype(BF16), v16[i]) for i in idx]
    for i, (l, h) in enumerate(items):
        o = sv[i] + jnp.exp(log_gamma[i] * (t_col + 1.0)) * qr[i]
        r_refs[l][0, h] = float(np.exp(log_gamma[i] * CHUNK)) * r_refs[l][0, h] + kv[i]
        gt = lane_in[l][3][:, hsl[i]]
        o_ref[0, l, :, hsl[i]] = (_head_norm(o) * (gt * _sigmoid(gt))).astype(BF16)


def _retention(seqs, proj, cos, sin, r0):
    rope = pl.BlockSpec((CHUNK, M_HEAD // 2), lambda s: (seqs.rope_block(s), 0))
    st_r = lambda index: pl.BlockSpec((1, M_HEADS, M_HEAD, M_HEAD), lambda s: (index(s), 0, 0, 0))

    def lane_in(l):
        blk = lambda j: pl.BlockSpec((CHUNK, MIX_BLK), lambda s: (seqs.chunk(l, s), j))
        return [blk(13), blk(14), blk(15), blk(16), st_r(lambda s: seqs.seq(l, s))]

    return pl.pallas_call(
        functools.partial(_ret_kernel, seqs),
        grid=(seqs.n_steps,),
        in_specs=_lane_specs(lane_in) + [rope, rope],
        out_specs=[_mixer_out_spec()] + [st_r(seqs.lane_seq)] * LANES,
        out_shape=[_mixer_out_shape(seqs)]
        + [jax.ShapeDtypeStruct((seqs.lane_seqs, M_HEADS, M_HEAD, M_HEAD), F32)] * LANES,
        compiler_params=_cparams(("arbitrary",)),
        name="retention",
    )(*([proj] * 4 + [r0]) * LANES, cos, sin)


def _halo_rows(prev, n_prompt):
    k = prev.shape[1]
    return jnp.pad(prev, ((n_prompt, 0), (HALO - k, 0), (0, 0)))


def _with_prompt(state, n_prompt):
    return jnp.pad(state, ((n_prompt, 0),) + ((0, 0),) * (state.ndim - 1))


def _rope_tables(tp, ts):
    half = M_HEAD // 2
    freq = ROPE_BASE ** (-jnp.arange(half, dtype=F32) / half)
    pos = jnp.concatenate([jnp.arange(tp), PAST_LEN + jnp.arange(ts)]).astype(F32)
    ang = pos[:, None] * freq
    return jnp.cos(ang), jnp.sin(ang)


def _tail_layout(lora, gates=None):
    lead = lora.shape[:-1]
    z = lambda n: jnp.zeros(lead + (n,), lora.dtype)
    mid = z(LANE - ZA_W) if gates is None else jnp.concatenate(
        [z(MGATE_LANE - ZA_W), gates, z(LANE - MGATE_LANE - gates.shape[-1])], axis=-1)
    return jnp.concatenate([lora[..., :ZW_W], z(LANE - ZW_W), lora[..., ZW_W:ZW_W + ZA_W], mid,
                            lora[..., ZW_W + ZA_W:], z(2 * LANE - ZG_W)], axis=-1)


def _tail_lora(t):
    return jnp.concatenate([t[..., ZW_OFF:ZW_OFF + ZW_W], t[..., ZA_OFF:ZA_OFF + ZA_W], t[..., ZG_OFF:ZG_OFF + ZG_W]],
                           axis=-1)


def _in_col_segments():
    rw = 3 * MIX_W + LORA_W
    mb = rw
    tb = mb + 4 * MIX_W + 2 * M_HEADS
    gb = tb + 4 * MIX_W
    tail = IN_COLS_PAD - TAIL_BLK
    return [(0, gb, 3 * D_MODEL), (3 * D_MODEL, 0, 3 * MIX_W),
            (3 * D_MODEL + 3 * MIX_W, mb, 4 * MIX_W), (3 * D_MODEL + 7 * MIX_W, tb, 4 * MIX_W),
            (tail + ZW_OFF, 3 * MIX_W, ZW_W), (tail + ZA_OFF, 3 * MIX_W + ZW_W, ZA_W),
            (tail + MGATE_BLK * LANE + MGATE_LANE, mb + 4 * MIX_W, 2 * M_HEADS),
            (tail + ZG_OFF, 3 * MIX_W + ZW_W + ZA_W, ZG_W)]


PERMUTE_LANES = LANE
PERMUTE_RUN = 512


def _permute_in_kernel(w_ref, o_ref):
    o_ref[IN_COLS_PAD - TAIL_BLK:, :] = jnp.zeros((TAIL_BLK, PERMUTE_LANES), BF16)
    for dst, src, width in _in_col_segments():
        for off in range(0, width, PERMUTE_RUN):
            n = min(PERMUTE_RUN, width - off)
            o_ref[dst + off:dst + off + n, :] = w_ref[src + off:src + off + n, :].astype(BF16)


def _permute_in_cols(wt):
    depth, n, d = wt.shape
    return pl.pallas_call(
        _permute_in_kernel,
        grid=(depth, d // PERMUTE_LANES),
        in_specs=[pl.BlockSpec((None, n, PERMUTE_LANES), lambda l, i: (l, 0, i))],
        out_specs=pl.BlockSpec((None, IN_COLS_PAD, PERMUTE_LANES), lambda l, i: (l, 0, i)),
        out_shape=jax.ShapeDtypeStruct((depth, IN_COLS_PAD, d), BF16),
        compiler_params=_cparams(("parallel", "parallel")),
        name="permute_w_in",
    )(wt)


def _lora_rows(w, k):
    return jnp.pad(w, ((0, k - w.shape[0]), (0, 0))).astype(BF16)


def kernel(x_prompt, x_sample, state_rwkv_shift, state_rwkv_wkv, state_mlstm_conv, state_mlstm_c, state_mlstm_n, state_mlstm_m, state_ret, state_ffn_conv, p_prompt, p_sample, norm_mix, w_in, rwkv_mu, rwkv_w0, rwkv_w2, rwkv_a0, rwkv_a2, rwkv_g2, rwkv_kk, rwkv_ka, rwkv_rk, rwkv_lnw, rwkv_lnb, mlstm_conv, mlstm_bi, mlstm_bf, mlstm_nw, w_branch, w_out, norm_ffn, ffn_up, ffn_conv, ffn_down, norm_ple, ple_proj, ple_gate, norm_final):
    bp, tp, _ = x_prompt.shape
    bs, ts, _ = x_sample.shape
    mp, ms = bp * tp, bs * ts
    depth = w_in.shape[0]
    seqs = _Seqs(bp, tp, bs, ts)
    xs = (x_prompt.reshape(mp, D_MODEL), x_sample.reshape(ms, D_MODEL))
    cos, sin = _rope_tables(tp, ts)
    row = lambda a: a.reshape(1, -1)
    new_states = []
    w_in_p = _permute_in_cols(jnp.swapaxes(w_in, 1, 2))
    w_branch16, w_out16, ffn_up16, ffn_down16, ple_gate16, ple_proj16 = (
        w.astype(BF16) for w in (w_branch, w_out, ffn_up, ffn_down, ple_gate, ple_proj))
    for i in range(depth):
        proj = _norm_matmul(xs, row(norm_mix[i]), w_in_p, i, "in_proj")

        shift = state_rwkv_shift[i][:, None, :]
        mu = rwkv_mu[i]
        rwkv_pr = dict(
            mu=row(mu[:3 * MIX_W]), mut=row(_tail_layout(mu[3 * MIX_W:])),
            w0=row(rwkv_w0[i]), w2=_lora_rows(rwkv_w2[i], LANE),
            a0=row(rwkv_a0[i]), a2=_lora_rows(rwkv_a2[i], LANE),
            g2=_lora_rows(rwkv_g2[i], 2 * LANE),
            kk=row(rwkv_kk[i]), ka=row(rwkv_ka[i]), rk=row(rwkv_rk[i]),
            lnw=row(rwkv_lnw[i]), lnb=row(rwkv_lnb[i]))
        o_r, *rwkv_st = _rwkv(
            seqs, proj,
            _halo_rows(shift[:, :, :3 * MIX_W], bp),
            _halo_rows(_tail_layout(shift[:, :, 3 * MIX_W:]), bp),
            _with_prompt(state_rwkv_wkv[i], bp), rwkv_pr)

        gate_bias = jnp.pad(jnp.concatenate([mlstm_bi[i], mlstm_bf[i]]),
                            (MGATE_LANE, LANE - MGATE_LANE - 2 * M_HEADS))
        mlstm_pr = dict(cw=mlstm_conv[i], gb=row(gate_bias), nw=row(mlstm_nw[i]))
        m0 = jnp.pad(state_mlstm_m[i], ((0, 0), (0, LANE - M_HEADS)))[:, None, :]
        o_m, cv_new, c_new, n_new, m_new = _mlstm(
            seqs, proj, _halo_rows(state_mlstm_conv[i], bp), _with_prompt(state_mlstm_c[i], bp),
            _with_prompt(state_mlstm_n[i], bp), _with_prompt(m0, bp), mlstm_pr)

        o_t, *ret_st = _retention(seqs, proj, cos, sin, _with_prompt(state_ret[i], bp))
        sh_new, sht_new, wkv_new = (seqs.unlane(rwkv_st[j::3]) for j in range(3))
        ret_new = seqs.unlane(ret_st)

        x = _merge(seqs, mp, ms, xs, proj, o_r, o_m, o_t, w_branch16, w_out16, i)

        x, fcv_p = _ffn(x, row(norm_ffn[i]), ffn_up16, ffn_down16, i, ffn_conv[i], bp, tp, 0)
        x, fcv_s = _ffn(x, row(norm_ffn[i]), ffn_up16, ffn_down16, i, ffn_conv[i], bs, ts, mp,
                        st=jnp.pad(state_ffn_conv[i], ((0, 0), (HALO - FFN_CONV + 1, 0), (0, 0))))
        fcv_new = jnp.concatenate([fcv_p.reshape(bp, -1, HALO, D_FF)[:, -1], fcv_s], axis=0)

        p = jnp.concatenate([p_prompt[i].reshape(mp, PLE_DIM), p_sample[i].reshape(ms, PLE_DIM)], axis=0)
        x = _ple(x, row(norm_ple[i]), ple_gate16, p, ple_proj16, i,
                 row(norm_final), split_rows=(mp, ms) if i == depth - 1 else None)
        xs = (x,)

        shift_new = jnp.concatenate([sh_new[:, HALO - 1, :], _tail_lora(sht_new[:, HALO - 1, :])], axis=-1)
        new_states.append((shift_new, wkv_new, cv_new[:, HALO - MLSTM_CONV + 1:, :], c_new, n_new,
                           m_new[:, 0, :M_HEADS], ret_new, fcv_new[:, HALO - FFN_CONV + 1:, :]))

    stacked = [jnp.stack(s, axis=0) for s in zip(*new_states)]
    y_prompt = x[0].reshape(bp, tp, D_MODEL)
    y_sample = x[1].reshape(bs, ts, D_MODEL)
    return (y_prompt, y_sample) + tuple(s[:, :bp] for s in stacked) + tuple(s[:, bp:] for s in stacked)
```
